```python
import jax, jax.numpy as jnp
from jax import lax
import numpy as np

D_MODEL = 1024
BATCH = 4
SEQ = 8192
DEPTH = 2

GRID_W = 64
CTX_LEN = 256
FOUR_GROUPS = 4
FOUR_GC = D_MODEL // 8
FOUR_WIDTH = FOUR_GROUPS * FOUR_GC
NA_HEADS = 8
NA_HEAD_DIM = D_MODEL // 16
NA_WIDTH = NA_HEADS * NA_HEAD_DIM
NA_WIN_ROWS = 8
NA_WIN_COLS = 16
POOL_SIZES = (2, 4, 8, 16)
POOL_GC = D_MODEL // len(POOL_SIZES)
MLP_HIDDEN = 4 * D_MODEL
N_MOD = 6
RMS_EPS = 1e-6
N_EVEN = (DEPTH + 1) // 2
N_ODD = DEPTH // 2

kernel_name = "hybrid_fnet_natten_pool_dit_block"


def rms_norm(x, g):
    xf = x.astype(jnp.float32)
    y = xf * lax.rsqrt(jnp.mean(xf * xf, axis=-1, keepdims=True) + RMS_EPS)
    return (y * g.astype(jnp.float32)).astype(x.dtype)


def modulate(xn, shift, scale):
    return xn * (1.0 + scale) + shift


def sq_relu_mlp(u, w1, w2):
    a = jnp.maximum(u @ w1, 0.0)
    return (a * a) @ w2


def fourier_mix(f, w_four):
    B, N, _ = f.shape
    fg = f.reshape(B, N, FOUR_GROUPS, FOUR_GC).transpose(0, 2, 1, 3)
    mixed = jnp.fft.fft2(fg.astype(jnp.float32), norm="ortho").real.astype(f.dtype)
    out = jnp.einsum('bgnc,gcd->bngd', mixed, w_four)
    return out.reshape(B, N, FOUR_WIDTH)


def neighbourhood_attention(q, k, v, k_ctx, v_ctx, rpb):
    B, N, H, dh = q.shape
    rows = N // GRID_W
    wr = min(NA_WIN_ROWS, rows)

    def to_grid(a):
        return a.reshape(B, rows, GRID_W, H, dh).transpose(0, 3, 1, 2, 4)

    qg, kg, vg = to_grid(q), to_grid(k), to_grid(v)
    r = jnp.arange(rows)
    row_start = jnp.clip(r - wr // 2, 0, rows - wr)
    row_idx = row_start[:, None] + jnp.arange(wr)[None, :]
    kb = kg[:, :, row_idx]
    vb = vg[:, :, row_idx]
    cidx = jnp.arange(GRID_W)
    col_start = jnp.clip(cidx - NA_WIN_COLS // 2, 0, GRID_W - NA_WIN_COLS)
    col_mask = (cidx[None, :] >= col_start[:, None]) & (cidx[None, :] < col_start[:, None] + NA_WIN_COLS)
    dr = row_idx - r[:, None] + (NA_WIN_ROWS - 1)
    dc = jnp.clip(cidx[None, :] - cidx[:, None] + (NA_WIN_COLS - 1), 0, 2 * NA_WIN_COLS - 2)
    bias = rpb[:, dr[:, None, :, None], dc[None, :, None, :]]
    scale = dh ** -0.5
    s_win = jnp.einsum('bhrqd,bhrjkd->bhrqjk', qg, kb).astype(jnp.float32) * scale + bias.astype(jnp.float32)
    s_win = jnp.where(col_mask[:, None, :], s_win, -jnp.inf)
    s_win = s_win.reshape(B, H, rows, GRID_W, wr * GRID_W)
    s_ctx = jnp.einsum('bhrqd,blhd->bhrql', qg, k_ctx).astype(jnp.float32) * scale
    p = jax.nn.softmax(jnp.concatenate([s_win, s_ctx], axis=-1), axis=-1)
    p_win = p[..., :wr * GRID_W].reshape(B, H, rows, GRID_W, wr, GRID_W).astype(v.dtype)
    p_ctx = p[..., wr * GRID_W:].astype(v.dtype)
    o = jnp.einsum('bhrqjk,bhrjkd->bhrqd', p_win, vb) + jnp.einsum('bhrql,blhd->bhrqd', p_ctx, v_ctx)
    return o.transpose(0, 2, 3, 1, 4).reshape(B, N, H * dh)


def context_attention(qc, kc, vc):
    B, L, H, dh = qc.shape
    s = jnp.einsum('blhd,bmhd->bhlm', qc, kc).astype(jnp.float32) * (dh ** -0.5)
    p = jax.nn.softmax(s, axis=-1).astype(vc.dtype)
    return jnp.einsum('bhlm,bmhd->blhd', p, vc).reshape(B, L, H * dh)


def split_heads(a):
    return a.reshape(a.shape[0], a.shape[1], NA_HEADS, NA_HEAD_DIM)


def even_mixer(u, uc, w_in, w_four, q_g, k_g, rpb, w_out, update_ctx):
    p = u @ w_in
    f, q, k, v = jnp.split(p, [FOUR_WIDTH, FOUR_WIDTH + NA_WIDTH, FOUR_WIDTH + 2 * NA_WIDTH], axis=-1)
    q = rms_norm(split_heads(q), q_g)
    k = rms_norm(split_heads(k), k_g)
    v = split_heads(v)
    kvc = uc @ w_in[:, FOUR_WIDTH + NA_WIDTH:]
    kc, vc = jnp.split(kvc, [NA_WIDTH], axis=-1)
    kc = rms_norm(split_heads(kc), k_g)
    vc = split_heads(vc)
    o_four = fourier_mix(f, w_four)
    o_na = neighbourhood_attention(q, k, v, kc, vc, rpb)
    y = jnp.concatenate([o_four, o_na], axis=-1) @ w_out
    if not update_ctx:
        return y, None
    fc, qc = jnp.split(uc @ w_in[:, :FOUR_WIDTH + NA_WIDTH], [FOUR_WIDTH], axis=-1)
    qc = rms_norm(split_heads(qc), q_g)
    yc = jnp.concatenate([fourier_mix(fc, w_four), context_attention(qc, kc, vc)], axis=-1) @ w_out
    return y, yc


def multiscale_pool(u):
    N = u.shape[1]
    uf = u.astype(jnp.float32)
    zeros = jnp.zeros((uf.shape[0], 1, uf.shape[2]), jnp.float32)
    cs = jnp.concatenate([zeros, jnp.cumsum(uf, axis=1)], axis=1)
    t = jnp.arange(N)
    outs = []
    for gi, w in enumerate(POOL_SIZES):
        sl = slice(gi * POOL_GC, (gi + 1) * POOL_GC)
        lo = jnp.clip(t - w // 2, 0, N)
        hi = jnp.clip(t + w - w // 2, 0, N)
        csg = cs[..., sl]
        s = jnp.take(csg, hi, axis=1) - jnp.take(csg, lo, axis=1)
        cnt = (hi - lo).astype(jnp.float32)[None, :, None]
        outs.append(s / cnt - uf[..., sl])
    return jnp.concatenate(outs, axis=-1).astype(u.dtype)


def pool_mixer(u, w_pool, pool_scale):
    B, N, _ = u.shape
    pooled = multiscale_pool(u).reshape(B, N, len(POOL_SIZES), POOL_GC)
    y = jnp.einsum('bngc,gcd->bngd', pooled, w_pool).reshape(B, N, D_MODEL)
    return y * pool_scale


def setup_inputs(seed: int = 0) -> dict:
    key = jax.random.key(seed)
    ks = jax.random.split(key, 20)
    f32 = jnp.float32
    n = lambda k, s: jax.random.normal(k, s, f32)
    return {
        "x": n(ks[0], (BATCH, SEQ, D_MODEL)),
        "c": n(ks[1], (BATCH, D_MODEL)),
        "ctx": n(ks[2], (BATCH, CTX_LEN, D_MODEL)),
        "c_ctx": n(ks[3], (D_MODEL,)),
        "w_mod": n(ks[4], (DEPTH, D_MODEL, N_MOD * D_MODEL)) * D_MODEL ** -0.5,
        "b_mod": n(ks[5], (DEPTH, N_MOD * D_MODEL)) * 0.02,
        "norm1_g": 1.0 + 0.02 * n(ks[6], (DEPTH, D_MODEL)),
        "norm2_g": 1.0 + 0.02 * n(ks[7], (DEPTH, D_MODEL)),
        "w_in_even": n(ks[8], (N_EVEN, D_MODEL, FOUR_WIDTH + 3 * NA_WIDTH)) * D_MODEL ** -0.5,
        "w_four": n(ks[9], (N_EVEN, FOUR_GROUPS, FOUR_GC, FOUR_GC)) * FOUR_GC ** -0.5,
        "q_norm_g": 1.0 + 0.02 * n(ks[10], (N_EVEN, NA_HEAD_DIM)),
        "k_norm_g": 1.0 + 0.02 * n(ks[11], (N_EVEN, NA_HEAD_DIM)),
        "rpb": 0.1 * n(ks[12], (N_EVEN, NA_HEADS, 2 * NA_WIN_ROWS - 1, 2 * NA_WIN_COLS - 1)),
        "w_out_even": n(ks[13], (N_EVEN, FOUR_WIDTH + NA_WIDTH, D_MODEL)) * (FOUR_WIDTH + NA_WIDTH) ** -0.5,
        "w_pool": n(ks[14], (N_ODD, len(POOL_SIZES), POOL_GC, POOL_GC)) * POOL_GC ** -0.5,
        "pool_scale": 1.0 + 0.1 * n(ks[15], (N_ODD, D_MODEL)),
        "w_mlp1": n(ks[16], (DEPTH, D_MODEL, MLP_HIDDEN)) * D_MODEL ** -0.5,
        "w_mlp2": n(ks[17], (DEPTH, MLP_HIDDEN, D_MODEL)) * MLP_HIDDEN ** -0.5,
    }


def reference(x, c, ctx, c_ctx, w_mod, b_mod, norm1_g, norm2_g, w_in_even, w_four, q_norm_g, k_norm_g,
              rpb, w_out_even, w_pool, pool_scale, w_mlp1, w_mlp2):
    h, hc = x, ctx
    sc = jax.nn.silu(c)
    scc = jax.nn.silu(c_ctx)
    for l in range(DEPTH):
        update_ctx = any(j % 2 == 0 for j in range(l + 1, DEPTH))
        need_ctx_in = (l % 2 == 0) or update_ctx
        mod = (sc @ w_mod[l] + b_mod[l])[:, None, :]
        sh1, sc1, g1, sh2, sc2, g2 = jnp.split(mod, N_MOD, axis=-1)
        u = modulate(rms_norm(h, norm1_g[l]), sh1, sc1)
        if need_ctx_in:
            mod_c = scc @ w_mod[l] + b_mod[l]
            csh1, csc1, cg1, csh2, csc2, cg2 = jnp.split(mod_c, N_MOD, axis=-1)
            uc = modulate(rms_norm(hc, norm1_g[l]), csh1, csc1)
        if l % 2 == 0:
            e = l // 2
            y, yc = even_mixer(u, uc, w_in_even[e], w_four[e], q_norm_g[e], k_norm_g[e], rpb[e],
                               w_out_even[e], update_ctx)
        else:
            o = l // 2
            y = pool_mixer(u, w_pool[o], pool_scale[o])
            yc = pool_mixer(uc, w_pool[o], pool_scale[o]) if update_ctx else None
        h = h + g1 * y
        h = h + g2 * sq_relu_mlp(modulate(rms_norm(h, norm2_g[l]), sh2, sc2), w_mlp1[l], w_mlp2[l])
        if update_ctx:
            hc = hc + cg1 * yc
            hc = hc + cg2 * sq_relu_mlp(modulate(rms_norm(hc, norm2_g[l]), csh2, csc2), w_mlp1[l], w_mlp2[l])
    return h
```

```python
import functools

import numpy as np
import jax
import jax.numpy as jnp
from jax import lax
from jax.experimental import pallas as pl
from jax.experimental.pallas import tpu as pltpu

D_MODEL = 1024
GRID_W = 64
FOUR_GROUPS = 4
FOUR_GC = 128
FOUR_WIDTH = 512
NA_HEADS = 8
NA_HEAD_DIM = 64
NA_WIDTH = 512
NA_WIN_ROWS = 8
NA_WIN_COLS = 16
POOL_SIZES = (2, 4, 8, 16)
POOL_GC = 256
MLP_HIDDEN = 4096
N_MOD = 6
RMS_EPS = 1e-6

VMEM_LIMIT_BYTES = 56 * 1024 * 1024
SUBLANES = 8

F32 = jnp.float32
BF16 = jnp.bfloat16
MASK_VALUE = -1e30

FFT_N1 = 128
FFT_N2 = 64
Y1_PITCH = 2 * FFT_N1 + 8
Z_PITCH = FFT_N1 + 8

NA_Q_ROWS = 8
NA_BAND_ROWS = 16


def _dot(a, b):
    return jnp.dot(a, b, preferred_element_type=F32)


def _dot_f32(a, b):
    return jnp.dot(a, b, preferred_element_type=F32, precision=lax.Precision.HIGHEST)


def _dot_nt(a, b):
    return lax.dot_general(a, b, (((1,), (1,)), ((), ())), preferred_element_type=F32)


def _params(*sem):
    return pltpu.CompilerParams(dimension_semantics=sem, vmem_limit_bytes=VMEM_LIMIT_BYTES)


def _resident(shape, index_map):
    return pl.BlockSpec(shape, index_map, pipeline_mode=pl.Buffered(1))


def _mod_kernel(c_ref, w_ref, b_ref, o_ref):
    c = c_ref[...]
    s = c * (1.0 / (1.0 + jnp.exp(-c)))
    o_ref[...] = _dot_f32(s, w_ref[...]) + b_ref[...]


def _modulation(c_rows, w_mod, b_mod):
    depth = w_mod.shape[0]
    rows = c_rows.shape[0]
    tn = D_MODEL
    return pl.pallas_call(
        _mod_kernel,
        grid=(depth, N_MOD * D_MODEL // tn),
        in_specs=[
            pl.BlockSpec((rows, D_MODEL), lambda l, j: (0, 0)),
            pl.BlockSpec((None, D_MODEL, tn), lambda l, j: (l, 0, j)),
            pl.BlockSpec((None, 1, tn), lambda l, j: (l, 0, j)),
        ],
        out_specs=pl.BlockSpec((None, rows, tn), lambda l, j: (l, 0, j)),
        out_shape=jax.ShapeDtypeStruct((depth, rows, N_MOD * D_MODEL), F32),
        compiler_params=_params("parallel", "parallel"),
        name="modulation",
    )(c_rows, w_mod, b_mod.reshape(depth, 1, N_MOD * D_MODEL))


def _norm_modulate(x, gain, shift, scale):
    ms = jnp.mean(x * x, axis=-1, keepdims=True)
    return (x * lax.rsqrt(ms + RMS_EPS) * gain) * (1.0 + scale) + shift


def _head_norm(a, gain, seg_ref):
    sq = (a * a).astype(BF16)
    half = seg_ref.shape[0]
    ms = jnp.concatenate([_dot(sq[:, :half], seg_ref[...]), _dot(sq[:, half:], seg_ref[...])], axis=1)
    return a * lax.rsqrt(ms * (1.0 / NA_HEAD_DIM) + RMS_EPS) * gain


def _inproj_kernel(x_ref, mod_ref, g_ref, w_ref, qg_ref, kg_ref, seg_ref, f_ref, q_ref, k_ref, v_ref):
    u = _norm_modulate(x_ref[...], g_ref[...], mod_ref[:, 0:D_MODEL], mod_ref[:, D_MODEL:2 * D_MODEL])
    p = _dot(u.astype(BF16), w_ref[...])
    f_ref[...] = p[:, :FOUR_WIDTH]
    q = p[:, FOUR_WIDTH:FOUR_WIDTH + NA_WIDTH]
    k = p[:, FOUR_WIDTH + NA_WIDTH:FOUR_WIDTH + 2 * NA_WIDTH]
    q_ref[...] = (_head_norm(q, qg_ref[...], seg_ref) * (NA_HEAD_DIM ** -0.5)).astype(BF16)
    k_ref[...] = _head_norm(k, kg_ref[...], seg_ref).astype(BF16)
    v_ref[...] = p[:, FOUR_WIDTH + 2 * NA_WIDTH:].astype(BF16)


def _in_projection(x, mod_rows, row_of_batch, norm_g, w_in, q_g, k_g, seg, tm):
    b, n, _ = x.shape
    width = w_in.shape[1]
    tok = lambda bi, j: (bi, j, 0)
    const2 = lambda bi, j: (0, 0)
    out_tok = lambda w: pl.BlockSpec((None, tm, w), tok)
    return pl.pallas_call(
        _inproj_kernel,
        grid=(b, n // tm),
        in_specs=[
            pl.BlockSpec((None, tm, D_MODEL), tok),
            pl.BlockSpec((None, 1, N_MOD * D_MODEL), lambda bi, j: (row_of_batch(bi), 0, 0)),
            pl.BlockSpec((1, D_MODEL), const2),
            _resident((D_MODEL, width), const2),
            pl.BlockSpec((1, NA_WIDTH), const2),
            pl.BlockSpec((1, NA_WIDTH), const2),
            pl.BlockSpec(seg.shape, const2),
        ],
        out_specs=[out_tok(FOUR_WIDTH), out_tok(NA_WIDTH), out_tok(NA_WIDTH), out_tok(NA_WIDTH)],
        out_shape=[
            jax.ShapeDtypeStruct((b, n, FOUR_WIDTH), F32),
            jax.ShapeDtypeStruct((b, n, NA_WIDTH), BF16),
            jax.ShapeDtypeStruct((b, n, NA_WIDTH), BF16),
            jax.ShapeDtypeStruct((b, n, NA_WIDTH), BF16),
        ],
        compiler_params=_params("parallel", "parallel"),
        name="in_projection",
    )(x, mod_rows, norm_g, w_in, q_g, k_g, seg)


def _fft_tables(n):
    n1, n2 = FFT_N1, FFT_N2
    assert n1 * n2 == n
    k1 = np.arange(n1)[None, :, None]
    pos = (n2 * np.arange(n1)[None, None, :] + np.arange(n2)[:, None, None])
    ang = 2.0 * np.pi * ((k1 * pos) % n) / n
    l1 = np.concatenate([np.cos(ang), -np.sin(ang)], axis=1)
    a2 = 2.0 * np.pi * ((np.arange(n2)[:, None] * np.arange(n2)[None, :]) % n2) / n2
    c2, s2 = np.cos(a2), np.sin(a2)
    l2 = np.block([[c2, s2], [-s2, c2]])
    ac = 2.0 * np.pi * ((np.arange(FOUR_GC)[:, None] * np.arange(FOUR_GC)[None, :]) % FOUR_GC) / FOUR_GC
    cs = np.concatenate([np.cos(ac), np.sin(ac)], axis=0)
    return l1.astype(np.float32), l2.astype(np.float32), cs.astype(np.float32)


def _fft_kernel(f_ref, l1_ref, l2_ref, cs_ref, w_ref, o_ref, y1_ref, z_ref, *, ortho):
    n1, n2 = FFT_N1, FFT_N2

    def stage1(j, carry):
        r = f_ref[pl.ds(j, n1, stride=n2), :].astype(BF16)
        y1_ref[pl.ds(pl.multiple_of(j * Y1_PITCH, SUBLANES), 2 * n1), :] = _dot(l1_ref[j], r)
        return carry

    lax.fori_loop(0, n2, stage1, 0)

    l2 = l2_ref[...]

    def stage2(k1, carry):
        yr = y1_ref[pl.ds(k1, n2, stride=Y1_PITCH), :]
        yi = y1_ref[pl.ds(n1 + k1, n2, stride=Y1_PITCH), :]
        z = _dot(l2, jnp.concatenate([yr, yi], axis=0).astype(BF16))
        z_ref[0, pl.ds(k1, n2, stride=Z_PITCH), :] = z[:n2]
        z_ref[1, pl.ds(k1, n2, stride=Z_PITCH), :] = z[n2:]
        return carry

    lax.fori_loop(0, n1, stage2, 0)

    wf = (_dot_f32(cs_ref[...], w_ref[...]) * ortho).astype(BF16)
    wr, wi = wf[:FOUR_GC], wf[FOUR_GC:]

    def stage3(k2, carry):
        base = pl.multiple_of(k2 * Z_PITCH, SUBLANES)
        zr = z_ref[0, pl.ds(base, n1), :].astype(BF16)
        zi = z_ref[1, pl.ds(base, n1), :].astype(BF16)
        o_ref[pl.ds(pl.multiple_of(k2 * n1, n1), n1), :] = (_dot(zr, wr) + _dot(zi, wi)).astype(o_ref.dtype)
        return carry

    lax.fori_loop(0, n2, stage3, 0)


def _fourier_mix(f, w_four):
    b, n, _ = f.shape
    l1, l2, cs = _fft_tables(n)
    l1 = jnp.asarray(l1).astype(BF16)
    l2 = jnp.asarray(l2).astype(BF16)
    cs = jnp.asarray(cs)
    ortho = float(1.0 / np.sqrt(n * FOUR_GC))
    slab = lambda bi, g: (bi, 0, g)
    return pl.pallas_call(
        functools.partial(_fft_kernel, ortho=ortho),
        grid=(b, FOUR_GROUPS),
        in_specs=[
            pl.BlockSpec((None, n, FOUR_GC), slab),
            _resident(l1.shape, lambda bi, g: (0, 0, 0)),
            _resident(l2.shape, lambda bi, g: (0, 0)),
            _resident(cs.shape, lambda bi, g: (0, 0)),
            pl.BlockSpec((None, FOUR_GC, FOUR_GC), lambda bi, g: (g, 0, 0)),
        ],
        out_specs=pl.BlockSpec((None, n, FOUR_GC), slab),
        out_shape=jax.ShapeDtypeStruct((b, n, FOUR_WIDTH), BF16),
        scratch_shapes=[
            pltpu.VMEM((FFT_N2 * Y1_PITCH, FOUR_GC), F32),
            pltpu.VMEM((2, FFT_N2 * Z_PITCH, FOUR_GC), F32),
        ],
        compiler_params=_params("parallel", "parallel"),
        name="fourier_mix",
    )(f, l1, l2, cs, w_four)


def _band_start(i, rows):
    return jnp.clip(NA_Q_ROWS * i - (NA_BAND_ROWS - NA_Q_ROWS) // 2, 0, rows - NA_BAND_ROWS)


def _natten_bias_tables(rpb, rows):
    groups = rows // NA_Q_ROWS
    tables_dr, tables_ok = [], []
    qc = np.arange(GRID_W)
    col_start = np.clip(qc - NA_WIN_COLS // 2, 0, GRID_W - NA_WIN_COLS)
    col_ok = (qc[None, :] >= col_start[:, None]) & (qc[None, :] < col_start[:, None] + NA_WIN_COLS)
    dc = np.clip(qc[None, :] - qc[:, None] + (NA_WIN_COLS - 1), 0, 2 * NA_WIN_COLS - 2)
    for i in (0, groups // 2, groups - 1):
        band0 = int(np.clip(NA_Q_ROWS * i - (NA_BAND_ROWS - NA_Q_ROWS) // 2, 0, rows - NA_BAND_ROWS))
        qr = NA_Q_ROWS * i + np.arange(NA_Q_ROWS)
        kr = band0 + np.arange(NA_BAND_ROWS)
        row_start = np.clip(qr - NA_WIN_ROWS // 2, 0, rows - NA_WIN_ROWS)
        row_ok = (kr[None, :] >= row_start[:, None]) & (kr[None, :] < row_start[:, None] + NA_WIN_ROWS)
        dr = np.clip(kr[None, :] - qr[:, None] + (NA_WIN_ROWS - 1), 0, 2 * NA_WIN_ROWS - 2)
        ok = row_ok[:, None, :, None] & col_ok[None, :, None, :]
        idx = dr[:, None, :, None] * (2 * NA_WIN_COLS - 1) + dc[None, :, None, :]
        nq, nk = NA_Q_ROWS * GRID_W, NA_BAND_ROWS * GRID_W
        tables_dr.append(np.broadcast_to(idx, ok.shape).reshape(nq, nk))
        tables_ok.append(ok.reshape(nq, nk))
    idx = jnp.asarray(np.stack(tables_dr).astype(np.int32))
    ok = jnp.asarray(np.stack(tables_ok))
    flat = rpb.reshape(rpb.shape[0], -1).astype(F32)
    return jnp.where(ok[None], jnp.take(flat, idx, axis=1), MASK_VALUE)


def _natten_kernel(q_ref, k_ref, v_ref, kc_ref, vc_ref, tab_ref, o_ref, *, rows):
    i = pl.program_id(2)
    nk = NA_BAND_ROWS * GRID_W
    tok0 = pl.multiple_of(_band_start(i, rows) * GRID_W, GRID_W * 4)
    q2 = q_ref[...]
    kb = k_ref[pl.ds(tok0, nk), :]
    vb = v_ref[pl.ds(tok0, nk), :]
    kc = kc_ref[...]
    vc = vc_ref[...]
    first_head = lax.broadcasted_iota(jnp.int32, (1, 2 * NA_HEAD_DIM), 1) < NA_HEAD_DIM
    outs = []
    for hh in range(2):
        qh = jnp.where(first_head if hh == 0 else jnp.logical_not(first_head), q2, jnp.zeros_like(q2))
        s_win = _dot_nt(qh, kb) + tab_ref[hh]
        s_ctx = _dot_nt(qh, kc)
        m = jnp.maximum(jnp.max(s_win, axis=-1, keepdims=True), jnp.max(s_ctx, axis=-1, keepdims=True))
        p_win = jnp.exp(s_win - m)
        p_ctx = jnp.exp(s_ctx - m)
        l = jnp.sum(p_win, axis=-1, keepdims=True) + jnp.sum(p_ctx, axis=-1, keepdims=True)
        o = _dot(p_win.astype(BF16), vb) + _dot(p_ctx.astype(BF16), vc)
        outs.append(o / l)
    o_ref[...] = jnp.where(first_head, outs[0], outs[1]).astype(o_ref.dtype)


def _neighbourhood_attention(q, k, v, kc, vc, rpb):
    b, n, _ = q.shape
    rows = n // GRID_W
    groups = rows // NA_Q_ROWS
    ctx_len = kc.shape[1]
    tq = NA_Q_ROWS * GRID_W
    tables = _natten_bias_tables(rpb, rows)
    lanes = 2 * NA_HEAD_DIM
    kind = lambda i: jnp.where(i == 0, 0, jnp.where(i == groups - 1, 2, 1))
    seq = lambda bi, hp, i: (bi, 0, hp)
    return pl.pallas_call(
        functools.partial(_natten_kernel, rows=rows),
        grid=(b, NA_HEADS // 2, groups),
        in_specs=[
            pl.BlockSpec((None, tq, lanes), lambda bi, hp, i: (bi, i, hp)),
            pl.BlockSpec((None, n, lanes), seq),
            pl.BlockSpec((None, n, lanes), seq),
            pl.BlockSpec((None, ctx_len, lanes), seq),
            pl.BlockSpec((None, ctx_len, lanes), seq),
            pl.BlockSpec((2, None, tq, NA_BAND_ROWS * GRID_W), lambda bi, hp, i: (hp, kind(i), 0, 0)),
        ],
        out_specs=pl.BlockSpec((None, tq, lanes), lambda bi, hp, i: (bi, i, hp)),
        out_shape=jax.ShapeDtypeStruct((b, n, NA_WIDTH), BF16),
        compiler_params=_params("parallel", "parallel", "arbitrary"),
        name="neighbourhood_attention",
    )(q, k, v, kc, vc, tables)


def _mod_chunk(mod_ref, idx):
    return mod_ref[:, idx * D_MODEL:(idx + 1) * D_MODEL]


def _residual_mlp(h, y, mod_ref, g2n_ref, w1_ref, w2_ref):
    h1 = h + _mod_chunk(mod_ref, 2) * y
    u = _norm_modulate(h1, g2n_ref[...], _mod_chunk(mod_ref, 3), _mod_chunk(mod_ref, 4)).astype(BF16)
    acc = None
    chunk = D_MODEL
    for c in range(MLP_HIDDEN // chunk):
        a = jnp.maximum(_dot(u, w1_ref[:, c * chunk:(c + 1) * chunk]), 0.0)
        part = _dot((a * a).astype(BF16), w2_ref[c * chunk:(c + 1) * chunk, :])
        acc = part if acc is None else acc + part
    return h1 + _mod_chunk(mod_ref, 5) * acc


def _even_tail_kernel(h_ref, of_ref, on_ref, mod_ref, wo_ref, g2n_ref, w1_ref, w2_ref, o_ref):
    y = _dot(of_ref[...], wo_ref[:FOUR_WIDTH, :]) + _dot(on_ref[...], wo_ref[FOUR_WIDTH:, :])
    o_ref[...] = _residual_mlp(h_ref[...], y, mod_ref, g2n_ref, w1_ref, w2_ref)


def _even_tail(h, o_four, o_na, mod_rows, w_out, norm2_g, w1, w2, tm):
    b, n, _ = h.shape
    tok = lambda bi, j: (bi, j, 0)
    const2 = lambda bi, j: (0, 0)
    return pl.pallas_call(
        _even_tail_kernel,
        grid=(b, n // tm),
        in_specs=[
            pl.BlockSpec((None, tm, D_MODEL), tok),
            pl.BlockSpec((None, tm, FOUR_WIDTH), tok),
            pl.BlockSpec((None, tm, NA_WIDTH), tok),
            pl.BlockSpec((None, 1, N_MOD * D_MODEL), lambda bi, j: (bi, 0, 0)),
            _resident(w_out.shape, const2),
            pl.BlockSpec((1, D_MODEL), const2),
            _resident(w1.shape, const2),
            _resident(w2.shape, const2),
        ],
        out_specs=pl.BlockSpec((None, tm, D_MODEL), tok),
        out_shape=jax.ShapeDtypeStruct(h.shape, F32),
        compiler_params=_params("parallel", "parallel"),
        name="even_tail",
    )(h, o_four, o_na, mod_rows, w_out, norm2_g, w1, w2)


HALO = 8


def _shift_rows(a, s):
    return pltpu.roll(a, s % a.shape[0], 0)


def _pool_kernel(hp_ref, h_ref, hn_ref, mod_ref, g1n_ref, wp_ref, ps_ref, g2n_ref, w1_ref, w2_ref, o_ref, *, seq_len):
    j = pl.program_id(1)
    tm = h_ref.shape[0]
    h = h_ref[...]
    gain, shift, scale = g1n_ref[...], _mod_chunk(mod_ref, 0), _mod_chunk(mod_ref, 1)
    u = _norm_modulate(h, gain, shift, scale)
    u_prev = jnp.where(j > 0, _norm_modulate(hp_ref[...], gain, shift, scale), 0.0)
    u_next = jnp.where(j < pl.num_programs(1) - 1, _norm_modulate(hn_ref[...], gain, shift, scale), 0.0)
    ue = jnp.concatenate([u_prev, u, u_next], axis=0)

    gc = POOL_GC
    s2 = ue + _shift_rows(ue, 1)
    s4 = _shift_rows(s2[:, gc:], 1) + _shift_rows(s2[:, gc:], -1)
    s8 = _shift_rows(s4[:, gc:], 2) + _shift_rows(s4[:, gc:], -2)
    s16 = _shift_rows(s8[:, gc:], 4) + _shift_rows(s8[:, gc:], -4)
    sums = (s2[:, :gc], s4[:, :gc], s8[:, :gc], s16)

    t = j * tm + lax.broadcasted_iota(jnp.int32, (tm, 1), 0)
    ys = []
    for gi, w in enumerate(POOL_SIZES):
        lo = jnp.maximum(t - w // 2, 0)
        hi = jnp.minimum(t + w - w // 2, seq_len)
        inv_cnt = 1.0 / (hi - lo).astype(F32)
        pooled = sums[gi][HALO:HALO + tm] * inv_cnt - u[:, gi * gc:(gi + 1) * gc]
        ys.append(_dot(pooled.astype(BF16), wp_ref[gi]))
    y = jnp.concatenate(ys, axis=1) * ps_ref[...]
    o_ref[...] = _residual_mlp(h, y, mod_ref, g2n_ref, w1_ref, w2_ref)


def _odd_layer(h, mod_rows, norm1_g, w_pool, pool_scale, norm2_g, w1, w2, tm):
    b, n, _ = h.shape
    tok = lambda bi, j: (bi, j, 0)
    const2 = lambda bi, j: (0, 0)
    per_tile = tm // HALO
    last = n // HALO - 1
    return pl.pallas_call(
        functools.partial(_pool_kernel, seq_len=n),
        grid=(b, n // tm),
        in_specs=[
            pl.BlockSpec((None, HALO, D_MODEL), lambda bi, j: (bi, jnp.maximum(j * per_tile - 1, 0), 0)),
            pl.BlockSpec((None, tm, D_MODEL), tok),
            pl.BlockSpec((None, HALO, D_MODEL), lambda bi, j: (bi, jnp.minimum((j + 1) * per_tile, last), 0)),
            pl.BlockSpec((None, 1, N_MOD * D_MODEL), lambda bi, j: (bi, 0, 0)),
            pl.BlockSpec((1, D_MODEL), const2),
            _resident(w_pool.shape, lambda bi, j: (0, 0, 0)),
            pl.BlockSpec((1, D_MODEL), const2),
            pl.BlockSpec((1, D_MODEL), const2),
            _resident(w1.shape, const2),
            _resident(w2.shape, const2),
        ],
        out_specs=pl.BlockSpec((None, tm, D_MODEL), tok),
        out_shape=jax.ShapeDtypeStruct(h.shape, F32),
        compiler_params=_params("parallel", "parallel"),
        name="odd_layer",
    )(h, h, h, mod_rows, norm1_g, w_pool, pool_scale, norm2_g, w1, w2)


def _head_segments():
    lanes = 256
    seg = (np.arange(lanes)[:, None] // NA_HEAD_DIM) == (np.arange(lanes)[None, :] // NA_HEAD_DIM)
    return jnp.asarray(seg.astype(np.float32)).astype(BF16)


def kernel(x, c, ctx, c_ctx, w_mod, b_mod, norm1_g, norm2_g, w_in_even, w_four, q_norm_g, k_norm_g, rpb, w_out_even, w_pool, pool_scale, w_mlp1, w_mlp2):
    b = x.shape[0]
    depth = w_mod.shape[0]
    assert depth == 2, "layer 0 is the even (Fourier + attention) layer, layer 1 the odd (pooling) layer"
    ctx_row = b
    pad = (-(b + 1)) % SUBLANES
    c_rows = jnp.concatenate([c, c_ctx[None, :], jnp.zeros((pad, D_MODEL), F32)], axis=0)
    mod = _modulation(c_rows, w_mod, b_mod)
    mod = mod.reshape(depth, mod.shape[1], 1, N_MOD * D_MODEL)

    row = lambda a: a.reshape(1, -1)
    heads = lambda g: jnp.tile(g, NA_HEADS).reshape(1, NA_WIDTH)
    seg = _head_segments()

    w_in = w_in_even[0].astype(BF16)
    qg, kg = heads(q_norm_g[0]), heads(k_norm_g[0])
    f, q, k, v = _in_projection(x, mod[0], lambda bi: bi, row(norm1_g[0]), w_in, qg, kg, seg, tm=512)
    _, _, kc, vc = _in_projection(ctx, mod[0], lambda bi: ctx_row, row(norm1_g[0]), w_in, qg, kg, seg,
                                  tm=ctx.shape[1])
    o_four = _fourier_mix(f, w_four[0])
    o_na = _neighbourhood_attention(q, k, v, kc, vc, rpb[0])
    h = _even_tail(x, o_four, o_na, mod[0], w_out_even[0].astype(BF16), row(norm2_g[0]),
                   w_mlp1[0].astype(BF16), w_mlp2[0].astype(BF16), tm=512)

    h = _odd_layer(h, mod[1], row(norm1_g[1]), w_pool[0].astype(BF16), row(pool_scale[0]), row(norm2_g[1]),
                   w_mlp1[1].astype(BF16), w_mlp2[1].astype(BF16), tm=512)
    return h
```

```python
import functools

import numpy as np
import jax
import jax.numpy as jnp
from jax import lax
from jax.experimental import pallas as pl
from jax.experimental.pallas import tpu as pltpu

D_MODEL = 1024
GRID_W = 64
FOUR_GROUPS = 4
FOUR_GC = 128
FOUR_WIDTH = 512
NA_HEADS = 8
NA_HEAD_DIM = 64
NA_WIDTH = 512
NA_WIN_ROWS = 8
NA_WIN_COLS = 16
POOL_SIZES = (2, 4, 8, 16)
POOL_GC = 256
MLP_HIDDEN = 4096
N_MOD = 6
RMS_EPS = 1e-6

VMEM_LIMIT_BYTES = 56 * 1024 * 1024
SUBLANES = 8

F32 = jnp.float32
BF16 = jnp.bfloat16
MASK_VALUE = -1e30

FFT_N1 = 128
FFT_N2 = 64
Y1_PITCH = 2 * FFT_N1 + 8
Z_PITCH = FFT_N1 + 8
FFT_UNROLL = 8

NA_Q_ROWS = 4
NA_BAND_ROWS = NA_Q_ROWS + NA_WIN_ROWS


def _dot(a, b):
    return jnp.dot(a, b, preferred_element_type=F32)


def _dot_f32(a, b):
    return jnp.dot(a, b, preferred_element_type=F32, precision=lax.Precision.HIGHEST)


def _dot_nt(a, b):
    return lax.dot_general(a, b, (((1,), (1,)), ((), ())), preferred_element_type=F32)


def _params(*sem):
    return pltpu.CompilerParams(dimension_semantics=sem, vmem_limit_bytes=VMEM_LIMIT_BYTES)


def _resident(shape, index_map):
    return pl.BlockSpec(shape, index_map, pipeline_mode=pl.Buffered(1))


def _mod_kernel(c_ref, w_ref, b_ref, o_ref):
    c = c_ref[...]
    s = c * (1.0 / (1.0 + jnp.exp(-c)))
    o_ref[...] = _dot_f32(s, w_ref[...]) + b_ref[...]


def _modulation(c_rows, w_mod, b_mod):
    depth = w_mod.shape[0]
    rows = c_rows.shape[0]
    tn = D_MODEL
    return pl.pallas_call(
        _mod_kernel,
        grid=(depth, N_MOD * D_MODEL // tn),
        in_specs=[
            pl.BlockSpec((rows, D_MODEL), lambda l, j: (0, 0)),
            pl.BlockSpec((None, D_MODEL, tn), lambda l, j: (l, 0, j)),
            pl.BlockSpec((None, 1, tn), lambda l, j: (l, 0, j)),
        ],
        out_specs=pl.BlockSpec((None, rows, tn), lambda l, j: (l, 0, j)),
        out_shape=jax.ShapeDtypeStruct((depth, rows, N_MOD * D_MODEL), F32),
        compiler_params=_params("parallel", "parallel"),
        name="modulation",
    )(c_rows, w_mod, b_mod.reshape(depth, 1, N_MOD * D_MODEL))


def _norm_modulate(x, gain, shift, scale):
    ms = jnp.mean(x * x, axis=-1, keepdims=True)
    return (x * lax.rsqrt(ms + RMS_EPS) * gain) * (1.0 + scale) + shift


def _head_norm(a, gain, seg_ref):
    sq = (a * a).astype(BF16)
    half = seg_ref.shape[0]
    ms = jnp.concatenate([_dot(sq[:, :half], seg_ref[...]), _dot(sq[:, half:], seg_ref[...])], axis=1)
    return a * lax.rsqrt(ms * (1.0 / NA_HEAD_DIM) + RMS_EPS) * gain


def _inproj_kernel(x_ref, mod_ref, g_ref, w_ref, qg_ref, kg_ref, seg_ref, f_ref, q_ref, k_ref, v_ref):
    u = _norm_modulate(x_ref[...], g_ref[...], mod_ref[:, 0:D_MODEL], mod_ref[:, D_MODEL:2 * D_MODEL])
    p = _dot(u.astype(BF16), w_ref[...])
    f_ref[...] = p[:, :FOUR_WIDTH]
    q = p[:, FOUR_WIDTH:FOUR_WIDTH + NA_WIDTH]
    k = p[:, FOUR_WIDTH + NA_WIDTH:FOUR_WIDTH + 2 * NA_WIDTH]
    q_ref[...] = (_head_norm(q, qg_ref[...], seg_ref) * (NA_HEAD_DIM ** -0.5)).astype(BF16)
    k_ref[...] = _head_norm(k, kg_ref[...], seg_ref).astype(BF16)
    v_ref[...] = p[:, FOUR_WIDTH + 2 * NA_WIDTH:].astype(BF16)


def _in_projection(x, mod_rows, row_of_batch, norm_g, w_in, q_g, k_g, seg, tm):
    b, n, _ = x.shape
    width = w_in.shape[1]
    tok = lambda bi, j: (bi, j, 0)
    const2 = lambda bi, j: (0, 0)
    out_tok = lambda w: pl.BlockSpec((None, tm, w), tok)
    return pl.pallas_call(
        _inproj_kernel,
        grid=(b, n // tm),
        in_specs=[
            pl.BlockSpec((None, tm, D_MODEL), tok),
            pl.BlockSpec((None, 1, N_MOD * D_MODEL), lambda bi, j: (row_of_batch(bi), 0, 0)),
            pl.BlockSpec((1, D_MODEL), const2),
            _resident((D_MODEL, width), const2),
            pl.BlockSpec((1, NA_WIDTH), const2),
            pl.BlockSpec((1, NA_WIDTH), const2),
            pl.BlockSpec(seg.shape, const2),
        ],
        out_specs=[out_tok(FOUR_WIDTH), out_tok(NA_WIDTH), out_tok(NA_WIDTH), out_tok(NA_WIDTH)],
        out_shape=[
            jax.ShapeDtypeStruct((b, n, FOUR_WIDTH), F32),
            jax.ShapeDtypeStruct((b, n, NA_WIDTH), BF16),
            jax.ShapeDtypeStruct((b, n, NA_WIDTH), BF16),
            jax.ShapeDtypeStruct((b, n, NA_WIDTH), BF16),
        ],
        compiler_params=_params("parallel", "parallel"),
        name="in_projection",
    )(x, mod_rows, norm_g, w_in, q_g, k_g, seg)


def _fft_tables(n):
    n1, n2 = FFT_N1, FFT_N2
    assert n1 * n2 == n
    k1 = np.arange(n1)[None, :, None]
    pos = (n2 * np.arange(n1)[None, None, :] + np.arange(n2)[:, None, None])
    ang = 2.0 * np.pi * ((k1 * pos) % n) / n
    l1 = np.concatenate([np.cos(ang), -np.sin(ang)], axis=1)
    a2 = 2.0 * np.pi * ((np.arange(n2)[:, None] * np.arange(n2)[None, :]) % n2) / n2
    c2, s2 = np.cos(a2), np.sin(a2)
    l2 = np.block([[c2, s2], [-s2, c2]])
    ac = 2.0 * np.pi * ((np.arange(FOUR_GC)[:, None] * np.arange(FOUR_GC)[None, :]) % FOUR_GC) / FOUR_GC
    cs = np.concatenate([np.cos(ac), np.sin(ac)], axis=0)
    return l1.astype(np.float32), l2.astype(np.float32), cs.astype(np.float32)


def _fft_kernel(f_ref, l1_ref, l2_ref, cs_ref, w_ref, o_ref, y1_ref, z_ref, *, ortho):
    n1, n2 = FFT_N1, FFT_N2

    def stage1(j, carry):
        r = f_ref[pl.ds(j, n1, stride=n2), :].astype(BF16)
        y1_ref[pl.ds(pl.multiple_of(j * Y1_PITCH, SUBLANES), 2 * n1), :] = _dot(l1_ref[j], r)
        return carry

    lax.fori_loop(0, n2, stage1, 0, unroll=FFT_UNROLL)

    l2 = l2_ref[...]

    def stage2(k1, carry):
        yr = y1_ref[pl.ds(k1, n2, stride=Y1_PITCH), :]
        yi = y1_ref[pl.ds(n1 + k1, n2, stride=Y1_PITCH), :]
        z = _dot(l2, jnp.concatenate([yr, yi], axis=0).astype(BF16))
        z_ref[0, pl.ds(k1, n2, stride=Z_PITCH), :] = z[:n2]
        z_ref[1, pl.ds(k1, n2, stride=Z_PITCH), :] = z[n2:]
        return carry

    lax.fori_loop(0, n1, stage2, 0, unroll=FFT_UNROLL)

    wf = (_dot_f32(cs_ref[...], w_ref[...]) * ortho).astype(BF16)
    wr, wi = wf[:FOUR_GC], wf[FOUR_GC:]

    def stage3(k2, carry):
        base = pl.multiple_of(k2 * Z_PITCH, SUBLANES)
        zr = z_ref[0, pl.ds(base, n1), :].astype(BF16)
        zi = z_ref[1, pl.ds(base, n1), :].astype(BF16)
        o_ref[pl.ds(pl.multiple_of(k2 * n1, n1), n1), :] = (_dot(zr, wr) + _dot(zi, wi)).astype(o_ref.dtype)
        return carry

    lax.fori_loop(0, n2, stage3, 0, unroll=FFT_UNROLL)


def _fourier_mix(f, w_four):
    b, n, _ = f.shape
    l1, l2, cs = _fft_tables(n)
    l1 = jnp.asarray(l1).astype(BF16)
    l2 = jnp.asarray(l2).astype(BF16)
    cs = jnp.asarray(cs)
    ortho = float(1.0 / np.sqrt(n * FOUR_GC))
    slab = lambda bi, g: (bi, 0, g)
    return pl.pallas_call(
        functools.partial(_fft_kernel, ortho=ortho),
        grid=(b, FOUR_GROUPS),
        in_specs=[
            pl.BlockSpec((None, n, FOUR_GC), slab),
            _resident(l1.shape, lambda bi, g: (0, 0, 0)),
            _resident(l2.shape, lambda bi, g: (0, 0)),
            _resident(cs.shape, lambda bi, g: (0, 0)),
            pl.BlockSpec((None, FOUR_GC, FOUR_GC), lambda bi, g: (g, 0, 0)),
        ],
        out_specs=pl.BlockSpec((None, n, FOUR_GC), slab),
        out_shape=jax.ShapeDtypeStruct((b, n, FOUR_WIDTH), BF16),
        scratch_shapes=[
            pltpu.VMEM((FFT_N2 * Y1_PITCH, FOUR_GC), F32),
            pltpu.VMEM((2, FFT_N2 * Z_PITCH, FOUR_GC), F32),
        ],
        compiler_params=_params("parallel", "parallel"),
        name="fourier_mix",
    )(f, l1, l2, cs, w_four)


def _band_start(i, rows):
    return jnp.clip(NA_Q_ROWS * i - (NA_BAND_ROWS - NA_Q_ROWS) // 2, 0, rows - NA_BAND_ROWS)


def _natten_bias_tables(rpb, rows):
    heads = rpb.shape[0]
    groups = rows // NA_Q_ROWS
    n_dr, n_dc = 2 * NA_WIN_ROWS - 1, 2 * NA_WIN_COLS - 1
    qc = np.arange(GRID_W)
    col_start = np.clip(qc - NA_WIN_COLS // 2, 0, GRID_W - NA_WIN_COLS)
    col_ok = (qc[None, :] >= col_start[:, None]) & (qc[None, :] < col_start[:, None] + NA_WIN_COLS)
    dc = np.clip(qc[None, :] - qc[:, None] + (NA_WIN_COLS - 1), 0, n_dc - 1)
    onehot_dc = (dc[None] == np.arange(n_dc)[:, None, None]).astype(np.float32)
    tiles = jnp.einsum('hre,eqk->hrqk', rpb.astype(F32), jnp.asarray(onehot_dc), precision=lax.Precision.HIGHEST)
    tiles = jnp.where(jnp.asarray(col_ok), tiles, MASK_VALUE)
    masked_tile = jnp.full((heads, GRID_W, GRID_W), MASK_VALUE, F32)
    kinds = []
    for i in (0, groups // 2, groups - 1):
        band0 = int(np.clip(NA_Q_ROWS * i - (NA_BAND_ROWS - NA_Q_ROWS) // 2, 0, rows - NA_BAND_ROWS))
        per_row = []
        for a in range(NA_Q_ROWS):
            qr = NA_Q_ROWS * i + a
            row_start = int(np.clip(qr - NA_WIN_ROWS // 2, 0, rows - NA_WIN_ROWS))
            assert band0 <= row_start and row_start + NA_WIN_ROWS <= band0 + NA_BAND_ROWS
            band = []
            for bb in range(NA_BAND_ROWS):
                kr = band0 + bb
                in_window = row_start <= kr < row_start + NA_WIN_ROWS
                band.append(tiles[:, kr - qr + NA_WIN_ROWS - 1] if in_window else masked_tile)
            per_row.append(jnp.concatenate(band, axis=-1))
        kinds.append(jnp.concatenate(per_row, axis=1))
    table = jnp.stack(kinds, axis=1)
    nq, nk = NA_Q_ROWS * GRID_W, NA_BAND_ROWS * GRID_W
    table = table.reshape(heads // 2, 2, 3, nq, nk).transpose(0, 2, 1, 3, 4)
    return table.reshape(heads // 2, 3, 2 * nq, nk)


def _natten_kernel(q_ref, k_ref, v_ref, kc_ref, vc_ref, tab_ref, o_ref, *, rows):
    i = pl.program_id(2)
    nq, nk = NA_Q_ROWS * GRID_W, NA_BAND_ROWS * GRID_W
    tok0 = pl.multiple_of(_band_start(i, rows) * GRID_W, GRID_W * 4)
    q2 = q_ref[...]
    kb = k_ref[pl.ds(tok0, nk), :]
    vb = v_ref[pl.ds(tok0, nk), :]
    first_head = lax.broadcasted_iota(jnp.int32, (1, 2 * NA_HEAD_DIM), 1) < NA_HEAD_DIM
    zero = jnp.zeros_like(q2)
    qs = jnp.concatenate([jnp.where(first_head, q2, zero), jnp.where(first_head, zero, q2)], axis=0)
    s_win = _dot_nt(qs, kb) + tab_ref[...]
    s_ctx = _dot_nt(qs, kc_ref[...])
    m = jnp.maximum(jnp.max(s_win, axis=-1, keepdims=True), jnp.max(s_ctx, axis=-1, keepdims=True))
    p_win = jnp.exp(s_win - m)
    p_ctx = jnp.exp(s_ctx - m)
    l = jnp.sum(p_win, axis=-1, keepdims=True) + jnp.sum(p_ctx, axis=-1, keepdims=True)
    o = (_dot(p_win.astype(BF16), vb) + _dot(p_ctx.astype(BF16), vc_ref[...])) / l
    o_ref[...] = jnp.where(first_head, o[:nq], o[nq:]).astype(o_ref.dtype)


def _neighbourhood_attention(q, k, v, kc, vc, rpb):
    b, n, _ = q.shape
    rows = n // GRID_W
    groups = rows // NA_Q_ROWS
    ctx_len = kc.shape[1]
    tq = NA_Q_ROWS * GRID_W
    tables = _natten_bias_tables(rpb, rows)
    lanes = 2 * NA_HEAD_DIM
    kind = lambda i: jnp.where(i == 0, 0, jnp.where(i == groups - 1, 2, 1))
    seq = lambda bi, hp, i: (bi, 0, hp)
    return pl.pallas_call(
        functools.partial(_natten_kernel, rows=rows),
        grid=(b, NA_HEADS // 2, groups),
        in_specs=[
            pl.BlockSpec((None, tq, lanes), lambda bi, hp, i: (bi, i, hp)),
            pl.BlockSpec((None, n, lanes), seq),
            pl.BlockSpec((None, n, lanes), seq),
            pl.BlockSpec((None, ctx_len, lanes), seq),
            pl.BlockSpec((None, ctx_len, lanes), seq),
            pl.BlockSpec((None, None, 2 * tq, NA_BAND_ROWS * GRID_W), lambda bi, hp, i: (hp, kind(i), 0, 0)),
        ],
        out_specs=pl.BlockSpec((None, tq, lanes), lambda bi, hp, i: (bi, i, hp)),
        out_shape=jax.ShapeDtypeStruct((b, n, NA_WIDTH), BF16),
        compiler_params=_params("parallel", "parallel", "arbitrary"),
        name="neighbourhood_attention",
    )(q, k, v, kc, vc, tables)


def _mod_chunk(mod_ref, idx):
    return mod_ref[:, idx * D_MODEL:(idx + 1) * D_MODEL]


def _residual_mlp(h, y, mod_ref, g2n_ref, w1_ref, w2_ref):
    h1 = h + _mod_chunk(mod_ref, 2) * y
    u = _norm_modulate(h1, g2n_ref[...], _mod_chunk(mod_ref, 3), _mod_chunk(mod_ref, 4)).astype(BF16)
    acc = None
    chunk = D_MODEL
    for c in range(MLP_HIDDEN // chunk):
        a = jnp.maximum(_dot(u, w1_ref[:, c * chunk:(c + 1) * chunk]), 0.0)
        part = _dot((a * a).astype(BF16), w2_ref[c * chunk:(c + 1) * chunk, :])
        acc = part if acc is None else acc + part
    return h1 + _mod_chunk(mod_ref, 5) * acc


def _even_tail_kernel(h_ref, of_ref, on_ref, mod_ref, wo_ref, g2n_ref, w1_ref, w2_ref, o_ref):
    y = _dot(of_ref[...], wo_ref[:FOUR_WIDTH, :]) + _dot(on_ref[...], wo_ref[FOUR_WIDTH:, :])
    o_ref[...] = _residual_mlp(h_ref[...], y, mod_ref, g2n_ref, w1_ref, w2_ref)


def _even_tail(h, o_four, o_na, mod_rows, w_out, norm2_g, w1, w2, tm):
    b, n, _ = h.shape
    tok = lambda bi, j: (bi, j, 0)
    const2 = lambda bi, j: (0, 0)
    return pl.pallas_call(
        _even_tail_kernel,
        grid=(b, n // tm),
        in_specs=[
            pl.BlockSpec((None, tm, D_MODEL), tok),
            pl.BlockSpec((None, tm, FOUR_WIDTH), tok),
            pl.BlockSpec((None, tm, NA_WIDTH), tok),
            pl.BlockSpec((None, 1, N_MOD * D_MODEL), lambda bi, j: (bi, 0, 0)),
            _resident(w_out.shape, const2),
            pl.BlockSpec((1, D_MODEL), const2),
            _resident(w1.shape, const2),
            _resident(w2.shape, const2),
        ],
        out_specs=pl.BlockSpec((None, tm, D_MODEL), tok),
        out_shape=jax.ShapeDtypeStruct(h.shape, F32),
        compiler_params=_params("parallel", "parallel"),
        name="even_tail",
    )(h, o_four, o_na, mod_rows, w_out, norm2_g, w1, w2)


HALO = 8


def _shift_rows(a, s):
    return pltpu.roll(a, s % a.shape[0], 0)


def _pool_kernel(hp_ref, h_ref, hn_ref, mod_ref, g1n_ref, wp_ref, ps_ref, g2n_ref, w1_ref, w2_ref, o_ref, *, seq_len):
    j = pl.program_id(1)
    tm = h_ref.shape[0]
    h = h_ref[...]
    gain, shift, scale = g1n_ref[...], _mod_chunk(mod_ref, 0), _mod_chunk(mod_ref, 1)
    u = _norm_modulate(h, gain, shift, scale)
    u_prev = jnp.where(j > 0, _norm_modulate(hp_ref[...], gain, shift, scale), 0.0)
    u_next = jnp.where(j < pl.num_programs(1) - 1, _norm_modulate(hn_ref[...], gain, shift, scale), 0.0)
    ue = jnp.concatenate([u_prev, u, u_next], axis=0)

    gc = POOL_GC
    s2 = ue + _shift_rows(ue, 1)
    s4 = _shift_rows(s2[:, gc:], 1) + _shift_rows(s2[:, gc:], -1)
    s8 = _shift_rows(s4[:, gc:], 2) + _shift_rows(s4[:, gc:], -2)
    s16 = _shift_rows(s8[:, gc:], 4) + _shift_rows(s8[:, gc:], -4)
    sums = (s2[:, :gc], s4[:, :gc], s8[:, :gc], s16)

    t = j * tm + lax.broadcasted_iota(jnp.int32, (tm, 1), 0)
    ys = []
    for gi, w in enumerate(POOL_SIZES):
        lo = jnp.maximum(t - w // 2, 0)
        hi = jnp.minimum(t + w - w // 2, seq_len)
        inv_cnt = 1.0 / (hi - lo).astype(F32)
        pooled = sums[gi][HALO:HALO + tm] * inv_cnt - u[:, gi * gc:(gi + 1) * gc]
        ys.append(_dot(pooled.astype(BF16), wp_ref[gi]))
    y = jnp.concatenate(ys, axis=1) * ps_ref[...]
    o_ref[...] = _residual_mlp(h, y, mod_ref, g2n_ref, w1_ref, w2_ref)


def _odd_layer(h, mod_rows, norm1_g, w_pool, pool_scale, norm2_g, w1, w2, tm):
    b, n, _ = h.shape
    tok = lambda bi, j: (bi, j, 0)
    const2 = lambda bi, j: (0, 0)
    per_tile = tm // HALO
    last = n // HALO - 1
    return pl.pallas_call(
        functools.partial(_pool_kernel, seq_len=n),
        grid=(b, n // tm),
        in_specs=[
            pl.BlockSpec((None, HALO, D_MODEL), lambda bi, j: (bi, jnp.maximum(j * per_tile - 1, 0), 0)),
            pl.BlockSpec((None, tm, D_MODEL), tok),
            pl.BlockSpec((None, HALO, D_MODEL), lambda bi, j: (bi, jnp.minimum((j + 1) * per_tile, last), 0)),
            pl.BlockSpec((None, 1, N_MOD * D_MODEL), lambda bi, j: (bi, 0, 0)),
            pl.BlockSpec((1, D_MODEL), const2),
            _resident(w_pool.shape, lambda bi, j: (0, 0, 0)),
            pl.BlockSpec((1, D_MODEL), const2),
            pl.BlockSpec((1, D_MODEL), const2),
            _resident(w1.shape, const2),
            _resident(w2.shape, const2),
        ],
        out_specs=pl.BlockSpec((None, tm, D_MODEL), tok),
        out_shape=jax.ShapeDtypeStruct(h.shape, F32),
        compiler_params=_params("parallel", "parallel"),
        name="odd_layer",
    )(h, h, h, mod_rows, norm1_g, w_pool, pool_scale, norm2_g, w1, w2)


def _head_segments():
    lanes = 256
    seg = (np.arange(lanes)[:, None] // NA_HEAD_DIM) == (np.arange(lanes)[None, :] // NA_HEAD_DIM)
    return jnp.asarray(seg.astype(np.float32)).astype(BF16)


def kernel(x, c, ctx, c_ctx, w_mod, b_mod, norm1_g, norm2_g, w_in_even, w_four, q_norm_g, k_norm_g, rpb, w_out_even, w_pool, pool_scale, w_mlp1, w_mlp2):
    b = x.shape[0]
    depth = w_mod.shape[0]
    assert depth == 2, "layer 0 is the even (Fourier + attention) layer, layer 1 the odd (pooling) layer"
    ctx_row = b
    pad = (-(b + 1)) % SUBLANES
    c_rows = jnp.concatenate([c, c_ctx[None, :], jnp.zeros((pad, D_MODEL), F32)], axis=0)
    mod = _modulation(c_rows, w_mod, b_mod)
    mod = mod.reshape(depth, mod.shape[1], 1, N_MOD * D_MODEL)

    row = lambda a: a.reshape(1, -1)
    heads = lambda g: jnp.tile(g, NA_HEADS).reshape(1, NA_WIDTH)
    seg = _head_segments()

    w_in = w_in_even[0].astype(BF16)
    qg, kg = heads(q_norm_g[0]), heads(k_norm_g[0])
    f, q, k, v = _in_projection(x, mod[0], lambda bi: bi, row(norm1_g[0]), w_in, qg, kg, seg, tm=512)
    _, _, kc, vc = _in_projection(ctx, mod[0], lambda bi: ctx_row, row(norm1_g[0]), w_in, qg, kg, seg,
                                  tm=ctx.shape[1])
    o_four = _fourier_mix(f, w_four[0])
    o_na = _neighbourhood_attention(q, k, v, kc, vc, rpb[0])
    h = _even_tail(x, o_four, o_na, mod[0], w_out_even[0].astype(BF16), row(norm2_g[0]),
                   w_mlp1[0].astype(BF16), w_mlp2[0].astype(BF16), tm=512)

    h = _odd_layer(h, mod[1], row(norm1_g[1]), w_pool[0].astype(BF16), row(pool_scale[0]), row(norm2_g[1]),
                   w_mlp1[1].astype(BF16), w_mlp2[1].astype(BF16), tm=512)
    return h
```

```python
import functools

import numpy as np
import jax
import jax.numpy as jnp
from jax import lax
from jax.experimental import pallas as pl
from jax.experimental.pallas import tpu as pltpu

D_MODEL = 1024
GRID_W = 64
FOUR_GROUPS = 4
FOUR_GC = 128
FOUR_WIDTH = 512
NA_HEADS = 8
NA_HEAD_DIM = 64
NA_WIDTH = 512
NA_WIN_ROWS = 8
NA_WIN_COLS = 16
POOL_SIZES = (2, 4, 8, 16)
POOL_GC = 256
MLP_HIDDEN = 4096
N_MOD = 6
RMS_EPS = 1e-6

VMEM_LIMIT_BYTES = 56 * 1024 * 1024
SUBLANES = 8

F32 = jnp.float32
BF16 = jnp.bfloat16
MASK_VALUE = -1e30

FFT_N1 = 128
FFT_N2 = 64
Y1_PITCH = 2 * FFT_N1 + 8
Z_PITCH = FFT_N1 + 8
FFT_UNROLL = 8

NA_Q_ROWS = 4
NA_BAND_ROWS = NA_Q_ROWS + NA_WIN_ROWS
NA_CHAINS = 4


def _dot(a, b):
    return jnp.dot(a, b, preferred_element_type=F32)


def _dot_f32(a, b):
    return jnp.dot(a, b, preferred_element_type=F32, precision=lax.Precision.HIGHEST)


def _dot_nt(a, b):
    return lax.dot_general(a, b, (((1,), (1,)), ((), ())), preferred_element_type=F32)


def _params(*sem):
    return pltpu.CompilerParams(dimension_semantics=sem, vmem_limit_bytes=VMEM_LIMIT_BYTES)


def _resident(shape, index_map):
    return pl.BlockSpec(shape, index_map, pipeline_mode=pl.Buffered(1))


def _mod_kernel(c_ref, w_ref, b_ref, o_ref):
    c = c_ref[...]
    s = c * (1.0 / (1.0 + jnp.exp(-c)))
    o_ref[...] = _dot_f32(s, w_ref[...]) + b_ref[...]


def _modulation(c_rows, w_mod, b_mod):
    depth = w_mod.shape[0]
    rows = c_rows.shape[0]
    tn = D_MODEL
    return pl.pallas_call(
        _mod_kernel,
        grid=(depth, N_MOD * D_MODEL // tn),
        in_specs=[
            pl.BlockSpec((rows, D_MODEL), lambda l, j: (0, 0)),
            pl.BlockSpec((None, D_MODEL, tn), lambda l, j: (l, 0, j)),
            pl.BlockSpec((None, 1, tn), lambda l, j: (l, 0, j)),
        ],
        out_specs=pl.BlockSpec((None, rows, tn), lambda l, j: (l, 0, j)),
        out_shape=jax.ShapeDtypeStruct((depth, rows, N_MOD * D_MODEL), F32),
        compiler_params=_params("parallel", "parallel"),
        name="modulation",
    )(c_rows, w_mod, b_mod.reshape(depth, 1, N_MOD * D_MODEL))


def _norm_modulate(x, gain, shift, scale):
    ms = jnp.mean(x * x, axis=-1, keepdims=True)
    return (x * lax.rsqrt(ms + RMS_EPS) * gain) * (1.0 + scale) + shift


def _head_norm(a, gain, seg_ref):
    sq = (a * a).astype(BF16)
    half = seg_ref.shape[0]
    ms = jnp.concatenate([_dot(sq[:, :half], seg_ref[...]), _dot(sq[:, half:], seg_ref[...])], axis=1)
    return a * lax.rsqrt(ms * (1.0 / NA_HEAD_DIM) + RMS_EPS) * gain


def _inproj_kernel(x_ref, mod_ref, g_ref, w_ref, qg_ref, kg_ref, seg_ref, f_ref, q_ref, k_ref, v_ref):
    u = _norm_modulate(x_ref[...], g_ref[...], mod_ref[:, 0:D_MODEL], mod_ref[:, D_MODEL:2 * D_MODEL])
    p = _dot(u.astype(BF16), w_ref[...])
    f_ref[...] = p[:, :FOUR_WIDTH]
    q = p[:, FOUR_WIDTH:FOUR_WIDTH + NA_WIDTH]
    k = p[:, FOUR_WIDTH + NA_WIDTH:FOUR_WIDTH + 2 * NA_WIDTH]
    q_ref[...] = (_head_norm(q, qg_ref[...], seg_ref) * (NA_HEAD_DIM ** -0.5)).astype(BF16)
    k_ref[...] = _head_norm(k, kg_ref[...], seg_ref).astype(BF16)
    v_ref[...] = p[:, FOUR_WIDTH + 2 * NA_WIDTH:].astype(BF16)


def _in_projection(x, mod_rows, row_of_batch, norm_g, w_in, q_g, k_g, seg, tm):
    b, n, _ = x.shape
    width = w_in.shape[1]
    tok = lambda bi, j: (bi, j, 0)
    const2 = lambda bi, j: (0, 0)
    out_tok = lambda w: pl.BlockSpec((None, tm, w), tok)
    return pl.pallas_call(
        _inproj_kernel,
        grid=(b, n // tm),
        in_specs=[
            pl.BlockSpec((None, tm, D_MODEL), tok),
            pl.BlockSpec((None, 1, N_MOD * D_MODEL), lambda bi, j: (row_of_batch(bi), 0, 0)),
            pl.BlockSpec((1, D_MODEL), const2),
            _resident((D_MODEL, width), const2),
            pl.BlockSpec((1, NA_WIDTH), const2),
            pl.BlockSpec((1, NA_WIDTH), const2),
            pl.BlockSpec(seg.shape, const2),
        ],
        out_specs=[out_tok(FOUR_WIDTH), out_tok(NA_WIDTH), out_tok(NA_WIDTH), out_tok(NA_WIDTH)],
        out_shape=[
            jax.ShapeDtypeStruct((b, n, FOUR_WIDTH), F32),
            jax.ShapeDtypeStruct((b, n, NA_WIDTH), BF16),
            jax.ShapeDtypeStruct((b, n, NA_WIDTH), BF16),
            jax.ShapeDtypeStruct((b, n, NA_WIDTH), BF16),
        ],
        compiler_params=_params("parallel", "parallel"),
        name="in_projection",
    )(x, mod_rows, norm_g, w_in, q_g, k_g, seg)


def _fft_tables(n):
    n1, n2 = FFT_N1, FFT_N2
    assert n1 * n2 == n
    k1 = np.arange(n1)[None, :, None]
    pos = (n2 * np.arange(n1)[None, None, :] + np.arange(n2)[:, None, None])
    ang = 2.0 * np.pi * ((k1 * pos) % n) / n
    l1 = np.concatenate([np.cos(ang), -np.sin(ang)], axis=1)
    a2 = 2.0 * np.pi * ((np.arange(n2)[:, None] * np.arange(n2)[None, :]) % n2) / n2
    c2, s2 = np.cos(a2), np.sin(a2)
    l2 = np.block([[c2, s2], [-s2, c2]])
    ac = 2.0 * np.pi * ((np.arange(FOUR_GC)[:, None] * np.arange(FOUR_GC)[None, :]) % FOUR_GC) / FOUR_GC
    cs = np.concatenate([np.cos(ac), np.sin(ac)], axis=0)
    return l1.astype(np.float32), l2.astype(np.float32), cs.astype(np.float32)


def _fft_kernel(f_ref, l1_ref, l2_ref, cs_ref, w_ref, o_ref, y1_ref, z_ref, *, ortho):
    n1, n2 = FFT_N1, FFT_N2

    def stage1(j, carry):
        r = f_ref[pl.ds(j, n1, stride=n2), :].astype(BF16)
        y1_ref[pl.ds(pl.multiple_of(j * Y1_PITCH, SUBLANES), 2 * n1), :] = _dot(l1_ref[j], r)
        return carry

    lax.fori_loop(0, n2, stage1, 0, unroll=FFT_UNROLL)

    l2 = l2_ref[...]

    def stage2(k1, carry):
        yr = y1_ref[pl.ds(k1, n2, stride=Y1_PITCH), :]
        yi = y1_ref[pl.ds(n1 + k1, n2, stride=Y1_PITCH), :]
        z = _dot(l2, jnp.concatenate([yr, yi], axis=0).astype(BF16))
        z_ref[0, pl.ds(k1, n2, stride=Z_PITCH), :] = z[:n2]
        z_ref[1, pl.ds(k1, n2, stride=Z_PITCH), :] = z[n2:]
        return carry

    lax.fori_loop(0, n1, stage2, 0, unroll=FFT_UNROLL)

    wf = (_dot_f32(cs_ref[...], w_ref[...]) * ortho).astype(BF16)
    wr, wi = wf[:FOUR_GC], wf[FOUR_GC:]

    def stage3(k2, carry):
        base = pl.multiple_of(k2 * Z_PITCH, SUBLANES)
        zr = z_ref[0, pl.ds(base, n1), :].astype(BF16)
        zi = z_ref[1, pl.ds(base, n1), :].astype(BF16)
        o_ref[pl.ds(pl.multiple_of(k2 * n1, n1), n1), :] = (_dot(zr, wr) + _dot(zi, wi)).astype(o_ref.dtype)
        return carry

    lax.fori_loop(0, n2, stage3, 0, unroll=FFT_UNROLL)


def _fourier_mix(f, w_four):
    b, n, _ = f.shape
    l1, l2, cs = _fft_tables(n)
    l1 = jnp.asarray(l1).astype(BF16)
    l2 = jnp.asarray(l2).astype(BF16)
    cs = jnp.asarray(cs)
    ortho = float(1.0 / np.sqrt(n * FOUR_GC))
    slab = lambda bi, g: (bi, 0, g)
    return pl.pallas_call(
        functools.partial(_fft_kernel, ortho=ortho),
        grid=(b, FOUR_GROUPS),
        in_specs=[
            pl.BlockSpec((None, n, FOUR_GC), slab),
            _resident(l1.shape, lambda bi, g: (0, 0, 0)),
            _resident(l2.shape, lambda bi, g: (0, 0)),
            _resident(cs.shape, lambda bi, g: (0, 0)),
            pl.BlockSpec((None, FOUR_GC, FOUR_GC), lambda bi, g: (g, 0, 0)),
        ],
        out_specs=pl.BlockSpec((None, n, FOUR_GC), slab),
        out_shape=jax.ShapeDtypeStruct((b, n, FOUR_WIDTH), BF16),
        scratch_shapes=[
            pltpu.VMEM((FFT_N2 * Y1_PITCH, FOUR_GC), F32),
            pltpu.VMEM((2, FFT_N2 * Z_PITCH, FOUR_GC), F32),
        ],
        compiler_params=_params("parallel", "parallel"),
        name="fourier_mix",
    )(f, l1, l2, cs, w_four)


def _band_start(i, rows):
    return jnp.clip(NA_Q_ROWS * i - (NA_BAND_ROWS - NA_Q_ROWS) // 2, 0, rows - NA_BAND_ROWS)


def _natten_bias_tables(rpb, rows):
    heads = rpb.shape[0]
    groups = rows // NA_Q_ROWS
    n_dr, n_dc = 2 * NA_WIN_ROWS - 1, 2 * NA_WIN_COLS - 1
    qc = np.arange(GRID_W)
    col_start = np.clip(qc - NA_WIN_COLS // 2, 0, GRID_W - NA_WIN_COLS)
    col_ok = (qc[None, :] >= col_start[:, None]) & (qc[None, :] < col_start[:, None] + NA_WIN_COLS)
    dc = np.clip(qc[None, :] - qc[:, None] + (NA_WIN_COLS - 1), 0, n_dc - 1)
    onehot_dc = (dc[None] == np.arange(n_dc)[:, None, None]).astype(np.float32)
    tiles = jnp.einsum('hre,eqk->hrqk', rpb.astype(F32), jnp.asarray(onehot_dc), precision=lax.Precision.HIGHEST)
    tiles = jnp.where(jnp.asarray(col_ok), tiles, MASK_VALUE)
    masked_tile = jnp.full((heads, GRID_W, GRID_W), MASK_VALUE, F32)
    kinds = []
    for i in (0, groups // 2, groups - 1):
        band0 = int(np.clip(NA_Q_ROWS * i - (NA_BAND_ROWS - NA_Q_ROWS) // 2, 0, rows - NA_BAND_ROWS))
        per_row = []
        for a in range(NA_Q_ROWS):
            qr = NA_Q_ROWS * i + a
            row_start = int(np.clip(qr - NA_WIN_ROWS // 2, 0, rows - NA_WIN_ROWS))
            assert band0 <= row_start and row_start + NA_WIN_ROWS <= band0 + NA_BAND_ROWS
            band = []
            for bb in range(NA_BAND_ROWS):
                kr = band0 + bb
                in_window = row_start <= kr < row_start + NA_WIN_ROWS
                band.append(tiles[:, kr - qr + NA_WIN_ROWS - 1] if in_window else masked_tile)
            per_row.append(jnp.concatenate(band, axis=-1))
        kinds.append(jnp.concatenate(per_row, axis=1))
    table = jnp.stack(kinds, axis=1)
    nq, nk = NA_Q_ROWS * GRID_W, NA_BAND_ROWS * GRID_W
    table = table.reshape(heads // 2, 2, 3, nq, nk).transpose(0, 2, 1, 3, 4)
    return table.reshape(heads // 2, 3, 2 * nq, nk)


def _natten_kernel(q_ref, k_ref, v_ref, kc_ref, vc_ref, tab_ref, o_ref, *, rows):
    nq, nk = NA_Q_ROWS * GRID_W, NA_BAND_ROWS * GRID_W
    groups = rows // NA_Q_ROWS
    first_head = lax.broadcasted_iota(jnp.int32, (1, 2 * NA_HEAD_DIM), 1) < NA_HEAD_DIM
    kc, vc = kc_ref[...], vc_ref[...]
    for j in range(NA_CHAINS):
        c = pl.program_id(2) * NA_CHAINS + j
        kind = jnp.where(c == 0, 0, jnp.where(c == groups - 1, 2, 1))
        tok0 = pl.multiple_of(_band_start(c, rows) * GRID_W, GRID_W * 4)
        q2 = q_ref[j * nq:(j + 1) * nq, :]
        kb = k_ref[pl.ds(tok0, nk), :]
        vb = v_ref[pl.ds(tok0, nk), :]
        zero = jnp.zeros_like(q2)
        qs = jnp.concatenate([jnp.where(first_head, q2, zero), jnp.where(first_head, zero, q2)], axis=0)
        s_win = _dot_nt(qs, kb) + tab_ref[kind]
        s_ctx = _dot_nt(qs, kc)
        m = jnp.maximum(jnp.max(s_win, axis=-1, keepdims=True), jnp.max(s_ctx, axis=-1, keepdims=True))
        p_win = jnp.exp(s_win - m)
        p_ctx = jnp.exp(s_ctx - m)
        l = jnp.sum(p_win, axis=-1, keepdims=True) + jnp.sum(p_ctx, axis=-1, keepdims=True)
        o = (_dot(p_win.astype(BF16), vb) + _dot(p_ctx.astype(BF16), vc)) / l
        o_ref[j * nq:(j + 1) * nq, :] = jnp.where(first_head, o[:nq], o[nq:]).astype(o_ref.dtype)


def _neighbourhood_attention(q, k, v, kc, vc, rpb):
    b, n, _ = q.shape
    rows = n // GRID_W
    steps = rows // (NA_Q_ROWS * NA_CHAINS)
    ctx_len = kc.shape[1]
    tq = NA_CHAINS * NA_Q_ROWS * GRID_W
    tables = _natten_bias_tables(rpb, rows)
    lanes = 2 * NA_HEAD_DIM
    seq = lambda bi, hp, i: (bi, 0, hp)
    return pl.pallas_call(
        functools.partial(_natten_kernel, rows=rows),
        grid=(b, NA_HEADS // 2, steps),
        in_specs=[
            pl.BlockSpec((None, tq, lanes), lambda bi, hp, i: (bi, i, hp)),
            pl.BlockSpec((None, n, lanes), seq),
            pl.BlockSpec((None, n, lanes), seq),
            pl.BlockSpec((None, ctx_len, lanes), seq),
            pl.BlockSpec((None, ctx_len, lanes), seq),
            pl.BlockSpec((None,) + tables.shape[1:], lambda bi, hp, i: (hp, 0, 0, 0)),
        ],
        out_specs=pl.BlockSpec((None, tq, lanes), lambda bi, hp, i: (bi, i, hp)),
        out_shape=jax.ShapeDtypeStruct((b, n, NA_WIDTH), BF16),
        compiler_params=_params("parallel", "parallel", "arbitrary"),
        name="neighbourhood_attention",
    )(q, k, v, kc, vc, tables)


def _mod_chunk(mod_ref, idx):
    return mod_ref[:, idx * D_MODEL:(idx + 1) * D_MODEL]


def _residual_mlp(h, y, mod_ref, g2n_ref, w1_ref, w2_ref):
    h1 = h + _mod_chunk(mod_ref, 2) * y
    u = _norm_modulate(h1, g2n_ref[...], _mod_chunk(mod_ref, 3), _mod_chunk(mod_ref, 4)).astype(BF16)
    acc = None
    chunk = D_MODEL
    for c in range(MLP_HIDDEN // chunk):
        a = jnp.maximum(_dot(u, w1_ref[:, c * chunk:(c + 1) * chunk]), 0.0)
        part = _dot((a * a).astype(BF16), w2_ref[c * chunk:(c + 1) * chunk, :])
        acc = part if acc is None else acc + part
    return h1 + _mod_chunk(mod_ref, 5) * acc


def _even_tail_kernel(h_ref, of_ref, on_ref, mod_ref, wo_ref, g2n_ref, w1_ref, w2_ref, o_ref):
    y = _dot(of_ref[...], wo_ref[:FOUR_WIDTH, :]) + _dot(on_ref[...], wo_ref[FOUR_WIDTH:, :])
    o_ref[...] = _residual_mlp(h_ref[...], y, mod_ref, g2n_ref, w1_ref, w2_ref)


def _even_tail(h, o_four, o_na, mod_rows, w_out, norm2_g, w1, w2, tm):
    b, n, _ = h.shape
    tok = lambda bi, j: (bi, j, 0)
    const2 = lambda bi, j: (0, 0)
    return pl.pallas_call(
        _even_tail_kernel,
        grid=(b, n // tm),
        in_specs=[
            pl.BlockSpec((None, tm, D_MODEL), tok),
            pl.BlockSpec((None, tm, FOUR_WIDTH), tok),
            pl.BlockSpec((None, tm, NA_WIDTH), tok),
            pl.BlockSpec((None, 1, N_MOD * D_MODEL), lambda bi, j: (bi, 0, 0)),
            _resident(w_out.shape, const2),
            pl.BlockSpec((1, D_MODEL), const2),
            _resident(w1.shape, const2),
            _resident(w2.shape, const2),
        ],
        out_specs=pl.BlockSpec((None, tm, D_MODEL), tok),
        out_shape=jax.ShapeDtypeStruct(h.shape, F32),
        compiler_params=_params("parallel", "parallel"),
        name="even_tail",
    )(h, o_four, o_na, mod_rows, w_out, norm2_g, w1, w2)


HALO = 8


def _shift_rows(a, s):
    return pltpu.roll(a, s % a.shape[0], 0)


def _pool_kernel(hp_ref, h_ref, hn_ref, mod_ref, g1n_ref, wp_ref, ps_ref, g2n_ref, w1_ref, w2_ref, o_ref, *, seq_len):
    j = pl.program_id(1)
    tm = h_ref.shape[0]
    h = h_ref[...]
    gain, shift, scale = g1n_ref[...], _mod_chunk(mod_ref, 0), _mod_chunk(mod_ref, 1)
    u = _norm_modulate(h, gain, shift, scale)
    u_prev = jnp.where(j > 0, _norm_modulate(hp_ref[...], gain, shift, scale), 0.0)
    u_next = jnp.where(j < pl.num_programs(1) - 1, _norm_modulate(hn_ref[...], gain, shift, scale), 0.0)
    ue = jnp.concatenate([u_prev, u, u_next], axis=0)

    gc = POOL_GC
    s2 = ue + _shift_rows(ue, 1)
    s4 = _shift_rows(s2[:, gc:], 1) + _shift_rows(s2[:, gc:], -1)
    s8 = _shift_rows(s4[:, gc:], 2) + _shift_rows(s4[:, gc:], -2)
    s16 = _shift_rows(s8[:, gc:], 4) + _shift_rows(s8[:, gc:], -4)
    sums = (s2[:, :gc], s4[:, :gc], s8[:, :gc], s16)

    t = j * tm + lax.broadcasted_iota(jnp.int32, (tm, 1), 0)
    ys = []
    for gi, w in enumerate(POOL_SIZES):
        lo = jnp.maximum(t - w // 2, 0)
        hi = jnp.minimum(t + w - w // 2, seq_len)
        inv_cnt = 1.0 / (hi - lo).astype(F32)
        pooled = sums[gi][HALO:HALO + tm] * inv_cnt - u[:, gi * gc:(gi + 1) * gc]
        ys.append(_dot(pooled.astype(BF16), wp_ref[gi]))
    y = jnp.concatenate(ys, axis=1) * ps_ref[...]
    o_ref[...] = _residual_mlp(h, y, mod_ref, g2n_ref, w1_ref, w2_ref)


def _odd_layer(h, mod_rows, norm1_g, w_pool, pool_scale, norm2_g, w1, w2, tm):
    b, n, _ = h.shape
    tok = lambda bi, j: (bi, j, 0)
    const2 = lambda bi, j: (0, 0)
    per_tile = tm // HALO
    last = n // HALO - 1
    return pl.pallas_call(
        functools.partial(_pool_kernel, seq_len=n),
        grid=(b, n // tm),
        in_specs=[
            pl.BlockSpec((None, HALO, D_MODEL), lambda bi, j: (bi, jnp.maximum(j * per_tile - 1, 0), 0)),
            pl.BlockSpec((None, tm, D_MODEL), tok),
            pl.BlockSpec((None, HALO, D_MODEL), lambda bi, j: (bi, jnp.minimum((j + 1) * per_tile, last), 0)),
            pl.BlockSpec((None, 1, N_MOD * D_MODEL), lambda bi, j: (bi, 0, 0)),
            pl.BlockSpec((1, D_MODEL), const2),
            _resident(w_pool.shape, lambda bi, j: (0, 0, 0)),
            pl.BlockSpec((1, D_MODEL), const2),
            pl.BlockSpec((1, D_MODEL), const2),
            _resident(w1.shape, const2),
            _resident(w2.shape, const2),
        ],
        out_specs=pl.BlockSpec((None, tm, D_MODEL), tok),
        out_shape=jax.ShapeDtypeStruct(h.shape, F32),
        compiler_params=_params("parallel", "parallel"),
        name="odd_layer",
    )(h, h, h, mod_rows, norm1_g, w_pool, pool_scale, norm2_g, w1, w2)


def _head_segments():
    lanes = 256
    seg = (np.arange(lanes)[:, None] // NA_HEAD_DIM) == (np.arange(lanes)[None, :] // NA_HEAD_DIM)
    return jnp.asarray(seg.astype(np.float32)).astype(BF16)


def kernel(x, c, ctx, c_ctx, w_mod, b_mod, norm1_g, norm2_g, w_in_even, w_four, q_norm_g, k_norm_g, rpb, w_out_even, w_pool, pool_scale, w_mlp1, w_mlp2):
    b = x.shape[0]
    depth = w_mod.shape[0]
    assert depth == 2, "layer 0 is the even (Fourier + attention) layer, layer 1 the odd (pooling) layer"
    ctx_row = b
    pad = (-(b + 1)) % SUBLANES
    c_rows = jnp.concatenate([c, c_ctx[None, :], jnp.zeros((pad, D_MODEL), F32)], axis=0)
    mod = _modulation(c_rows, w_mod, b_mod)
    mod = mod.reshape(depth, mod.shape[1], 1, N_MOD * D_MODEL)

    row = lambda a: a.reshape(1, -1)
    heads = lambda g: jnp.tile(g, NA_HEADS).reshape(1, NA_WIDTH)
    seg = _head_segments()

    w_in = w_in_even[0].astype(BF16)
    qg, kg = heads(q_norm_g[0]), heads(k_norm_g[0])
    f, q, k, v = _in_projection(x, mod[0], lambda bi: bi, row(norm1_g[0]), w_in, qg, kg, seg, tm=512)
    _, _, kc, vc = _in_projection(ctx, mod[0], lambda bi: ctx_row, row(norm1_g[0]), w_in, qg, kg, seg,
                                  tm=ctx.shape[1])
    o_four = _fourier_mix(f, w_four[0])
    o_na = _neighbourhood_attention(q, k, v, kc, vc, rpb[0])
    h = _even_tail(x, o_four, o_na, mod[0], w_out_even[0].astype(BF16), row(norm2_g[0]),
                   w_mlp1[0].astype(BF16), w_mlp2[0].astype(BF16), tm=512)

    h = _odd_layer(h, mod[1], row(norm1_g[1]), w_pool[0].astype(BF16), row(pool_scale[0]), row(norm2_g[1]),
                   w_mlp1[1].astype(BF16), w_mlp2[1].astype(BF16), tm=512)
    return h
```

```python
import functools

import numpy as np
import jax
import jax.numpy as jnp
from jax import lax
from jax.experimental import pallas as pl
from jax.experimental.pallas import tpu as pltpu

D_MODEL = 1024
GRID_W = 64
FOUR_GROUPS = 4
FOUR_GC = 128
FOUR_WIDTH = 512
NA_HEADS = 8
NA_HEAD_DIM = 64
NA_WIDTH = 512
NA_WIN_ROWS = 8
NA_WIN_COLS = 16
POOL_SIZES = (2, 4, 8, 16)
POOL_GC = 256
MLP_HIDDEN = 4096
N_MOD = 6
RMS_EPS = 1e-6

VMEM_LIMIT_BYTES = 56 * 1024 * 1024
SUBLANES = 8

F32 = jnp.float32
BF16 = jnp.bfloat16
MASK_VALUE = -1e30
LOG2_E = 1.4426950408889634

FFT_N1 = 128
FFT_N2 = 64
Y1_PITCH = 2 * FFT_N1 + 8
Z_PITCH = FFT_N1 + 8
FFT_UNROLL = 8

NA_Q_ROWS = 4
NA_BAND_ROWS = NA_Q_ROWS + NA_WIN_ROWS
NA_CHAINS = 8
TOKEN_CHAINS = 4
INPROJ_CHAIN_ROWS = 256
MLP_CHUNK = 1024
MLP_CHUNKS = MLP_HIDDEN // MLP_CHUNK


def _dot(a, b):
    return jnp.dot(a, b, preferred_element_type=F32)


def _dot_f32(a, b):
    return jnp.dot(a, b, preferred_element_type=F32, precision=lax.Precision.HIGHEST)


def _dot_nt(a, b):
    return lax.dot_general(a, b, (((1,), (1,)), ((), ())), preferred_element_type=F32)


def _params(*sem):
    return pltpu.CompilerParams(dimension_semantics=sem, vmem_limit_bytes=VMEM_LIMIT_BYTES)


def _resident(shape, index_map):
    return pl.BlockSpec(shape, index_map, pipeline_mode=pl.Buffered(1))


def _mod_kernel(c_ref, w_ref, b_ref, o_ref):
    c = c_ref[...]
    s = c * (1.0 / (1.0 + jnp.exp(-c)))
    o_ref[...] = _dot_f32(s, w_ref[...]) + b_ref[...]


def _modulation(c_rows, w_mod, b_mod):
    depth = w_mod.shape[0]
    rows = c_rows.shape[0]
    tn = D_MODEL
    return pl.pallas_call(
        _mod_kernel,
        grid=(depth, N_MOD * D_MODEL // tn),
        in_specs=[
            pl.BlockSpec((rows, D_MODEL), lambda l, j: (0, 0)),
            pl.BlockSpec((None, D_MODEL, tn), lambda l, j: (l, 0, j)),
            pl.BlockSpec((None, 1, tn), lambda l, j: (l, 0, j)),
        ],
        out_specs=pl.BlockSpec((None, rows, tn), lambda l, j: (l, 0, j)),
        out_shape=jax.ShapeDtypeStruct((depth, rows, N_MOD * D_MODEL), F32),
        compiler_params=_params("parallel", "parallel"),
        name="modulation",
    )(c_rows, w_mod, b_mod.reshape(depth, 1, N_MOD * D_MODEL))


def _norm_modulate(x, gain, shift, scale):
    ms = jnp.mean(x * x, axis=-1, keepdims=True)
    return (x * lax.rsqrt(ms + RMS_EPS)) * (gain * (1.0 + scale)) + shift


def _head_norm(a, gain, seg_ref):
    sq = (a * a).astype(BF16)
    half = seg_ref.shape[0]
    ms = jnp.concatenate([_dot(sq[:, :half], seg_ref[...]), _dot(sq[:, half:], seg_ref[...])], axis=1)
    return a * lax.rsqrt(ms * (1.0 / NA_HEAD_DIM) + RMS_EPS) * gain


def _inproj_kernel(x_ref, mod_ref, g_ref, w_ref, qg_ref, kg_ref, seg_ref, f_ref, q_ref, k_ref, v_ref):
    tm = x_ref.shape[0]
    chains = max(1, tm // INPROJ_CHAIN_ROWS)
    tc = tm // chains

    def normed(i):
        x = x_ref[i * tc:(i + 1) * tc, :]
        return _norm_modulate(x, g_ref[...], mod_ref[:, 0:D_MODEL], mod_ref[:, D_MODEL:2 * D_MODEL]).astype(BF16)

    def outputs(i, p):
        rows = slice(i * tc, (i + 1) * tc)
        f_ref[rows, :] = p[:, :FOUR_WIDTH]
        q = p[:, FOUR_WIDTH:FOUR_WIDTH + NA_WIDTH]
        k = p[:, FOUR_WIDTH + NA_WIDTH:FOUR_WIDTH + 2 * NA_WIDTH]
        q_ref[rows, :] = (_head_norm(q, qg_ref[...], seg_ref) * (NA_HEAD_DIM ** -0.5 * LOG2_E)).astype(BF16)
        k_ref[rows, :] = _head_norm(k, kg_ref[...], seg_ref).astype(BF16)
        v_ref[rows, :] = p[:, FOUR_WIDTH + 2 * NA_WIDTH:].astype(BF16)

    u, p = {}, {}
    for step in range(chains + 2):
        if step < chains:
            u[step] = normed(step)
        if 0 <= step - 1 < chains:
            p[step - 1] = _dot(u.pop(step - 1), w_ref[...])
        if 0 <= step - 2 < chains:
            outputs(step - 2, p.pop(step - 2))


def _in_projection(x, mod_rows, row_of_batch, norm_g, w_in, q_g, k_g, seg, tm):
    b, n, _ = x.shape
    width = w_in.shape[1]
    tok = lambda bi, j: (bi, j, 0)
    const2 = lambda bi, j: (0, 0)
    out_tok = lambda w: pl.BlockSpec((None, tm, w), tok)
    return pl.pallas_call(
        _inproj_kernel,
        grid=(b, n // tm),
        in_specs=[
            pl.BlockSpec((None, tm, D_MODEL), tok),
            pl.BlockSpec((None, 1, N_MOD * D_MODEL), lambda bi, j: (row_of_batch(bi), 0, 0)),
            pl.BlockSpec((1, D_MODEL), const2),
            _resident((D_MODEL, width), const2),
            pl.BlockSpec((1, NA_WIDTH), const2),
            pl.BlockSpec((1, NA_WIDTH), const2),
            pl.BlockSpec(seg.shape, const2),
        ],
        out_specs=[out_tok(FOUR_WIDTH), out_tok(NA_WIDTH), out_tok(NA_WIDTH), out_tok(NA_WIDTH)],
        out_shape=[
            jax.ShapeDtypeStruct((b, n, FOUR_WIDTH), F32),
            jax.ShapeDtypeStruct((b, n, NA_WIDTH), BF16),
            jax.ShapeDtypeStruct((b, n, NA_WIDTH), BF16),
            jax.ShapeDtypeStruct((b, n, NA_WIDTH), BF16),
        ],
        compiler_params=_params("parallel", "parallel"),
        name="in_projection",
    )(x, mod_rows, norm_g, w_in, q_g, k_g, seg)


def _fft_tables(n):
    n1, n2 = FFT_N1, FFT_N2
    assert n1 * n2 == n
    k1 = np.arange(n1)[None, :, None]
    pos = (n2 * np.arange(n1)[None, None, :] + np.arange(n2)[:, None, None])
    ang = 2.0 * np.pi * ((k1 * pos) % n) / n
    l1 = np.concatenate([np.cos(ang), -np.sin(ang)], axis=1)
    a2 = 2.0 * np.pi * ((np.arange(n2)[:, None] * np.arange(n2)[None, :]) % n2) / n2
    c2, s2 = np.cos(a2), np.sin(a2)
    l2 = np.block([[c2, s2], [-s2, c2]])
    ac = 2.0 * np.pi * ((np.arange(FOUR_GC)[:, None] * np.arange(FOUR_GC)[None, :]) % FOUR_GC) / FOUR_GC
    cs = np.concatenate([np.cos(ac), np.sin(ac)], axis=0)
    return l1.astype(np.float32), l2.astype(np.float32), cs.astype(np.float32)


def _fft_kernel(f_ref, l1_ref, l2_ref, cs_ref, w_ref, o_ref, y1_ref, z_ref, *, ortho):
    n1, n2 = FFT_N1, FFT_N2

    def stage1(j, carry):
        r = f_ref[pl.ds(j, n1, stride=n2), :].astype(BF16)
        y1_ref[pl.ds(pl.multiple_of(j * Y1_PITCH, SUBLANES), 2 * n1), :] = _dot(l1_ref[j], r)
        return carry

    lax.fori_loop(0, n2, stage1, 0, unroll=FFT_UNROLL)

    l2 = l2_ref[...]

    def stage2(k1, carry):
        yr = y1_ref[pl.ds(k1, n2, stride=Y1_PITCH), :]
        yi = y1_ref[pl.ds(n1 + k1, n2, stride=Y1_PITCH), :]
        z = _dot(l2, jnp.concatenate([yr, yi], axis=0).astype(BF16))
        z_ref[0, pl.ds(k1, n2, stride=Z_PITCH), :] = z[:n2]
        z_ref[1, pl.ds(k1, n2, stride=Z_PITCH), :] = z[n2:]
        return carry

    lax.fori_loop(0, n1, stage2, 0, unroll=FFT_UNROLL)

    wf = (_dot_f32(cs_ref[...], w_ref[...]) * ortho).astype(BF16)
    wr, wi = wf[:FOUR_GC], wf[FOUR_GC:]

    def stage3(k2, carry):
        base = pl.multiple_of(k2 * Z_PITCH, SUBLANES)
        zr = z_ref[0, pl.ds(base, n1), :].astype(BF16)
        zi = z_ref[1, pl.ds(base, n1), :].astype(BF16)
        o_ref[pl.ds(pl.multiple_of(k2 * n1, n1), n1), :] = (_dot(zr, wr) + _dot(zi, wi)).astype(o_ref.dtype)
        return carry

    lax.fori_loop(0, n2, stage3, 0, unroll=FFT_UNROLL)


def _fourier_mix(f, w_four):
    b, n, _ = f.shape
    l1, l2, cs = _fft_tables(n)
    l1 = jnp.asarray(l1).astype(BF16)
    l2 = jnp.asarray(l2).astype(BF16)
    cs = jnp.asarray(cs)
    ortho = float(1.0 / np.sqrt(n * FOUR_GC))
    slab = lambda bi, g: (bi, 0, g)
    return pl.pallas_call(
        functools.partial(_fft_kernel, ortho=ortho),
        grid=(b, FOUR_GROUPS),
        in_specs=[
            pl.BlockSpec((None, n, FOUR_GC), slab),
            _resident(l1.shape, lambda bi, g: (0, 0, 0)),
            _resident(l2.shape, lambda bi, g: (0, 0)),
            _resident(cs.shape, lambda bi, g: (0, 0)),
            pl.BlockSpec((None, FOUR_GC, FOUR_GC), lambda bi, g: (g, 0, 0)),
        ],
        out_specs=pl.BlockSpec((None, n, FOUR_GC), slab),
        out_shape=jax.ShapeDtypeStruct((b, n, FOUR_WIDTH), BF16),
        scratch_shapes=[
            pltpu.VMEM((FFT_N2 * Y1_PITCH, FOUR_GC), F32),
            pltpu.VMEM((2, FFT_N2 * Z_PITCH, FOUR_GC), F32),
        ],
        compiler_params=_params("parallel", "parallel"),
        name="fourier_mix",
    )(f, l1, l2, cs, w_four)


def _band_start(i, rows):
    return jnp.clip(NA_Q_ROWS * i - (NA_BAND_ROWS - NA_Q_ROWS) // 2, 0, rows - NA_BAND_ROWS)


def _natten_bias_tables(rpb, rows):
    heads = rpb.shape[0]
    groups = rows // NA_Q_ROWS
    n_dr, n_dc = 2 * NA_WIN_ROWS - 1, 2 * NA_WIN_COLS - 1
    qc = np.arange(GRID_W)
    col_start = np.clip(qc - NA_WIN_COLS // 2, 0, GRID_W - NA_WIN_COLS)
    col_ok = (qc[None, :] >= col_start[:, None]) & (qc[None, :] < col_start[:, None] + NA_WIN_COLS)
    dc = np.clip(qc[None, :] - qc[:, None] + (NA_WIN_COLS - 1), 0, n_dc - 1)
    onehot_dc = (dc[None] == np.arange(n_dc)[:, None, None]).astype(np.float32)
    tiles = jnp.einsum('hre,eqk->hrqk', rpb.astype(F32), jnp.asarray(onehot_dc), precision=lax.Precision.HIGHEST)
    tiles = jnp.where(jnp.asarray(col_ok), tiles * LOG2_E, MASK_VALUE)
    masked_tile = jnp.full((heads, GRID_W, GRID_W), MASK_VALUE, F32)
    kinds = []
    for i in (0, groups // 2, groups - 1):
        band0 = int(np.clip(NA_Q_ROWS * i - (NA_BAND_ROWS - NA_Q_ROWS) // 2, 0, rows - NA_BAND_ROWS))
        per_row = []
        for a in range(NA_Q_ROWS):
            qr = NA_Q_ROWS * i + a
            row_start = int(np.clip(qr - NA_WIN_ROWS // 2, 0, rows - NA_WIN_ROWS))
            assert band0 <= row_start and row_start + NA_WIN_ROWS <= band0 + NA_BAND_ROWS
            band = []
            for bb in range(NA_BAND_ROWS):
                kr = band0 + bb
                in_window = row_start <= kr < row_start + NA_WIN_ROWS
                band.append(tiles[:, kr - qr + NA_WIN_ROWS - 1] if in_window else masked_tile)
            per_row.append(jnp.concatenate(band, axis=-1))
        kinds.append(jnp.concatenate(per_row, axis=1))
    table = jnp.stack(kinds, axis=1)
    nq, nk = NA_Q_ROWS * GRID_W, NA_BAND_ROWS * GRID_W
    table = table.reshape(heads // 2, 2, 3, nq, nk).transpose(0, 2, 1, 3, 4)
    return table.reshape(heads // 2, 3, 2 * nq, nk)


def _natten_kernel(q_ref, k_ref, v_ref, kc_ref, vc_ref, tab_ref, o_ref, *, rows):
    nq, nk = NA_Q_ROWS * GRID_W, NA_BAND_ROWS * GRID_W
    groups = rows // NA_Q_ROWS
    first_head = lax.broadcasted_iota(jnp.int32, (1, 2 * NA_HEAD_DIM), 1) < NA_HEAD_DIM
    kc, vc = kc_ref[...], vc_ref[...]
    tok0 = [pl.multiple_of(_band_start(pl.program_id(2) * NA_CHAINS + j, rows) * GRID_W, GRID_W * 4)
            for j in range(NA_CHAINS)]

    def scores(j):
        c = pl.program_id(2) * NA_CHAINS + j
        kind = jnp.where(c == 0, 0, jnp.where(c == groups - 1, 2, 1))
        q2 = q_ref[j * nq:(j + 1) * nq, :]
        zero = jnp.zeros_like(q2)
        qs = jnp.concatenate([jnp.where(first_head, q2, zero), jnp.where(first_head, zero, q2)], axis=0)
        return _dot_nt(qs, k_ref[pl.ds(tok0[j], nk), :]) + tab_ref[kind], _dot_nt(qs, kc)

    def softmax(s_win, s_ctx):
        m = jnp.maximum(jnp.max(s_win, axis=-1, keepdims=True), jnp.max(s_ctx, axis=-1, keepdims=True))
        p_win = jnp.exp2(s_win - m)
        p_ctx = jnp.exp2(s_ctx - m)
        l = jnp.sum(p_win, axis=-1, keepdims=True) + jnp.sum(p_ctx, axis=-1, keepdims=True)
        return p_win.astype(BF16), p_ctx.astype(BF16), l

    def weighted_values(j, p_win, p_ctx, l):
        o = (_dot(p_win, v_ref[pl.ds(tok0[j], nk), :]) + _dot(p_ctx, vc)) / l
        o_ref[j * nq:(j + 1) * nq, :] = jnp.where(first_head, o[:nq], o[nq:]).astype(o_ref.dtype)

    s, p = {}, {}
    for step in range(NA_CHAINS + 2):
        if step < NA_CHAINS:
            s[step] = scores(step)
        if 0 <= step - 1 < NA_CHAINS:
            p[step - 1] = softmax(*s.pop(step - 1))
        if 0 <= step - 2 < NA_CHAINS:
            weighted_values(step - 2, *p.pop(step - 2))


def _neighbourhood_attention(q, k, v, kc, vc, rpb):
    b, n, _ = q.shape
    rows = n // GRID_W
    steps = rows // (NA_Q_ROWS * NA_CHAINS)
    ctx_len = kc.shape[1]
    tq = NA_CHAINS * NA_Q_ROWS * GRID_W
    tables = _natten_bias_tables(rpb, rows)
    lanes = 2 * NA_HEAD_DIM
    seq = lambda bi, hp, i: (bi, 0, hp)
    return pl.pallas_call(
        functools.partial(_natten_kernel, rows=rows),
        grid=(b, NA_HEADS // 2, steps),
        in_specs=[
            pl.BlockSpec((None, tq, lanes), lambda bi, hp, i: (bi, i, hp)),
            pl.BlockSpec((None, n, lanes), seq),
            pl.BlockSpec((None, n, lanes), seq),
            pl.BlockSpec((None, ctx_len, lanes), seq),
            pl.BlockSpec((None, ctx_len, lanes), seq),
            pl.BlockSpec((None,) + tables.shape[1:], lambda bi, hp, i: (hp, 0, 0, 0)),
        ],
        out_specs=pl.BlockSpec((None, tq, lanes), lambda bi, hp, i: (bi, i, hp)),
        out_shape=jax.ShapeDtypeStruct((b, n, NA_WIDTH), BF16),
        compiler_params=_params("parallel", "parallel", "arbitrary"),
        name="neighbourhood_attention",
    )(q, k, v, kc, vc, tables)


def _mod_chunk(mod_ref, idx):
    return mod_ref[:, idx * D_MODEL:(idx + 1) * D_MODEL]


def _mlp_inputs(h, y, mod_ref, g2n_ref):
    h1 = h + _mod_chunk(mod_ref, 2) * y
    u = _norm_modulate(h1, g2n_ref[...], _mod_chunk(mod_ref, 3), _mod_chunk(mod_ref, 4)).astype(BF16)
    return h1, u


def _mlp_chunk(u, acc, c, w1_ref, w2_ref):
    lo, hi = c * MLP_CHUNK, (c + 1) * MLP_CHUNK
    a = jnp.maximum(_dot(u, w1_ref[:, lo:hi]), 0.0)
    part = _dot((a * a).astype(BF16), w2_ref[lo:hi, :])
    return part if acc is None else acc + part


def _run_chains(prologues, mod_ref, w1_ref, w2_ref, o_ref, tc):
    n = len(prologues)
    for stage in prologues[0][:-1]:
        stage()
    h1, u = prologues[0][-1]()
    for i in range(n):
        acc, nxt = None, None
        for c in range(MLP_CHUNKS):
            acc = _mlp_chunk(u, acc, c, w1_ref, w2_ref)
            if i + 1 < n:
                nxt = prologues[i + 1][c]()
        o_ref[i * tc:(i + 1) * tc, :] = h1 + _mod_chunk(mod_ref, 5) * acc
        if i + 1 < n:
            h1, u = nxt


def _even_tail_kernel(h_ref, of_ref, on_ref, mod_ref, wo_ref, g2n_ref, w1_ref, w2_ref, o_ref):
    tc = h_ref.shape[0] // TOKEN_CHAINS

    def prologue(i):
        rows = slice(i * tc, (i + 1) * tc)
        st = {}

        def s0():
            st["y"] = _dot(of_ref[rows, :], wo_ref[:FOUR_WIDTH, :])

        def s1():
            st["y"] = st["y"] + _dot(on_ref[rows, :], wo_ref[FOUR_WIDTH:, :])

        def s3():
            return _mlp_inputs(h_ref[rows, :], st["y"], mod_ref, g2n_ref)

        return [s0, s1, lambda: None, s3]

    _run_chains([prologue(i) for i in range(TOKEN_CHAINS)], mod_ref, w1_ref, w2_ref, o_ref, tc)


def _even_tail(h, o_four, o_na, mod_rows, w_out, norm2_g, w1, w2, tm):
    b, n, _ = h.shape
    tok = lambda bi, j: (bi, j, 0)
    const2 = lambda bi, j: (0, 0)
    return pl.pallas_call(
        _even_tail_kernel,
        grid=(b, n // tm),
        in_specs=[
            pl.BlockSpec((None, tm, D_MODEL), tok),
            pl.BlockSpec((None, tm, FOUR_WIDTH), tok),
            pl.BlockSpec((None, tm, NA_WIDTH), tok),
            pl.BlockSpec((None, 1, N_MOD * D_MODEL), lambda bi, j: (bi, 0, 0)),
            _resident(w_out.shape, const2),
            pl.BlockSpec((1, D_MODEL), const2),
            _resident(w1.shape, const2),
            _resident(w2.shape, const2),
        ],
        out_specs=pl.BlockSpec((None, tm, D_MODEL), tok),
        out_shape=jax.ShapeDtypeStruct(h.shape, F32),
        compiler_params=_params("parallel", "parallel"),
        name="even_tail",
    )(h, o_four, o_na, mod_rows, w_out, norm2_g, w1, w2)


HALO = 8


def _shift_rows(a, s):
    return pltpu.roll(a, s % a.shape[0], 0)


def _pool_kernel(hp_ref, h_ref, hn_ref, mod_ref, g1n_ref, wp_ref, ps_ref, g2n_ref, w1_ref, w2_ref, o_ref, *, seq_len):
    j = pl.program_id(1)
    tm = h_ref.shape[0]
    tc = tm // TOKEN_CHAINS
    gc = POOL_GC
    gain, shift, scale = g1n_ref[...], _mod_chunk(mod_ref, 0), _mod_chunk(mod_ref, 1)
    unorm = lambda x: _norm_modulate(x, gain, shift, scale)

    def prologue(i):
        r0 = i * tc
        st = {}

        def normed():
            u = unorm(h_ref[r0:r0 + tc, :])
            if i == 0:
                u_prev = jnp.where(j > 0, unorm(hp_ref[...]), 0.0)
            else:
                u_prev = unorm(h_ref[r0 - HALO:r0, :])
            if i == TOKEN_CHAINS - 1:
                u_next = jnp.where(j < pl.num_programs(1) - 1, unorm(hn_ref[...]), 0.0)
            else:
                u_next = unorm(h_ref[r0 + tc:r0 + tc + HALO, :])
            st["u"] = u
            st["ue"] = jnp.concatenate([u_prev, u, u_next], axis=0)

        def window_sums():
            ue = st["ue"]
            s2 = ue + _shift_rows(ue, 1)
            s4 = _shift_rows(s2[:, gc:], 1) + _shift_rows(s2[:, gc:], -1)
            s8 = _shift_rows(s4[:, gc:], 2) + _shift_rows(s4[:, gc:], -2)
            s16 = _shift_rows(s8[:, gc:], 4) + _shift_rows(s8[:, gc:], -4)
            st["sums"] = (s2[:, :gc], s4[:, :gc], s8[:, :gc], s16)

        def group_linear():
            t = j * tm + r0 + lax.broadcasted_iota(jnp.int32, (tc, 1), 0)
            ys = []
            for gi, w in enumerate(POOL_SIZES):
                lo = jnp.maximum(t - w // 2, 0)
                hi = jnp.minimum(t + w - w // 2, seq_len)
                inv_cnt = 1.0 / (hi - lo).astype(F32)
                pooled = st["sums"][gi][HALO:HALO + tc] * inv_cnt - st["u"][:, gi * gc:(gi + 1) * gc]
                ys.append(_dot(pooled.astype(BF16), wp_ref[gi]))
            st["y"] = jnp.concatenate(ys, axis=1) * ps_ref[...]

        def mlp_inputs():
            return _mlp_inputs(h_ref[r0:r0 + tc, :], st["y"], mod_ref, g2n_ref)

        return [normed, window_sums, group_linear, mlp_inputs]

    _run_chains([prologue(i) for i in range(TOKEN_CHAINS)], mod_ref, w1_ref, w2_ref, o_ref, tc)


def _odd_layer(h, mod_rows, norm1_g, w_pool, pool_scale, norm2_g, w1, w2, tm):
    b, n, _ = h.shape
    tok = lambda bi, j: (bi, j, 0)
    const2 = lambda bi, j: (0, 0)
    per_tile = tm // HALO
    last = n // HALO - 1
    return pl.pallas_call(
        functools.partial(_pool_kernel, seq_len=n),
        grid=(b, n // tm),
        in_specs=[
            pl.BlockSpec((None, HALO, D_MODEL), lambda bi, j: (bi, jnp.maximum(j * per_tile - 1, 0), 0)),
            pl.BlockSpec((None, tm, D_MODEL), tok),
            pl.BlockSpec((None, HALO, D_MODEL), lambda bi, j: (bi, jnp.minimum((j + 1) * per_tile, last), 0)),
            pl.BlockSpec((None, 1, N_MOD * D_MODEL), lambda bi, j: (bi, 0, 0)),
            pl.BlockSpec((1, D_MODEL), const2),
            _resident(w_pool.shape, lambda bi, j: (0, 0, 0)),
            pl.BlockSpec((1, D_MODEL), const2),
            pl.BlockSpec((1, D_MODEL), const2),
            _resident(w1.shape, const2),
            _resident(w2.shape, const2),
        ],
        out_specs=pl.BlockSpec((None, tm, D_MODEL), tok),
        out_shape=jax.ShapeDtypeStruct(h.shape, F32),
        compiler_params=_params("parallel", "parallel"),
        name="odd_layer",
    )(h, h, h, mod_rows, norm1_g, w_pool, pool_scale, norm2_g, w1, w2)


def _head_segments():
    lanes = 256
    seg = (np.arange(lanes)[:, None] // NA_HEAD_DIM) == (np.arange(lanes)[None, :] // NA_HEAD_DIM)
    return jnp.asarray(seg.astype(np.float32)).astype(BF16)


def kernel(x, c, ctx, c_ctx, w_mod, b_mod, norm1_g, norm2_g, w_in_even, w_four, q_norm_g, k_norm_g, rpb, w_out_even, w_pool, pool_scale, w_mlp1, w_mlp2):
    b = x.shape[0]
    depth = w_mod.shape[0]
    assert depth == 2, "layer 0 is the even (Fourier + attention) layer, layer 1 the odd (pooling) layer"
    ctx_row = b
    pad = (-(b + 1)) % SUBLANES
    c_rows = jnp.concatenate([c, c_ctx[None, :], jnp.zeros((pad, D_MODEL), F32)], axis=0)
    mod = _modulation(c_rows, w_mod, b_mod)
    mod = mod.reshape(depth, mod.shape[1], 1, N_MOD * D_MODEL)

    row = lambda a: a.reshape(1, -1)
    heads = lambda g: jnp.tile(g, NA_HEADS).reshape(1, NA_WIDTH)
    seg = _head_segments()

    w_in = w_in_even[0].astype(BF16)
    qg, kg = heads(q_norm_g[0]), heads(k_norm_g[0])
    f, q, k, v = _in_projection(x, mod[0], lambda bi: bi, row(norm1_g[0]), w_in, qg, kg, seg, tm=1024)
    _, _, kc, vc = _in_projection(ctx, mod[0], lambda bi: ctx_row, row(norm1_g[0]), w_in, qg, kg, seg,
                                  tm=ctx.shape[1])
    o_four = _fourier_mix(f, w_four[0])
    o_na = _neighbourhood_attention(q, k, v, kc, vc, rpb[0])
    h = _even_tail(x, o_four, o_na, mod[0], w_out_even[0].astype(BF16), row(norm2_g[0]),
                   w_mlp1[0].astype(BF16), w_mlp2[0].astype(BF16), tm=1024)

    h = _odd_layer(h, mod[1], row(norm1_g[1]), w_pool[0].astype(BF16), row(pool_scale[0]), row(norm2_g[1]),
                   w_mlp1[1].astype(BF16), w_mlp2[1].astype(BF16), tm=1024)
    return h
```

```python
import functools

import numpy as np
import jax
import jax.numpy as jnp
from jax import lax
from jax.experimental import pallas as pl
from jax.experimental.pallas import tpu as pltpu

D_MODEL = 1024
GRID_W = 64
FOUR_GROUPS = 4
FOUR_GC = 128
FOUR_WIDTH = 512
NA_HEADS = 8
NA_HEAD_DIM = 64
NA_WIDTH = 512
NA_WIN_ROWS = 8
NA_WIN_COLS = 16
POOL_SIZES = (2, 4, 8, 16)
POOL_GC = 256
MLP_HIDDEN = 4096
N_MOD = 6
RMS_EPS = 1e-6

VMEM_LIMIT_BYTES = 56 * 1024 * 1024
SUBLANES = 8

F32 = jnp.float32
BF16 = jnp.bfloat16
MASK_VALUE = -1e30
LOG2_E = 1.4426950408889634

FFT_N1 = 128
FFT_N2 = 64
Y1_PITCH = 2 * FFT_N1 + 8
Z_PITCH = FFT_N1 + 8
FFT_UNROLL = 8

NA_Q_ROWS = 8
NA_BAND_ROWS = NA_Q_ROWS + NA_WIN_ROWS
NA_Q_COLS = 16
NA_K_COLS = NA_Q_COLS + NA_WIN_COLS
NA_GROUPS_PER_STEP = 2
TOKEN_CHAINS = 4
INPROJ_CHAIN_ROWS = 256
MLP_CHUNK = 1024
MLP_CHUNKS = MLP_HIDDEN // MLP_CHUNK


def _dot(a, b):
    return jnp.dot(a, b, preferred_element_type=F32)


def _dot_f32(a, b):
    return jnp.dot(a, b, preferred_element_type=F32, precision=lax.Precision.HIGHEST)


def _dot_nt(a, b):
    return lax.dot_general(a, b, (((1,), (1,)), ((), ())), preferred_element_type=F32)


def _params(*sem):
    return pltpu.CompilerParams(dimension_semantics=sem, vmem_limit_bytes=VMEM_LIMIT_BYTES)


def _resident(shape, index_map):
    return pl.BlockSpec(shape, index_map, pipeline_mode=pl.Buffered(1))


def _mod_kernel(c_ref, w_ref, b_ref, o_ref):
    c = c_ref[...]
    s = c * (1.0 / (1.0 + jnp.exp(-c)))
    o_ref[...] = _dot_f32(s, w_ref[...]) + b_ref[...]


def _modulation(c_rows, w_mod, b_mod):
    depth = w_mod.shape[0]
    rows = c_rows.shape[0]
    tn = D_MODEL
    return pl.pallas_call(
        _mod_kernel,
        grid=(depth, N_MOD * D_MODEL // tn),
        in_specs=[
            pl.BlockSpec((rows, D_MODEL), lambda l, j: (0, 0)),
            pl.BlockSpec((None, D_MODEL, tn), lambda l, j: (l, 0, j)),
            pl.BlockSpec((None, 1, tn), lambda l, j: (l, 0, j)),
        ],
        out_specs=pl.BlockSpec((None, rows, tn), lambda l, j: (l, 0, j)),
        out_shape=jax.ShapeDtypeStruct((depth, rows, N_MOD * D_MODEL), F32),
        compiler_params=_params("parallel", "parallel"),
        name="modulation",
    )(c_rows, w_mod, b_mod.reshape(depth, 1, N_MOD * D_MODEL))


def _norm_modulate(x, gain, shift, scale):
    ms = jnp.mean(x * x, axis=-1, keepdims=True)
    return (x * lax.rsqrt(ms + RMS_EPS)) * (gain * (1.0 + scale)) + shift


def _head_norm(a, gain, seg_ref):
    sq = (a * a).astype(BF16)
    half = seg_ref.shape[0]
    ms = jnp.concatenate([_dot(sq[:, :half], seg_ref[...]), _dot(sq[:, half:], seg_ref[...])], axis=1)
    return a * lax.rsqrt(ms * (1.0 / NA_HEAD_DIM) + RMS_EPS) * gain


def _inproj_kernel(x_ref, mod_ref, g_ref, w_ref, qg_ref, kg_ref, seg_ref, f_ref, q_ref, k_ref, v_ref):
    tm = x_ref.shape[0]
    chains = max(1, tm // INPROJ_CHAIN_ROWS)
    tc = tm // chains

    def normed(i):
        x = x_ref[i * tc:(i + 1) * tc, :]
        return _norm_modulate(x, g_ref[...], mod_ref[:, 0:D_MODEL], mod_ref[:, D_MODEL:2 * D_MODEL]).astype(BF16)

    def outputs(i, p):
        rows = slice(i * tc, (i + 1) * tc)
        f_ref[rows, :] = p[:, :FOUR_WIDTH]
        q = p[:, FOUR_WIDTH:FOUR_WIDTH + NA_WIDTH]
        k = p[:, FOUR_WIDTH + NA_WIDTH:FOUR_WIDTH + 2 * NA_WIDTH]
        q_ref[rows, :] = (_head_norm(q, qg_ref[...], seg_ref) * (NA_HEAD_DIM ** -0.5 * LOG2_E)).astype(BF16)
        k_ref[rows, :] = _head_norm(k, kg_ref[...], seg_ref)
        v_ref[rows, :] = p[:, FOUR_WIDTH + 2 * NA_WIDTH:]

    u, p = {}, {}
    for step in range(chains + 2):
        if step < chains:
            u[step] = normed(step)
        if 0 <= step - 1 < chains:
            p[step - 1] = _dot(u.pop(step - 1), w_ref[...])
        if 0 <= step - 2 < chains:
            outputs(step - 2, p.pop(step - 2))


def _in_projection(x, mod_rows, row_of_batch, norm_g, w_in, q_g, k_g, seg, tm):
    b, n, _ = x.shape
    width = w_in.shape[1]
    tok = lambda bi, j: (bi, j, 0)
    const2 = lambda bi, j: (0, 0)
    out_tok = lambda w: pl.BlockSpec((None, tm, w), tok)
    return pl.pallas_call(
        _inproj_kernel,
        grid=(b, n // tm),
        in_specs=[
            pl.BlockSpec((None, tm, D_MODEL), tok),
            pl.BlockSpec((None, 1, N_MOD * D_MODEL), lambda bi, j: (row_of_batch(bi), 0, 0)),
            pl.BlockSpec((1, D_MODEL), const2),
            _resident((D_MODEL, width), const2),
            pl.BlockSpec((1, NA_WIDTH), const2),
            pl.BlockSpec((1, NA_WIDTH), const2),
            pl.BlockSpec(seg.shape, const2),
        ],
        out_specs=[out_tok(FOUR_WIDTH), out_tok(NA_WIDTH), out_tok(NA_WIDTH), out_tok(NA_WIDTH)],
        out_shape=[
            jax.ShapeDtypeStruct((b, n, FOUR_WIDTH), F32),
            jax.ShapeDtypeStruct((b, n, NA_WIDTH), BF16),
            jax.ShapeDtypeStruct((b, n, NA_WIDTH), F32),
            jax.ShapeDtypeStruct((b, n, NA_WIDTH), F32),
        ],
        compiler_params=_params("parallel", "parallel"),
        name="in_projection",
    )(x, mod_rows, norm_g, w_in, q_g, k_g, seg)


def _fft_tables(n):
    n1, n2 = FFT_N1, FFT_N2
    assert n1 * n2 == n
    k1 = np.arange(n1)[None, :, None]
    pos = (n2 * np.arange(n1)[None, None, :] + np.arange(n2)[:, None, None])
    ang = 2.0 * np.pi * ((k1 * pos) % n) / n
    l1 = np.concatenate([np.cos(ang), -np.sin(ang)], axis=1)
    a2 = 2.0 * np.pi * ((np.arange(n2)[:, None] * np.arange(n2)[None, :]) % n2) / n2
    c2, s2 = np.cos(a2), np.sin(a2)
    l2 = np.block([[c2, s2], [-s2, c2]])
    ac = 2.0 * np.pi * ((np.arange(FOUR_GC)[:, None] * np.arange(FOUR_GC)[None, :]) % FOUR_GC) / FOUR_GC
    cs = np.concatenate([np.cos(ac), np.sin(ac)], axis=0)
    return l1.astype(np.float32), l2.astype(np.float32), cs.astype(np.float32)


def _fft_kernel(f_ref, l1_ref, l2_ref, cs_ref, w_ref, o_ref, y1_ref, z_ref, *, ortho):
    n1, n2 = FFT_N1, FFT_N2

    def stage1(j, carry):
        r = f_ref[pl.ds(j, n1, stride=n2), :].astype(BF16)
        y1_ref[pl.ds(pl.multiple_of(j * Y1_PITCH, SUBLANES), 2 * n1), :] = _dot(l1_ref[j], r)
        return carry

    lax.fori_loop(0, n2, stage1, 0, unroll=FFT_UNROLL)

    l2 = l2_ref[...]

    def stage2(k1, carry):
        yr = y1_ref[pl.ds(k1, n2, stride=Y1_PITCH), :]
        yi = y1_ref[pl.ds(n1 + k1, n2, stride=Y1_PITCH), :]
        z = _dot(l2, jnp.concatenate([yr, yi], axis=0).astype(BF16))
        z_ref[0, pl.ds(k1, n2, stride=Z_PITCH), :] = z[:n2]
        z_ref[1, pl.ds(k1, n2, stride=Z_PITCH), :] = z[n2:]
        return carry

    lax.fori_loop(0, n1, stage2, 0, unroll=FFT_UNROLL)

    wf = (_dot_f32(cs_ref[...], w_ref[...]) * ortho).astype(BF16)
    wr, wi = wf[:FOUR_GC], wf[FOUR_GC:]

    def stage3(k2, carry):
        base = pl.multiple_of(k2 * Z_PITCH, SUBLANES)
        zr = z_ref[0, pl.ds(base, n1), :].astype(BF16)
        zi = z_ref[1, pl.ds(base, n1), :].astype(BF16)
        o_ref[pl.ds(pl.multiple_of(k2 * n1, n1), n1), :] = (_dot(zr, wr) + _dot(zi, wi)).astype(o_ref.dtype)
        return carry

    lax.fori_loop(0, n2, stage3, 0, unroll=FFT_UNROLL)


def _fourier_mix(f, w_four):
    b, n, _ = f.shape
    l1, l2, cs = _fft_tables(n)
    l1 = jnp.asarray(l1).astype(BF16)
    l2 = jnp.asarray(l2).astype(BF16)
    cs = jnp.asarray(cs)
    ortho = float(1.0 / np.sqrt(n * FOUR_GC))
    slab = lambda bi, g: (bi, 0, g)
    return pl.pallas_call(
        functools.partial(_fft_kernel, ortho=ortho),
        grid=(b, FOUR_GROUPS),
        in_specs=[
            pl.BlockSpec((None, n, FOUR_GC), slab),
            _resident(l1.shape, lambda bi, g: (0, 0, 0)),
            _resident(l2.shape, lambda bi, g: (0, 0)),
            _resident(cs.shape, lambda bi, g: (0, 0)),
            pl.BlockSpec((None, FOUR_GC, FOUR_GC), lambda bi, g: (g, 0, 0)),
        ],
        out_specs=pl.BlockSpec((None, n, FOUR_GC), slab),
        out_shape=jax.ShapeDtypeStruct((b, n, FOUR_WIDTH), BF16),
        scratch_shapes=[
            pltpu.VMEM((FFT_N2 * Y1_PITCH, FOUR_GC), F32),
            pltpu.VMEM((2, FFT_N2 * Z_PITCH, FOUR_GC), F32),
        ],
        compiler_params=_params("parallel", "parallel"),
        name="fourier_mix",
    )(f, l1, l2, cs, w_four)


def _band_start(g, rows):
    return jnp.clip(NA_Q_ROWS * g - (NA_BAND_ROWS - NA_Q_ROWS) // 2, 0, rows - NA_BAND_ROWS)


def _patch_col_start(cb):
    return int(np.clip(NA_Q_COLS * cb - (NA_K_COLS - NA_Q_COLS) // 2, 0, GRID_W - NA_K_COLS))


def _natten_bias_tables(rpb, rows):
    heads = rpb.shape[0]
    groups = rows // NA_Q_ROWS
    n_dr, n_dc = 2 * NA_WIN_ROWS - 1, 2 * NA_WIN_COLS - 1
    col = np.arange(GRID_W)
    col_start = np.clip(col - NA_WIN_COLS // 2, 0, GRID_W - NA_WIN_COLS)
    col_ok = (col[None, :] >= col_start[:, None]) & (col[None, :] < col_start[:, None] + NA_WIN_COLS)
    dc = np.clip(col[None, :] - col[:, None] + (NA_WIN_COLS - 1), 0, n_dc - 1)
    onehot_dc = (dc[None] == np.arange(n_dc)[:, None, None]).astype(np.float32)
    exact = dict(precision=lax.Precision.HIGHEST)
    tiles = jnp.einsum('hre,eqk->hrqk', rpb.astype(F32), jnp.asarray(onehot_dc), **exact)
    tiles = jnp.where(jnp.asarray(col_ok), tiles * LOG2_E, MASK_VALUE)
    tiles = jnp.concatenate([tiles, jnp.full((heads, 1, GRID_W, GRID_W), MASK_VALUE, F32)], axis=1)

    pick = np.zeros((3, NA_Q_ROWS, NA_BAND_ROWS, n_dr + 1), np.float32)
    for kind, g in enumerate((0, groups // 2, groups - 1)):
        band0 = int(np.clip(NA_Q_ROWS * g - (NA_BAND_ROWS - NA_Q_ROWS) // 2, 0, rows - NA_BAND_ROWS))
        for a in range(NA_Q_ROWS):
            qr = NA_Q_ROWS * g + a
            row_start = int(np.clip(qr - NA_WIN_ROWS // 2, 0, rows - NA_WIN_ROWS))
            assert band0 <= row_start and row_start + NA_WIN_ROWS <= band0 + NA_BAND_ROWS
            for bb in range(NA_BAND_ROWS):
                kr = band0 + bb
                in_window = row_start <= kr < row_start + NA_WIN_ROWS
                pick[kind, a, bb, kr - qr + NA_WIN_ROWS - 1 if in_window else n_dr] = 1.0
    per_block = []
    for cb in range(GRID_W // NA_Q_COLS):
        q0, k0 = NA_Q_COLS * cb, _patch_col_start(cb)
        assert col_start[q0] >= k0 and col_start[q0 + NA_Q_COLS - 1] + NA_WIN_COLS <= k0 + NA_K_COLS
        sub = tiles[:, :, q0:q0 + NA_Q_COLS, k0:k0 + NA_K_COLS]
        per_block.append(jnp.einsum('kabd,hdqc->hkaqbc', jnp.asarray(pick), sub, **exact))
    table = jnp.stack(per_block, axis=2)
    nq, nk = NA_Q_ROWS * NA_Q_COLS, NA_BAND_ROWS * NA_K_COLS
    n_cb = GRID_W // NA_Q_COLS
    table = table.reshape(heads // 2, 2, 3 * n_cb, nq, nk).transpose(0, 2, 1, 3, 4)
    return table.reshape(heads // 2, 3 * n_cb, 2 * nq, nk)


def _natten_kernel(q_ref, k_ref, v_ref, kc_ref, vc_ref, tab_ref, o_ref, *, rows):
    nq = NA_Q_ROWS * NA_Q_COLS
    groups = rows // NA_Q_ROWS
    n_cb = GRID_W // NA_Q_COLS
    first_head = lax.broadcasted_iota(jnp.int32, (1, 2 * NA_HEAD_DIM), 1) < NA_HEAD_DIM
    kc, vc = kc_ref[...].astype(BF16), vc_ref[...].astype(BF16)
    chains = [(g, cb) for g in range(NA_GROUPS_PER_STEP) for cb in range(n_cb)]

    def group_index(g):
        return pl.program_id(2) * NA_GROUPS_PER_STEP + g

    def q_rows(g, cb, a):
        start = (g * NA_Q_ROWS + a) * GRID_W + cb * NA_Q_COLS
        return slice(start, start + NA_Q_COLS)

    def patch(ref, g, cb):
        tok0 = _band_start(group_index(g), rows) * GRID_W + _patch_col_start(cb)
        parts = [ref[pl.ds(pl.multiple_of(tok0 + bb * GRID_W, SUBLANES), NA_K_COLS), :] for bb in range(NA_BAND_ROWS)]
        return jnp.concatenate(parts, axis=0).astype(BF16)

    def scores(g, cb):
        gi = group_index(g)
        kind = jnp.where(gi == 0, 0, jnp.where(gi == groups - 1, 2, 1))
        q2 = jnp.concatenate([q_ref[q_rows(g, cb, a), :] for a in range(NA_Q_ROWS)], axis=0)
        zero = jnp.zeros_like(q2)
        qs = jnp.concatenate([jnp.where(first_head, q2, zero), jnp.where(first_head, zero, q2)], axis=0)
        return _dot_nt(qs, patch(k_ref, g, cb)) + tab_ref[kind * n_cb + cb], _dot_nt(qs, kc)

    def softmax(s_win, s_ctx):
        m = jnp.maximum(jnp.max(s_win, axis=-1, keepdims=True), jnp.max(s_ctx, axis=-1, keepdims=True))
        p_win = jnp.exp2(s_win - m)
        p_ctx = jnp.exp2(s_ctx - m)
        l = jnp.sum(p_win, axis=-1, keepdims=True) + jnp.sum(p_ctx, axis=-1, keepdims=True)
        return p_win.astype(BF16), p_ctx.astype(BF16), l

    def weighted_values(g, cb, p_win, p_ctx, l):
        o = (_dot(p_win, patch(v_ref, g, cb)) + _dot(p_ctx, vc)) / l
        o = jnp.where(first_head, o[:nq], o[nq:]).astype(o_ref.dtype)
        for a in range(NA_Q_ROWS):
            o_ref[q_rows(g, cb, a), :] = o[a * NA_Q_COLS:(a + 1) * NA_Q_COLS]

    s, p = {}, {}
    for step in range(len(chains) + 2):
        if step < len(chains):
            s[step] = scores(*chains[step])
        if 0 <= step - 1 < len(chains):
            p[step - 1] = softmax(*s.pop(step - 1))
        if 0 <= step - 2 < len(chains):
            weighted_values(*chains[step - 2], *p.pop(step - 2))


def _neighbourhood_attention(q, k, v, kc, vc, rpb):
    b, n, _ = q.shape
    rows = n // GRID_W
    steps = rows // (NA_Q_ROWS * NA_GROUPS_PER_STEP)
    ctx_len = kc.shape[1]
    tq = NA_GROUPS_PER_STEP * NA_Q_ROWS * GRID_W
    tables = _natten_bias_tables(rpb, rows)
    lanes = 2 * NA_HEAD_DIM
    seq = lambda bi, hp, i: (bi, 0, hp)
    return pl.pallas_call(
        functools.partial(_natten_kernel, rows=rows),
        grid=(b, NA_HEADS // 2, steps),
        in_specs=[
            pl.BlockSpec((None, tq, lanes), lambda bi, hp, i: (bi, i, hp)),
            pl.BlockSpec((None, n, lanes), seq),
            pl.BlockSpec((None, n, lanes), seq),
            pl.BlockSpec((None, ctx_len, lanes), seq),
            pl.BlockSpec((None, ctx_len, lanes), seq),
            pl.BlockSpec((None,) + tables.shape[1:], lambda bi, hp, i: (hp, 0, 0, 0)),
        ],
        out_specs=pl.BlockSpec((None, tq, lanes), lambda bi, hp, i: (bi, i, hp)),
        out_shape=jax.ShapeDtypeStruct((b, n, NA_WIDTH), BF16),
        compiler_params=_params("parallel", "parallel", "arbitrary"),
        name="neighbourhood_attention",
    )(q, k, v, kc, vc, tables)


def _mod_chunk(mod_ref, idx):
    return mod_ref[:, idx * D_MODEL:(idx + 1) * D_MODEL]


def _mlp_inputs(h, y, mod_ref, g2n_ref):
    h1 = h + _mod_chunk(mod_ref, 2) * y
    u = _norm_modulate(h1, g2n_ref[...], _mod_chunk(mod_ref, 3), _mod_chunk(mod_ref, 4)).astype(BF16)
    return h1, u


def _mlp_chunk(u, acc, c, w1_ref, w2_ref):
    lo, hi = c * MLP_CHUNK, (c + 1) * MLP_CHUNK
    a = jnp.maximum(_dot(u, w1_ref[:, lo:hi]), 0.0)
    part = _dot((a * a).astype(BF16), w2_ref[lo:hi, :])
    return part if acc is None else acc + part


def _run_chains(prologues, mod_ref, w1_ref, w2_ref, o_ref, tc):
    n = len(prologues)
    for stage in prologues[0][:-1]:
        stage()
    h1, u = prologues[0][-1]()
    for i in range(n):
        acc, nxt = None, None
        for c in range(MLP_CHUNKS):
            acc = _mlp_chunk(u, acc, c, w1_ref, w2_ref)
            if i + 1 < n:
                nxt = prologues[i + 1][c]()
        o_ref[i * tc:(i + 1) * tc, :] = h1 + _mod_chunk(mod_ref, 5) * acc
        if i + 1 < n:
            h1, u = nxt


def _even_tail_kernel(h_ref, of_ref, on_ref, mod_ref, wo_ref, g2n_ref, w1_ref, w2_ref, o_ref):
    tc = h_ref.shape[0] // TOKEN_CHAINS

    def prologue(i):
        rows = slice(i * tc, (i + 1) * tc)
        st = {}

        def s0():
            st["y"] = _dot(of_ref[rows, :], wo_ref[:FOUR_WIDTH, :])

        def s1():
            st["y"] = st["y"] + _dot(on_ref[rows, :], wo_ref[FOUR_WIDTH:, :])

        def s3():
            return _mlp_inputs(h_ref[rows, :], st["y"], mod_ref, g2n_ref)

        return [s0, s1, lambda: None, s3]

    _run_chains([prologue(i) for i in range(TOKEN_CHAINS)], mod_ref, w1_ref, w2_ref, o_ref, tc)


def _even_tail(h, o_four, o_na, mod_rows, w_out, norm2_g, w1, w2, tm):
    b, n, _ = h.shape
    tok = lambda bi, j: (bi, j, 0)
    const2 = lambda bi, j: (0, 0)
    return pl.pallas_call(
        _even_tail_kernel,
        grid=(b, n // tm),
        in_specs=[
            pl.BlockSpec((None, tm, D_MODEL), tok),
            pl.BlockSpec((None, tm, FOUR_WIDTH), tok),
            pl.BlockSpec((None, tm, NA_WIDTH), tok),
            pl.BlockSpec((None, 1, N_MOD * D_MODEL), lambda bi, j: (bi, 0, 0)),
            _resident(w_out.shape, const2),
            pl.BlockSpec((1, D_MODEL), const2),
            _resident(w1.shape, const2),
            _resident(w2.shape, const2),
        ],
        out_specs=pl.BlockSpec((None, tm, D_MODEL), tok),
        out_shape=jax.ShapeDtypeStruct(h.shape, F32),
        compiler_params=_params("parallel", "parallel"),
        name="even_tail",
    )(h, o_four, o_na, mod_rows, w_out, norm2_g, w1, w2)


HALO = 8


def _shift_rows(a, s):
    return pltpu.roll(a, s % a.shape[0], 0)


def _pool_kernel(hp_ref, h_ref, hn_ref, mod_ref, g1n_ref, wp_ref, ps_ref, g2n_ref, w1_ref, w2_ref, o_ref, *, seq_len):
    j = pl.program_id(1)
    tm = h_ref.shape[0]
    tc = tm // TOKEN_CHAINS
    gc = POOL_GC
    gain, shift, scale = g1n_ref[...], _mod_chunk(mod_ref, 0), _mod_chunk(mod_ref, 1)
    unorm = lambda x: _norm_modulate(x, gain, shift, scale)

    def prologue(i):
        r0 = i * tc
        st = {}

        def normed():
            u = unorm(h_ref[r0:r0 + tc, :])
            if i == 0:
                u_prev = jnp.where(j > 0, unorm(hp_ref[...]), 0.0)
            else:
                u_prev = unorm(h_ref[r0 - HALO:r0, :])
            if i == TOKEN_CHAINS - 1:
                u_next = jnp.where(j < pl.num_programs(1) - 1, unorm(hn_ref[...]), 0.0)
            else:
                u_next = unorm(h_ref[r0 + tc:r0 + tc + HALO, :])
            st["u"] = u
            st["ue"] = jnp.concatenate([u_prev, u, u_next], axis=0)

        def window_sums():
            ue = st["ue"]
            s2 = ue + _shift_rows(ue, 1)
            s4 = _shift_rows(s2[:, gc:], 1) + _shift_rows(s2[:, gc:], -1)
            s8 = _shift_rows(s4[:, gc:], 2) + _shift_rows(s4[:, gc:], -2)
            s16 = _shift_rows(s8[:, gc:], 4) + _shift_rows(s8[:, gc:], -4)
            st["sums"] = (s2[:, :gc], s4[:, :gc], s8[:, :gc], s16)

        def group_linear():
            t = j * tm + r0 + lax.broadcasted_iota(jnp.int32, (tc, 1), 0)
            ys = []
            for gi, w in enumerate(POOL_SIZES):
                lo = jnp.maximum(t - w // 2, 0)
                hi = jnp.minimum(t + w - w // 2, seq_len)
                inv_cnt = 1.0 / (hi - lo).astype(F32)
                pooled = st["sums"][gi][HALO:HALO + tc] * inv_cnt - st["u"][:, gi * gc:(gi + 1) * gc]
                ys.append(_dot(pooled.astype(BF16), wp_ref[gi]))
            st["y"] = jnp.concatenate(ys, axis=1) * ps_ref[...]

        def mlp_inputs():
            return _mlp_inputs(h_ref[r0:r0 + tc, :], st["y"], mod_ref, g2n_ref)

        return [normed, window_sums, group_linear, mlp_inputs]

    _run_chains([prologue(i) for i in range(TOKEN_CHAINS)], mod_ref, w1_ref, w2_ref, o_ref, tc)


def _odd_layer(h, mod_rows, norm1_g, w_pool, pool_scale, norm2_g, w1, w2, tm):
    b, n, _ = h.shape
    tok = lambda bi, j: (bi, j, 0)
    const2 = lambda bi, j: (0, 0)
    per_tile = tm // HALO
    last = n // HALO - 1
    return pl.pallas_call(
        functools.partial(_pool_kernel, seq_len=n),
        grid=(b, n // tm),
        in_specs=[
            pl.BlockSpec((None, HALO, D_MODEL), lambda bi, j: (bi, jnp.maximum(j * per_tile - 1, 0), 0)),
            pl.BlockSpec((None, tm, D_MODEL), tok),
            pl.BlockSpec((None, HALO, D_MODEL), lambda bi, j: (bi, jnp.minimum((j + 1) * per_tile, last), 0)),
            pl.BlockSpec((None, 1, N_MOD * D_MODEL), lambda bi, j: (bi, 0, 0)),
            pl.BlockSpec((1, D_MODEL), const2),
            _resident(w_pool.shape, lambda bi, j: (0, 0, 0)),
            pl.BlockSpec((1, D_MODEL), const2),
            pl.BlockSpec((1, D_MODEL), const2),
            _resident(w1.shape, const2),
            _resident(w2.shape, const2),
        ],
        out_specs=pl.BlockSpec((None, tm, D_MODEL), tok),
        out_shape=jax.ShapeDtypeStruct(h.shape, F32),
        compiler_params=_params("parallel", "parallel"),
        name="odd_layer",
    )(h, h, h, mod_rows, norm1_g, w_pool, pool_scale, norm2_g, w1, w2)


def _head_segments():
    lanes = 256
    seg = (np.arange(lanes)[:, None] // NA_HEAD_DIM) == (np.arange(lanes)[None, :] // NA_HEAD_DIM)
    return jnp.asarray(seg.astype(np.float32)).astype(BF16)


def kernel(x, c, ctx, c_ctx, w_mod, b_mod, norm1_g, norm2_g, w_in_even, w_four, q_norm_g, k_norm_g, rpb, w_out_even, w_pool, pool_scale, w_mlp1, w_mlp2):
    b = x.shape[0]
    depth = w_mod.shape[0]
    assert depth == 2, "layer 0 is the even (Fourier + attention) layer, layer 1 the odd (pooling) layer"
    ctx_row = b
    pad = (-(b + 1)) % SUBLANES
    c_rows = jnp.concatenate([c, c_ctx[None, :], jnp.zeros((pad, D_MODEL), F32)], axis=0)
    mod = _modulation(c_rows, w_mod, b_mod)
    mod = mod.reshape(depth, mod.shape[1], 1, N_MOD * D_MODEL)

    row = lambda a: a.reshape(1, -1)
    heads = lambda g: jnp.tile(g, NA_HEADS).reshape(1, NA_WIDTH)
    seg = _head_segments()

    w_in = w_in_even[0].astype(BF16)
    qg, kg = heads(q_norm_g[0]), heads(k_norm_g[0])
    f, q, k, v = _in_projection(x, mod[0], lambda bi: bi, row(norm1_g[0]), w_in, qg, kg, seg, tm=1024)
    _, _, kc, vc = _in_projection(ctx, mod[0], lambda bi: ctx_row, row(norm1_g[0]), w_in, qg, kg, seg,
                                  tm=ctx.shape[1])
    o_four = _fourier_mix(f, w_four[0])
    o_na = _neighbourhood_attention(q, k, v, kc, vc, rpb[0])
    h = _even_tail(x, o_four, o_na, mod[0], w_out_even[0].astype(BF16), row(norm2_g[0]),
                   w_mlp1[0].astype(BF16), w_mlp2[0].astype(BF16), tm=1024)

    h = _odd_layer(h, mod[1], row(norm1_g[1]), w_pool[0].astype(BF16), row(pool_scale[0]), row(norm2_g[1]),
                   w_mlp1[1].astype(BF16), w_mlp2[1].astype(BF16), tm=1024)
    return h
```

```python
import functools

import numpy as np
import jax
import jax.numpy as jnp
from jax import lax
from jax.experimental import pallas as pl
from jax.experimental.pallas import tpu as pltpu

D_MODEL = 1024
GRID_W = 64
FOUR_GROUPS = 4
FOUR_GC = 128
FOUR_WIDTH = 512
NA_HEADS = 8
NA_HEAD_DIM = 64
NA_WIDTH = 512
NA_WIN_ROWS = 8
NA_WIN_COLS = 16
POOL_SIZES = (2, 4, 8, 16)
POOL_GC = 256
MLP_HIDDEN = 4096
N_MOD = 6
RMS_EPS = 1e-6

VMEM_LIMIT_BYTES = 56 * 1024 * 1024
SUBLANES = 8

F32 = jnp.float32
BF16 = jnp.bfloat16
MASK_VALUE = -1e30
LOG2_E = 1.4426950408889634

FFT_N1 = 128
FFT_N2 = 64
Y1_PITCH = 2 * FFT_N1 + 8
Z_PITCH = FFT_N1 + 8
FFT_UNROLL = 8

NA_Q_ROWS = 8
NA_BAND_ROWS = NA_Q_ROWS + NA_WIN_ROWS
NA_Q_COLS = 16
NA_K_COLS = NA_Q_COLS + NA_WIN_COLS
NA_GROUPS_PER_STEP = 4
TOKEN_CHAINS = 4
INPROJ_CHAIN_ROWS = 256
MLP_CHUNK = 1024
MLP_CHUNKS = MLP_HIDDEN // MLP_CHUNK


def _dot(a, b):
    return jnp.dot(a, b, preferred_element_type=F32)


def _dot_f32(a, b):
    return jnp.dot(a, b, preferred_element_type=F32, precision=lax.Precision.HIGHEST)


def _dot_nt(a, b):
    return lax.dot_general(a, b, (((1,), (1,)), ((), ())), preferred_element_type=F32)


def _params(*sem):
    return pltpu.CompilerParams(dimension_semantics=sem, vmem_limit_bytes=VMEM_LIMIT_BYTES)


def _resident(shape, index_map):
    return pl.BlockSpec(shape, index_map, pipeline_mode=pl.Buffered(1))


def _mod_kernel(c_ref, w_ref, b_ref, o_ref):
    c = c_ref[...]
    s = c * (1.0 / (1.0 + jnp.exp(-c)))
    o_ref[...] = _dot_f32(s, w_ref[...]) + b_ref[...]


def _modulation(c_rows, w_mod, b_mod):
    depth = w_mod.shape[0]
    rows = c_rows.shape[0]
    tn = D_MODEL
    return pl.pallas_call(
        _mod_kernel,
        grid=(depth, N_MOD * D_MODEL // tn),
        in_specs=[
            pl.BlockSpec((rows, D_MODEL), lambda l, j: (0, 0)),
            pl.BlockSpec((None, D_MODEL, tn), lambda l, j: (l, 0, j)),
            pl.BlockSpec((None, 1, tn), lambda l, j: (l, 0, j)),
        ],
        out_specs=pl.BlockSpec((None, rows, tn), lambda l, j: (l, 0, j)),
        out_shape=jax.ShapeDtypeStruct((depth, rows, N_MOD * D_MODEL), F32),
        compiler_params=_params("parallel", "parallel"),
        name="modulation",
    )(c_rows, w_mod, b_mod.reshape(depth, 1, N_MOD * D_MODEL))


def _norm_modulate(x, gain, shift, scale):
    ms = jnp.mean(x * x, axis=-1, keepdims=True)
    return (x * lax.rsqrt(ms + RMS_EPS)) * (gain * (1.0 + scale)) + shift


def _head_norm(a, gain, seg_ref):
    sq = (a * a).astype(BF16)
    half = seg_ref.shape[0]
    ms = jnp.concatenate([_dot(sq[:, :half], seg_ref[...]), _dot(sq[:, half:], seg_ref[...])], axis=1)
    return a * lax.rsqrt(ms * (1.0 / NA_HEAD_DIM) + RMS_EPS) * gain


def _inproj_kernel(x_ref, mod_ref, g_ref, w_ref, qg_ref, kg_ref, seg_ref, f_ref, q_ref, k_ref, v_ref):
    tm = x_ref.shape[0]
    chains = max(1, tm // INPROJ_CHAIN_ROWS)
    tc = tm // chains

    def normed(i):
        x = x_ref[i * tc:(i + 1) * tc, :]
        return _norm_modulate(x, g_ref[...], mod_ref[:, 0:D_MODEL], mod_ref[:, D_MODEL:2 * D_MODEL]).astype(BF16)

    def outputs(i, p):
        rows = slice(i * tc, (i + 1) * tc)
        f_ref[rows, :] = p[:, :FOUR_WIDTH]
        q = p[:, FOUR_WIDTH:FOUR_WIDTH + NA_WIDTH]
        k = p[:, FOUR_WIDTH + NA_WIDTH:FOUR_WIDTH + 2 * NA_WIDTH]
        q_ref[rows, :] = (_head_norm(q, qg_ref[...], seg_ref) * (NA_HEAD_DIM ** -0.5 * LOG2_E)).astype(BF16)
        k_ref[rows, :] = _head_norm(k, kg_ref[...], seg_ref)
        v_ref[rows, :] = p[:, FOUR_WIDTH + 2 * NA_WIDTH:]

    u, p = {}, {}
    for step in range(chains + 2):
        if step < chains:
            u[step] = normed(step)
        if 0 <= step - 1 < chains:
            p[step - 1] = _dot(u.pop(step - 1), w_ref[...])
        if 0 <= step - 2 < chains:
            outputs(step - 2, p.pop(step - 2))


def _in_projection(x, mod_rows, row_of_batch, norm_g, w_in, q_g, k_g, seg, tm):
    b, n, _ = x.shape
    width = w_in.shape[1]
    tok = lambda bi, j: (bi, j, 0)
    const2 = lambda bi, j: (0, 0)
    out_tok = lambda w: pl.BlockSpec((None, tm, w), tok)
    return pl.pallas_call(
        _inproj_kernel,
        grid=(b, n // tm),
        in_specs=[
            pl.BlockSpec((None, tm, D_MODEL), tok),
            pl.BlockSpec((None, 1, N_MOD * D_MODEL), lambda bi, j: (row_of_batch(bi), 0, 0)),
            pl.BlockSpec((1, D_MODEL), const2),
            _resident((D_MODEL, width), const2),
            pl.BlockSpec((1, NA_WIDTH), const2),
            pl.BlockSpec((1, NA_WIDTH), const2),
            pl.BlockSpec(seg.shape, const2),
        ],
        out_specs=[out_tok(FOUR_WIDTH), out_tok(NA_WIDTH), out_tok(NA_WIDTH), out_tok(NA_WIDTH)],
        out_shape=[
            jax.ShapeDtypeStruct((b, n, FOUR_WIDTH), F32),
            jax.ShapeDtypeStruct((b, n, NA_WIDTH), BF16),
            jax.ShapeDtypeStruct((b, n, NA_WIDTH), F32),
            jax.ShapeDtypeStruct((b, n, NA_WIDTH), F32),
        ],
        compiler_params=_params("parallel", "parallel"),
        name="in_projection",
    )(x, mod_rows, norm_g, w_in, q_g, k_g, seg)


def _fft_tables(n):
    n1, n2 = FFT_N1, FFT_N2
    assert n1 * n2 == n
    k1 = np.arange(n1)[None, :, None]
    pos = (n2 * np.arange(n1)[None, None, :] + np.arange(n2)[:, None, None])
    ang = 2.0 * np.pi * ((k1 * pos) % n) / n
    l1 = np.concatenate([np.cos(ang), -np.sin(ang)], axis=1)
    a2 = 2.0 * np.pi * ((np.arange(n2)[:, None] * np.arange(n2)[None, :]) % n2) / n2
    c2, s2 = np.cos(a2), np.sin(a2)
    l2 = np.block([[c2, s2], [-s2, c2]])
    ac = 2.0 * np.pi * ((np.arange(FOUR_GC)[:, None] * np.arange(FOUR_GC)[None, :]) % FOUR_GC) / FOUR_GC
    cs = np.concatenate([np.cos(ac), np.sin(ac)], axis=0)
    return l1.astype(np.float32), l2.astype(np.float32), cs.astype(np.float32)


def _fft_kernel(f_ref, l1_ref, l2_ref, cs_ref, w_ref, o_ref, y1_ref, z_ref, *, ortho):
    n1, n2 = FFT_N1, FFT_N2

    def stage1(j, carry):
        r = f_ref[pl.ds(j, n1, stride=n2), :].astype(BF16)
        y1_ref[pl.ds(pl.multiple_of(j * Y1_PITCH, SUBLANES), 2 * n1), :] = _dot(l1_ref[j], r)
        return carry

    lax.fori_loop(0, n2, stage1, 0, unroll=FFT_UNROLL)

    l2 = l2_ref[...]

    def stage2(k1, carry):
        yr = y1_ref[pl.ds(k1, n2, stride=Y1_PITCH), :]
        yi = y1_ref[pl.ds(n1 + k1, n2, stride=Y1_PITCH), :]
        z = _dot(l2, jnp.concatenate([yr, yi], axis=0).astype(BF16))
        z_ref[0, pl.ds(k1, n2, stride=Z_PITCH), :] = z[:n2]
        z_ref[1, pl.ds(k1, n2, stride=Z_PITCH), :] = z[n2:]
        return carry

    lax.fori_loop(0, n1, stage2, 0, unroll=FFT_UNROLL)

    wf = (_dot_f32(cs_ref[...], w_ref[...]) * ortho).astype(BF16)
    wr, wi = wf[:FOUR_GC], wf[FOUR_GC:]

    def stage3(k2, carry):
        base = pl.multiple_of(k2 * Z_PITCH, SUBLANES)
        zr = z_ref[0, pl.ds(base, n1), :].astype(BF16)
        zi = z_ref[1, pl.ds(base, n1), :].astype(BF16)
        o_ref[pl.ds(pl.multiple_of(k2 * n1, n1), n1), :] = (_dot(zr, wr) + _dot(zi, wi)).astype(o_ref.dtype)
        return carry

    lax.fori_loop(0, n2, stage3, 0, unroll=FFT_UNROLL)


def _fourier_mix(f, w_four):
    b, n, _ = f.shape
    l1, l2, cs = _fft_tables(n)
    l1 = jnp.asarray(l1).astype(BF16)
    l2 = jnp.asarray(l2).astype(BF16)
    cs = jnp.asarray(cs)
    ortho = float(1.0 / np.sqrt(n * FOUR_GC))
    slab = lambda bi, g: (bi, 0, g)
    return pl.pallas_call(
        functools.partial(_fft_kernel, ortho=ortho),
        grid=(b, FOUR_GROUPS),
        in_specs=[
            pl.BlockSpec((None, n, FOUR_GC), slab),
            _resident(l1.shape, lambda bi, g: (0, 0, 0)),
            _resident(l2.shape, lambda bi, g: (0, 0)),
            _resident(cs.shape, lambda bi, g: (0, 0)),
            pl.BlockSpec((None, FOUR_GC, FOUR_GC), lambda bi, g: (g, 0, 0)),
        ],
        out_specs=pl.BlockSpec((None, n, FOUR_GC), slab),
        out_shape=jax.ShapeDtypeStruct((b, n, FOUR_WIDTH), BF16),
        scratch_shapes=[
            pltpu.VMEM((FFT_N2 * Y1_PITCH, FOUR_GC), F32),
            pltpu.VMEM((2, FFT_N2 * Z_PITCH, FOUR_GC), F32),
        ],
        compiler_params=_params("parallel", "parallel"),
        name="fourier_mix",
    )(f, l1, l2, cs, w_four)


def _band_start(g, rows):
    return jnp.clip(NA_Q_ROWS * g - (NA_BAND_ROWS - NA_Q_ROWS) // 2, 0, rows - NA_BAND_ROWS)


def _patch_col_start(cb):
    return int(np.clip(NA_Q_COLS * cb - (NA_K_COLS - NA_Q_COLS) // 2, 0, GRID_W - NA_K_COLS))


def _natten_bias_tables(rpb, rows):
    heads = rpb.shape[0]
    groups = rows // NA_Q_ROWS
    n_dr, n_dc = 2 * NA_WIN_ROWS - 1, 2 * NA_WIN_COLS - 1
    n_cb = GRID_W // NA_Q_COLS
    nq, nk = NA_Q_ROWS * NA_Q_COLS, NA_BAND_ROWS * NA_K_COLS
    lane_tiles = 128 // NA_K_COLS
    lead = NA_WIN_ROWS
    strip_tiles = 2 * NA_BAND_ROWS
    col = np.arange(GRID_W)
    col_start = np.clip(col - NA_WIN_COLS // 2, 0, GRID_W - NA_WIN_COLS)
    col_ok = (col[None, :] >= col_start[:, None]) & (col[None, :] < col_start[:, None] + NA_WIN_COLS)
    dc = np.clip(col[None, :] - col[:, None] + (NA_WIN_COLS - 1), 0, n_dc - 1)
    onehot_dc = (dc[None] == np.arange(n_dc)[:, None, None]).astype(np.float32)
    tiles = jnp.einsum('hre,eqk->hrqk', rpb.astype(F32), jnp.asarray(onehot_dc), precision=lax.Precision.HIGHEST)
    tiles = jnp.where(jnp.asarray(col_ok), tiles * LOG2_E, MASK_VALUE)
    tail = strip_tiles + lane_tiles - 1 - lead - n_dr
    tiles = jnp.pad(tiles, ((0, 0), (lead, tail), (0, 0), (0, 0)), constant_values=MASK_VALUE)
    strips = []
    for cb in range(n_cb):
        q0, k0 = NA_Q_COLS * cb, _patch_col_start(cb)
        assert col_start[q0] >= k0 and col_start[q0 + NA_Q_COLS - 1] + NA_WIN_COLS <= k0 + NA_K_COLS
        sub = tiles[:, :, q0:q0 + NA_Q_COLS, k0:k0 + NA_K_COLS].transpose(0, 2, 1, 3)
        strips.append(jnp.stack([sub[:, :, r:r + strip_tiles].reshape(heads, NA_Q_COLS, strip_tiles * NA_K_COLS)
                                 for r in range(lane_tiles)], axis=1))
    strips = jnp.stack(strips, axis=1)

    plan = []
    for kind, g in enumerate((0, groups // 2, groups - 1)):
        band0 = int(np.clip(NA_Q_ROWS * g - (NA_BAND_ROWS - NA_Q_ROWS) // 2, 0, rows - NA_BAND_ROWS))
        for a in range(NA_Q_ROWS):
            qr = NA_Q_ROWS * g + a
            row_start = int(np.clip(qr - NA_WIN_ROWS // 2, 0, rows - NA_WIN_ROWS))
            assert band0 <= row_start and row_start + NA_WIN_ROWS <= band0 + NA_BAND_ROWS
            first_tile = band0 - qr + NA_WIN_ROWS - 1 + lead
            assert 0 <= first_tile and first_tile + NA_BAND_ROWS <= strip_tiles + lane_tiles - 1
            for cb in range(n_cb):
                for hd in range(2):
                    plan.append((kind * n_cb + cb, hd, cb, a, first_tile, row_start - band0))

    def build(strip_ref, o_ref):
        patch_row = lax.broadcasted_iota(jnp.int32, (1, nk), 1) // NA_K_COLS
        for idx, hd, cb, a, first_tile, first_row in plan:
            lane0 = (first_tile // lane_tiles) * 128
            val = strip_ref[hd, cb, first_tile % lane_tiles, :, lane0:lane0 + nk]
            in_window = (patch_row >= first_row) & (patch_row < first_row + NA_WIN_ROWS)
            r0 = hd * nq + a * NA_Q_COLS
            o_ref[idx, r0:r0 + NA_Q_COLS, :] = jnp.where(in_window, val, MASK_VALUE)

    return pl.pallas_call(
        build,
        grid=(heads // 2,),
        in_specs=[pl.BlockSpec((2,) + strips.shape[1:], lambda hp: (hp, 0, 0, 0, 0))],
        out_specs=pl.BlockSpec((None, 3 * n_cb, 2 * nq, nk), lambda hp: (hp, 0, 0, 0)),
        out_shape=jax.ShapeDtypeStruct((heads // 2, 3 * n_cb, 2 * nq, nk), F32),
        compiler_params=_params("parallel"),
        name="bias_table",
    )(strips)


def _natten_kernel(q_ref, k_ref, v_ref, kc_ref, vc_ref, tab_ref, o_ref, *, rows):
    nq = NA_Q_ROWS * NA_Q_COLS
    groups = rows // NA_Q_ROWS
    n_cb = GRID_W // NA_Q_COLS
    first_head = lax.broadcasted_iota(jnp.int32, (1, 2 * NA_HEAD_DIM), 1) < NA_HEAD_DIM
    kc, vc = kc_ref[...].astype(BF16), vc_ref[...].astype(BF16)
    chains = [(g, cb) for g in range(NA_GROUPS_PER_STEP) for cb in range(n_cb)]

    def group_index(g):
        return pl.program_id(2) * NA_GROUPS_PER_STEP + g

    def q_rows(g, cb, a):
        start = (g * NA_Q_ROWS + a) * GRID_W + cb * NA_Q_COLS
        return slice(start, start + NA_Q_COLS)

    def patch(ref, g, cb):
        tok0 = _band_start(group_index(g), rows) * GRID_W + _patch_col_start(cb)
        parts = [ref[pl.ds(pl.multiple_of(tok0 + bb * GRID_W, SUBLANES), NA_K_COLS), :] for bb in range(NA_BAND_ROWS)]
        return jnp.concatenate(parts, axis=0).astype(BF16)

    def scores(g, cb):
        gi = group_index(g)
        kind = jnp.where(gi == 0, 0, jnp.where(gi == groups - 1, 2, 1))
        q2 = jnp.concatenate([q_ref[q_rows(g, cb, a), :] for a in range(NA_Q_ROWS)], axis=0)
        zero = jnp.zeros_like(q2)
        qs = jnp.concatenate([jnp.where(first_head, q2, zero), jnp.where(first_head, zero, q2)], axis=0)
        return _dot_nt(qs, patch(k_ref, g, cb)) + tab_ref[kind * n_cb + cb], _dot_nt(qs, kc)

    def softmax(s_win, s_ctx):
        m = jnp.maximum(jnp.max(s_win, axis=-1, keepdims=True), jnp.max(s_ctx, axis=-1, keepdims=True))
        p_win = jnp.exp2(s_win - m)
        p_ctx = jnp.exp2(s_ctx - m)
        l = jnp.sum(p_win, axis=-1, keepdims=True) + jnp.sum(p_ctx, axis=-1, keepdims=True)
        return p_win.astype(BF16), p_ctx.astype(BF16), l

    def weighted_values(g, cb, p_win, p_ctx, l):
        o = (_dot(p_win, patch(v_ref, g, cb)) + _dot(p_ctx, vc)) / l
        o = jnp.where(first_head, o[:nq], o[nq:]).astype(o_ref.dtype)
        for a in range(NA_Q_ROWS):
            o_ref[q_rows(g, cb, a), :] = o[a * NA_Q_COLS:(a + 1) * NA_Q_COLS]

    s, p = {}, {}
    for step in range(len(chains) + 2):
        if step < len(chains):
            s[step] = scores(*chains[step])
        if 0 <= step - 1 < len(chains):
            p[step - 1] = softmax(*s.pop(step - 1))
        if 0 <= step - 2 < len(chains):
            weighted_values(*chains[step - 2], *p.pop(step - 2))


def _neighbourhood_attention(q, k, v, kc, vc, rpb):
    b, n, _ = q.shape
    rows = n // GRID_W
    steps = rows // (NA_Q_ROWS * NA_GROUPS_PER_STEP)
    ctx_len = kc.shape[1]
    tq = NA_GROUPS_PER_STEP * NA_Q_ROWS * GRID_W
    tables = _natten_bias_tables(rpb, rows)
    lanes = 2 * NA_HEAD_DIM
    seq = lambda bi, hp, i: (bi, 0, hp)
    return pl.pallas_call(
        functools.partial(_natten_kernel, rows=rows),
        grid=(b, NA_HEADS // 2, steps),
        in_specs=[
            pl.BlockSpec((None, tq, lanes), lambda bi, hp, i: (bi, i, hp)),
            pl.BlockSpec((None, n, lanes), seq),
            pl.BlockSpec((None, n, lanes), seq),
            pl.BlockSpec((None, ctx_len, lanes), seq),
            pl.BlockSpec((None, ctx_len, lanes), seq),
            pl.BlockSpec((None,) + tables.shape[1:], lambda bi, hp, i: (hp, 0, 0, 0)),
        ],
        out_specs=pl.BlockSpec((None, tq, lanes), lambda bi, hp, i: (bi, i, hp)),
        out_shape=jax.ShapeDtypeStruct((b, n, NA_WIDTH), BF16),
        compiler_params=_params("parallel", "parallel", "arbitrary"),
        name="neighbourhood_attention",
    )(q, k, v, kc, vc, tables)


def _mod_chunk(mod_ref, idx):
    return mod_ref[:, idx * D_MODEL:(idx + 1) * D_MODEL]


def _mlp_inputs(h, y, mod_ref, g2n_ref):
    h1 = h + _mod_chunk(mod_ref, 2) * y
    u = _norm_modulate(h1, g2n_ref[...], _mod_chunk(mod_ref, 3), _mod_chunk(mod_ref, 4)).astype(BF16)
    return h1, u


def _mlp_chunk(u, acc, c, w1_ref, w2_ref):
    lo, hi = c * MLP_CHUNK, (c + 1) * MLP_CHUNK
    a = jnp.maximum(_dot(u, w1_ref[:, lo:hi]), 0.0)
    part = _dot((a * a).astype(BF16), w2_ref[lo:hi, :])
    return part if acc is None else acc + part


def _run_chains(prologues, mod_ref, w1_ref, w2_ref, o_ref, tc):
    n = len(prologues)
    for stage in prologues[0][:-1]:
        stage()
    h1, u = prologues[0][-1]()
    for i in range(n):
        acc, nxt = None, None
        for c in range(MLP_CHUNKS):
            acc = _mlp_chunk(u, acc, c, w1_ref, w2_ref)
            if i + 1 < n:
                nxt = prologues[i + 1][c]()
        o_ref[i * tc:(i + 1) * tc, :] = h1 + _mod_chunk(mod_ref, 5) * acc
        if i + 1 < n:
            h1, u = nxt


def _even_tail_kernel(h_ref, of_ref, on_ref, mod_ref, wo_ref, g2n_ref, w1_ref, w2_ref, o_ref):
    tc = h_ref.shape[0] // TOKEN_CHAINS

    def prologue(i):
        rows = slice(i * tc, (i + 1) * tc)
        st = {}

        def s0():
            st["y"] = _dot(of_ref[rows, :], wo_ref[:FOUR_WIDTH, :])

        def s1():
            st["y"] = st["y"] + _dot(on_ref[rows, :], wo_ref[FOUR_WIDTH:, :])

        def s3():
            return _mlp_inputs(h_ref[rows, :], st["y"], mod_ref, g2n_ref)

        return [s0, s1, lambda: None, s3]

    _run_chains([prologue(i) for i in range(TOKEN_CHAINS)], mod_ref, w1_ref, w2_ref, o_ref, tc)


def _even_tail(h, o_four, o_na, mod_rows, w_out, norm2_g, w1, w2, tm):
    b, n, _ = h.shape
    tok = lambda bi, j: (bi, j, 0)
    const2 = lambda bi, j: (0, 0)
    return pl.pallas_call(
        _even_tail_kernel,
        grid=(b, n // tm),
        in_specs=[
            pl.BlockSpec((None, tm, D_MODEL), tok),
            pl.BlockSpec((None, tm, FOUR_WIDTH), tok),
            pl.BlockSpec((None, tm, NA_WIDTH), tok),
            pl.BlockSpec((None, 1, N_MOD * D_MODEL), lambda bi, j: (bi, 0, 0)),
            _resident(w_out.shape, const2),
            pl.BlockSpec((1, D_MODEL), const2),
            _resident(w1.shape, const2),
            _resident(w2.shape, const2),
        ],
        out_specs=pl.BlockSpec((None, tm, D_MODEL), tok),
        out_shape=jax.ShapeDtypeStruct(h.shape, F32),
        compiler_params=_params("parallel", "parallel"),
        name="even_tail",
    )(h, o_four, o_na, mod_rows, w_out, norm2_g, w1, w2)


HALO = 8


def _shift_rows(a, s):
    return pltpu.roll(a, s % a.shape[0], 0)


def _pool_kernel(hp_ref, h_ref, hn_ref, mod_ref, g1n_ref, wp_ref, ps_ref, g2n_ref, w1_ref, w2_ref, o_ref, *, seq_len):
    j = pl.program_id(1)
    tm = h_ref.shape[0]
    tc = tm // TOKEN_CHAINS
    gc = POOL_GC
    gain, shift, scale = g1n_ref[...], _mod_chunk(mod_ref, 0), _mod_chunk(mod_ref, 1)
    unorm = lambda x: _norm_modulate(x, gain, shift, scale)

    def prologue(i):
        r0 = i * tc
        st = {}

        def normed():
            u = unorm(h_ref[r0:r0 + tc, :])
            if i == 0:
                u_prev = jnp.where(j > 0, unorm(hp_ref[...]), 0.0)
            else:
                u_prev = unorm(h_ref[r0 - HALO:r0, :])
            if i == TOKEN_CHAINS - 1:
                u_next = jnp.where(j < pl.num_programs(1) - 1, unorm(hn_ref[...]), 0.0)
            else:
                u_next = unorm(h_ref[r0 + tc:r0 + tc + HALO, :])
            st["u"] = u
            st["ue"] = jnp.concatenate([u_prev, u, u_next], axis=0)

        def window_sums():
            ue = st["ue"]
            s2 = ue + _shift_rows(ue, 1)
            s4 = _shift_rows(s2[:, gc:], 1) + _shift_rows(s2[:, gc:], -1)
            s8 = _shift_rows(s4[:, gc:], 2) + _shift_rows(s4[:, gc:], -2)
            s16 = _shift_rows(s8[:, gc:], 4) + _shift_rows(s8[:, gc:], -4)
            st["sums"] = (s2[:, :gc], s4[:, :gc], s8[:, :gc], s16)

        def group_linear():
            t = j * tm + r0 + lax.broadcasted_iota(jnp.int32, (tc, 1), 0)
            ys = []
            for gi, w in enumerate(POOL_SIZES):
                lo = jnp.maximum(t - w // 2, 0)
                hi = jnp.minimum(t + w - w // 2, seq_len)
                inv_cnt = 1.0 / (hi - lo).astype(F32)
                pooled = st["sums"][gi][HALO:HALO + tc] * inv_cnt - st["u"][:, gi * gc:(gi + 1) * gc]
                ys.append(_dot(pooled.astype(BF16), wp_ref[gi]))
            st["y"] = jnp.concatenate(ys, axis=1) * ps_ref[...]

        def mlp_inputs():
            return _mlp_inputs(h_ref[r0:r0 + tc, :], st["y"], mod_ref, g2n_ref)

        return [normed, window_sums, group_linear, mlp_inputs]

    _run_chains([prologue(i) for i in range(TOKEN_CHAINS)], mod_ref, w1_ref, w2_ref, o_ref, tc)


def _odd_layer(h, mod_rows, norm1_g, w_pool, pool_scale, norm2_g, w1, w2, tm):
    b, n, _ = h.shape
    tok = lambda bi, j: (bi, j, 0)
    const2 = lambda bi, j: (0, 0)
    per_tile = tm // HALO
    last = n // HALO - 1
    return pl.pallas_call(
        functools.partial(_pool_kernel, seq_len=n),
        grid=(b, n // tm),
        in_specs=[
            pl.BlockSpec((None, HALO, D_MODEL), lambda bi, j: (bi, jnp.maximum(j * per_tile - 1, 0), 0)),
            pl.BlockSpec((None, tm, D_MODEL), tok),
            pl.BlockSpec((None, HALO, D_MODEL), lambda bi, j: (bi, jnp.minimum((j + 1) * per_tile, last), 0)),
            pl.BlockSpec((None, 1, N_MOD * D_MODEL), lambda bi, j: (bi, 0, 0)),
            pl.BlockSpec((1, D_MODEL), const2),
            _resident(w_pool.shape, lambda bi, j: (0, 0, 0)),
            pl.BlockSpec((1, D_MODEL), const2),
            pl.BlockSpec((1, D_MODEL), const2),
            _resident(w1.shape, const2),
            _resident(w2.shape, const2),
        ],
        out_specs=pl.BlockSpec((None, tm, D_MODEL), tok),
        out_shape=jax.ShapeDtypeStruct(h.shape, F32),
        compiler_params=_params("parallel", "parallel"),
        name="odd_layer",
    )(h, h, h, mod_rows, norm1_g, w_pool, pool_scale, norm2_g, w1, w2)


def _head_segments():
    lanes = 256
    seg = (np.arange(lanes)[:, None] // NA_HEAD_DIM) == (np.arange(lanes)[None, :] // NA_HEAD_DIM)
    return jnp.asarray(seg.astype(np.float32)).astype(BF16)


def kernel(x, c, ctx, c_ctx, w_mod, b_mod, norm1_g, norm2_g, w_in_even, w_four, q_norm_g, k_norm_g, rpb, w_out_even, w_pool, pool_scale, w_mlp1, w_mlp2):
    b = x.shape[0]
    depth = w_mod.shape[0]
    assert depth == 2, "layer 0 is the even (Fourier + attention) layer, layer 1 the odd (pooling) layer"
    ctx_row = b
    pad = (-(b + 1)) % SUBLANES
    c_rows = jnp.concatenate([c, c_ctx[None, :], jnp.zeros((pad, D_MODEL), F32)], axis=0)
    mod = _modulation(c_rows, w_mod, b_mod)
    mod = mod.reshape(depth, mod.shape[1], 1, N_MOD * D_MODEL)

    row = lambda a: a.reshape(1, -1)
    heads = lambda g: jnp.tile(g, NA_HEADS).reshape(1, NA_WIDTH)
    seg = _head_segments()

    w_in = w_in_even[0].astype(BF16)
    qg, kg = heads(q_norm_g[0]), heads(k_norm_g[0])
    f, q, k, v = _in_projection(x, mod[0], lambda bi: bi, row(norm1_g[0]), w_in, qg, kg, seg, tm=1024)
    _, _, kc, vc = _in_projection(ctx, mod[0], lambda bi: ctx_row, row(norm1_g[0]), w_in, qg, kg, seg,
                                  tm=ctx.shape[1])
    o_four = _fourier_mix(f, w_four[0])
    o_na = _neighbourhood_attention(q, k, v, kc, vc, rpb[0])
    h = _even_tail(x, o_four, o_na, mod[0], w_out_even[0].astype(BF16), row(norm2_g[0]),
                   w_mlp1[0].astype(BF16), w_mlp2[0].astype(BF16), tm=1024)

    h = _odd_layer(h, mod[1], row(norm1_g[1]), w_pool[0].astype(BF16), row(pool_scale[0]), row(norm2_g[1]),
                   w_mlp1[1].astype(BF16), w_mlp2[1].astype(BF16), tm=1024)
    return h
```

```python
import functools

import numpy as np
import jax
import jax.numpy as jnp
from jax import lax
from jax.experimental import pallas as pl
from jax.experimental.pallas import tpu as pltpu

D_MODEL = 1024
GRID_W = 64
FOUR_GROUPS = 4
FOUR_GC = 128
FOUR_WIDTH = 512
NA_HEADS = 8
NA_HEAD_DIM = 64
NA_WIDTH = 512
NA_WIN_ROWS = 8
NA_WIN_COLS = 16
POOL_SIZES = (2, 4, 8, 16)
POOL_GC = 256
MLP_HIDDEN = 4096
N_MOD = 6
RMS_EPS = 1e-6

VMEM_LIMIT_BYTES = 56 * 1024 * 1024
SUBLANES = 8

F32 = jnp.float32
BF16 = jnp.bfloat16
MASK_VALUE = -1e30
LOG2_E = 1.4426950408889634

FFT_N1 = 128
FFT_N2 = 64
Y1_PITCH = 2 * FFT_N1 + 8
Z_PITCH = FFT_N1 + 8
FFT_UNROLL = 8

NA_Q_ROWS = 8
NA_BAND_ROWS = NA_Q_ROWS + NA_WIN_ROWS
NA_Q_COLS = 16
NA_K_COLS = NA_Q_COLS + NA_WIN_COLS
NA_GROUPS_PER_STEP = 4
TOKEN_CHAINS = 4
INPROJ_CHAIN_ROWS = 256
MLP_CHUNK = 1024
MLP_CHUNKS = MLP_HIDDEN // MLP_CHUNK
POOL_STAGE_SIZES = (2, 2, 2, 2)


def _dot(a, b):
    return jnp.dot(a, b, preferred_element_type=F32)


def _dot_f32(a, b):
    return jnp.dot(a, b, preferred_element_type=F32, precision=lax.Precision.HIGHEST)


def _dot_nt(a, b):
    return lax.dot_general(a, b, (((1,), (1,)), ((), ())), preferred_element_type=F32)


def _params(*sem):
    return pltpu.CompilerParams(dimension_semantics=sem, vmem_limit_bytes=VMEM_LIMIT_BYTES)


def _resident(shape, index_map):
    return pl.BlockSpec(shape, index_map, pipeline_mode=pl.Buffered(1))


def _load_weights_once(pairs, sem):
    @pl.when((pl.program_id(0) == 0) & (pl.program_id(1) == 0))
    def _():
        copies = [pltpu.make_async_copy(src, dst, sem.at[i]) for i, (src, dst) in enumerate(pairs)]
        for cp in copies:
            cp.start()
        for cp in copies:
            cp.wait()


def _mod_kernel(c_ref, w_ref, b_ref, o_ref):
    c = c_ref[...]
    s = c * (1.0 / (1.0 + jnp.exp(-c)))
    o_ref[...] = _dot_f32(s, w_ref[...]) + b_ref[...]


def _modulation(c_rows, w_mod, b_mod):
    depth = w_mod.shape[0]
    rows = c_rows.shape[0]
    tn = D_MODEL
    return pl.pallas_call(
        _mod_kernel,
        grid=(depth, N_MOD * D_MODEL // tn),
        in_specs=[
            pl.BlockSpec((rows, D_MODEL), lambda l, j: (0, 0)),
            pl.BlockSpec((None, D_MODEL, tn), lambda l, j: (l, 0, j)),
            pl.BlockSpec((None, 1, tn), lambda l, j: (l, 0, j)),
        ],
        out_specs=pl.BlockSpec((None, rows, tn), lambda l, j: (l, 0, j)),
        out_shape=jax.ShapeDtypeStruct((depth, rows, N_MOD * D_MODEL), F32),
        compiler_params=_params("parallel", "parallel"),
        name="modulation",
    )(c_rows, w_mod, b_mod.reshape(depth, 1, N_MOD * D_MODEL))


def _norm_modulate(x, gain, shift, scale):
    ms = jnp.mean(x * x, axis=-1, keepdims=True)
    return (x * lax.rsqrt(ms + RMS_EPS)) * (gain * (1.0 + scale)) + shift


def _head_norm(a, gain, seg_ref):
    sq = (a * a).astype(BF16)
    half = seg_ref.shape[0]
    ms = jnp.concatenate([_dot(sq[:, :half], seg_ref[...]), _dot(sq[:, half:], seg_ref[...])], axis=1)
    return a * lax.rsqrt(ms * (1.0 / NA_HEAD_DIM) + RMS_EPS) * gain


def _inproj_kernel(x_ref, mod_ref, g_ref, w_ref, qg_ref, kg_ref, seg_ref, f_ref, q_ref, k_ref, v_ref):
    tm = x_ref.shape[0]
    chains = max(1, tm // INPROJ_CHAIN_ROWS)
    tc = tm // chains

    def normed(i):
        x = x_ref[i * tc:(i + 1) * tc, :]
        return _norm_modulate(x, g_ref[...], mod_ref[:, 0:D_MODEL], mod_ref[:, D_MODEL:2 * D_MODEL]).astype(BF16)

    def outputs(i, p):
        rows = slice(i * tc, (i + 1) * tc)
        f_ref[rows, :] = p[:, :FOUR_WIDTH]
        q = p[:, FOUR_WIDTH:FOUR_WIDTH + NA_WIDTH]
        k = p[:, FOUR_WIDTH + NA_WIDTH:FOUR_WIDTH + 2 * NA_WIDTH]
        q_ref[rows, :] = (_head_norm(q, qg_ref[...], seg_ref) * (NA_HEAD_DIM ** -0.5 * LOG2_E)).astype(BF16)
        k_ref[rows, :] = _head_norm(k, kg_ref[...], seg_ref)
        v_ref[rows, :] = p[:, FOUR_WIDTH + 2 * NA_WIDTH:]

    u, p = {}, {}
    for step in range(chains + 2):
        if step < chains:
            u[step] = normed(step)
        if 0 <= step - 1 < chains:
            p[step - 1] = _dot(u.pop(step - 1), w_ref[...])
        if 0 <= step - 2 < chains:
            outputs(step - 2, p.pop(step - 2))


def _in_projection(x, mod_rows, row_of_batch, norm_g, w_in, q_g, k_g, seg, tm):
    b, n, _ = x.shape
    width = w_in.shape[1]
    tok = lambda bi, j: (bi, j, 0)
    const2 = lambda bi, j: (0, 0)
    out_tok = lambda w: pl.BlockSpec((None, tm, w), tok)
    return pl.pallas_call(
        _inproj_kernel,
        grid=(b, n // tm),
        in_specs=[
            pl.BlockSpec((None, tm, D_MODEL), tok),
            pl.BlockSpec((None, 1, N_MOD * D_MODEL), lambda bi, j: (row_of_batch(bi), 0, 0)),
            pl.BlockSpec((1, D_MODEL), const2),
            _resident((D_MODEL, width), const2),
            pl.BlockSpec((1, NA_WIDTH), const2),
            pl.BlockSpec((1, NA_WIDTH), const2),
            pl.BlockSpec(seg.shape, const2),
        ],
        out_specs=[out_tok(FOUR_WIDTH), out_tok(NA_WIDTH), out_tok(NA_WIDTH), out_tok(NA_WIDTH)],
        out_shape=[
            jax.ShapeDtypeStruct((b, n, FOUR_WIDTH), F32),
            jax.ShapeDtypeStruct((b, n, NA_WIDTH), BF16),
            jax.ShapeDtypeStruct((b, n, NA_WIDTH), F32),
            jax.ShapeDtypeStruct((b, n, NA_WIDTH), F32),
        ],
        compiler_params=_params("parallel", "parallel"),
        name="in_projection",
    )(x, mod_rows, norm_g, w_in, q_g, k_g, seg)


def _fft_tables(n):
    n1, n2 = FFT_N1, FFT_N2
    assert n1 * n2 == n
    k1 = np.arange(n1)[None, :, None]
    pos = (n2 * np.arange(n1)[None, None, :] + np.arange(n2)[:, None, None])
    ang = 2.0 * np.pi * ((k1 * pos) % n) / n
    l1 = np.concatenate([np.cos(ang), -np.sin(ang)], axis=1)
    a2 = 2.0 * np.pi * ((np.arange(n2)[:, None] * np.arange(n2)[None, :]) % n2) / n2
    c2, s2 = np.cos(a2), np.sin(a2)
    l2 = np.block([[c2, s2], [-s2, c2]])
    ac = 2.0 * np.pi * ((np.arange(FOUR_GC)[:, None] * np.arange(FOUR_GC)[None, :]) % FOUR_GC) / FOUR_GC
    cs = np.concatenate([np.cos(ac), np.sin(ac)], axis=0)
    return l1.astype(np.float32), l2.astype(np.float32), cs.astype(np.float32)


def _fft_kernel(f_ref, l1_ref, l2_ref, cs_ref, w_ref, o_ref, y1_ref, z_ref, *, ortho):
    n1, n2 = FFT_N1, FFT_N2

    def stage1(j, carry):
        r = f_ref[pl.ds(j, n1, stride=n2), :].astype(BF16)
        y1_ref[pl.ds(pl.multiple_of(j * Y1_PITCH, SUBLANES), 2 * n1), :] = _dot(l1_ref[j], r)
        return carry

    lax.fori_loop(0, n2, stage1, 0, unroll=FFT_UNROLL)

    l2 = l2_ref[...]

    def stage2(k1, carry):
        yr = y1_ref[pl.ds(k1, n2, stride=Y1_PITCH), :]
        yi = y1_ref[pl.ds(n1 + k1, n2, stride=Y1_PITCH), :]
        z = _dot(l2, jnp.concatenate([yr, yi], axis=0).astype(BF16))
        z_ref[0, pl.ds(k1, n2, stride=Z_PITCH), :] = z[:n2]
        z_ref[1, pl.ds(k1, n2, stride=Z_PITCH), :] = z[n2:]
        return carry

    lax.fori_loop(0, n1, stage2, 0, unroll=FFT_UNROLL)

    wf = (_dot_f32(cs_ref[...], w_ref[...]) * ortho).astype(BF16)
    wr, wi = wf[:FOUR_GC], wf[FOUR_GC:]

    def stage3(k2, carry):
        base = pl.multiple_of(k2 * Z_PITCH, SUBLANES)
        zr = z_ref[0, pl.ds(base, n1), :].astype(BF16)
        zi = z_ref[1, pl.ds(base, n1), :].astype(BF16)
        o_ref[pl.ds(pl.multiple_of(k2 * n1, n1), n1), :] = (_dot(zr, wr) + _dot(zi, wi)).astype(o_ref.dtype)
        return carry

    lax.fori_loop(0, n2, stage3, 0, unroll=FFT_UNROLL)


def _fourier_mix(f, w_four):
    b, n, _ = f.shape
    l1, l2, cs = _fft_tables(n)
    l1 = jnp.asarray(l1).astype(BF16)
    l2 = jnp.asarray(l2).astype(BF16)
    cs = jnp.asarray(cs)
    ortho = float(1.0 / np.sqrt(n * FOUR_GC))
    slab = lambda bi, g: (bi, 0, g)
    return pl.pallas_call(
        functools.partial(_fft_kernel, ortho=ortho),
        grid=(b, FOUR_GROUPS),
        in_specs=[
            pl.BlockSpec((None, n, FOUR_GC), slab),
            _resident(l1.shape, lambda bi, g: (0, 0, 0)),
            _resident(l2.shape, lambda bi, g: (0, 0)),
            _resident(cs.shape, lambda bi, g: (0, 0)),
            pl.BlockSpec((None, FOUR_GC, FOUR_GC), lambda bi, g: (g, 0, 0)),
        ],
        out_specs=pl.BlockSpec((None, n, FOUR_GC), slab),
        out_shape=jax.ShapeDtypeStruct((b, n, FOUR_WIDTH), BF16),
        scratch_shapes=[
            pltpu.VMEM((FFT_N2 * Y1_PITCH, FOUR_GC), F32),
            pltpu.VMEM((2, FFT_N2 * Z_PITCH, FOUR_GC), F32),
        ],
        compiler_params=_params("parallel", "parallel"),
        name="fourier_mix",
    )(f, l1, l2, cs, w_four)


def _band_start(g, rows):
    return jnp.clip(NA_Q_ROWS * g - (NA_BAND_ROWS - NA_Q_ROWS) // 2, 0, rows - NA_BAND_ROWS)


def _patch_col_start(cb):
    return int(np.clip(NA_Q_COLS * cb - (NA_K_COLS - NA_Q_COLS) // 2, 0, GRID_W - NA_K_COLS))


def _natten_bias_tables(rpb, rows):
    heads = rpb.shape[0]
    groups = rows // NA_Q_ROWS
    n_dr, n_dc = 2 * NA_WIN_ROWS - 1, 2 * NA_WIN_COLS - 1
    n_cb = GRID_W // NA_Q_COLS
    nq, nk = NA_Q_ROWS * NA_Q_COLS, NA_BAND_ROWS * NA_K_COLS
    lane_tiles = 128 // NA_K_COLS
    lead = NA_WIN_ROWS
    strip_tiles = 2 * NA_BAND_ROWS
    col = np.arange(GRID_W)
    col_start = np.clip(col - NA_WIN_COLS // 2, 0, GRID_W - NA_WIN_COLS)
    col_ok = (col[None, :] >= col_start[:, None]) & (col[None, :] < col_start[:, None] + NA_WIN_COLS)
    dc = np.clip(col[None, :] - col[:, None] + (NA_WIN_COLS - 1), 0, n_dc - 1)
    onehot_dc = (dc[None] == np.arange(n_dc)[:, None, None]).astype(np.float32)
    tiles = jnp.einsum('hre,eqk->hrqk', rpb.astype(F32), jnp.asarray(onehot_dc), precision=lax.Precision.HIGHEST)
    tiles = jnp.where(jnp.asarray(col_ok), tiles * LOG2_E, MASK_VALUE)
    tail = strip_tiles + lane_tiles - 1 - lead - n_dr
    tiles = jnp.pad(tiles, ((0, 0), (lead, tail), (0, 0), (0, 0)), constant_values=MASK_VALUE)
    strips = []
    for cb in range(n_cb):
        q0, k0 = NA_Q_COLS * cb, _patch_col_start(cb)
        assert col_start[q0] >= k0 and col_start[q0 + NA_Q_COLS - 1] + NA_WIN_COLS <= k0 + NA_K_COLS
        sub = tiles[:, :, q0:q0 + NA_Q_COLS, k0:k0 + NA_K_COLS].transpose(0, 2, 1, 3)
        strips.append(jnp.stack([sub[:, :, r:r + strip_tiles].reshape(heads, NA_Q_COLS, strip_tiles * NA_K_COLS)
                                 for r in range(lane_tiles)], axis=1))
    strips = jnp.stack(strips, axis=1)

    plan = []
    for kind, g in enumerate((0, groups // 2, groups - 1)):
        band0 = int(np.clip(NA_Q_ROWS * g - (NA_BAND_ROWS - NA_Q_ROWS) // 2, 0, rows - NA_BAND_ROWS))
        for a in range(NA_Q_ROWS):
            qr = NA_Q_ROWS * g + a
            row_start = int(np.clip(qr - NA_WIN_ROWS // 2, 0, rows - NA_WIN_ROWS))
            assert band0 <= row_start and row_start + NA_WIN_ROWS <= band0 + NA_BAND_ROWS
            first_tile = band0 - qr + NA_WIN_ROWS - 1 + lead
            assert 0 <= first_tile and first_tile + NA_BAND_ROWS <= strip_tiles + lane_tiles - 1
            for cb in range(n_cb):
                for hd in range(2):
                    plan.append((kind * n_cb + cb, hd, cb, a, first_tile, row_start - band0))

    def build(strip_ref, o_ref):
        patch_row = lax.broadcasted_iota(jnp.int32, (1, nk), 1) // NA_K_COLS
        for idx, hd, cb, a, first_tile, first_row in plan:
            lane0 = (first_tile // lane_tiles) * 128
            val = strip_ref[hd, cb, first_tile % lane_tiles, :, lane0:lane0 + nk]
            in_window = (patch_row >= first_row) & (patch_row < first_row + NA_WIN_ROWS)
            r0 = hd * nq + a * NA_Q_COLS
            o_ref[idx, r0:r0 + NA_Q_COLS, :] = jnp.where(in_window, val, MASK_VALUE)

    return pl.pallas_call(
        build,
        grid=(heads // 2,),
        in_specs=[pl.BlockSpec((2,) + strips.shape[1:], lambda hp: (hp, 0, 0, 0, 0))],
        out_specs=pl.BlockSpec((None, 3 * n_cb, 2 * nq, nk), lambda hp: (hp, 0, 0, 0)),
        out_shape=jax.ShapeDtypeStruct((heads // 2, 3 * n_cb, 2 * nq, nk), F32),
        compiler_params=_params("parallel"),
        name="bias_table",
    )(strips)


def _natten_kernel(q_ref, k_ref, v_ref, kc_ref, vc_ref, tab_ref, o_ref, *, rows):
    nq = NA_Q_ROWS * NA_Q_COLS
    groups = rows // NA_Q_ROWS
    n_cb = GRID_W // NA_Q_COLS
    first_head = lax.broadcasted_iota(jnp.int32, (1, 2 * NA_HEAD_DIM), 1) < NA_HEAD_DIM
    kc, vc = kc_ref[...].astype(BF16), vc_ref[...].astype(BF16)
    chains = [(g, cb) for g in range(NA_GROUPS_PER_STEP) for cb in range(n_cb)]

    def group_index(g):
        return pl.program_id(2) * NA_GROUPS_PER_STEP + g

    def q_rows(g, cb, a):
        start = (g * NA_Q_ROWS + a) * GRID_W + cb * NA_Q_COLS
        return slice(start, start + NA_Q_COLS)

    def patch(ref, g, cb):
        tok0 = _band_start(group_index(g), rows) * GRID_W + _patch_col_start(cb)
        parts = [ref[pl.ds(pl.multiple_of(tok0 + bb * GRID_W, SUBLANES), NA_K_COLS), :] for bb in range(NA_BAND_ROWS)]
        return jnp.concatenate(parts, axis=0).astype(BF16)

    def scores(g, cb):
        gi = group_index(g)
        kind = jnp.where(gi == 0, 0, jnp.where(gi == groups - 1, 2, 1))
        q2 = jnp.concatenate([q_ref[q_rows(g, cb, a), :] for a in range(NA_Q_ROWS)], axis=0)
        zero = jnp.zeros_like(q2)
        qs = jnp.concatenate([jnp.where(first_head, q2, zero), jnp.where(first_head, zero, q2)], axis=0)
        return _dot_nt(qs, patch(k_ref, g, cb)) + tab_ref[kind * n_cb + cb], _dot_nt(qs, kc)

    def softmax(s_win, s_ctx):
        m = jnp.maximum(jnp.max(s_win, axis=-1, keepdims=True), jnp.max(s_ctx, axis=-1, keepdims=True))
        p_win = jnp.exp2(s_win - m)
        p_ctx = jnp.exp2(s_ctx - m)
        l = jnp.sum(p_win, axis=-1, keepdims=True) + jnp.sum(p_ctx, axis=-1, keepdims=True)
        return p_win.astype(BF16), p_ctx.astype(BF16), l

    def weighted_values(g, cb, p_win, p_ctx, l):
        o = (_dot(p_win, patch(v_ref, g, cb)) + _dot(p_ctx, vc)) / l
        o = jnp.where(first_head, o[:nq], o[nq:]).astype(o_ref.dtype)
        for a in range(NA_Q_ROWS):
            o_ref[q_rows(g, cb, a), :] = o[a * NA_Q_COLS:(a + 1) * NA_Q_COLS]

    s, p = {}, {}
    for step in range(len(chains) + 2):
        if step < len(chains):
            s[step] = scores(*chains[step])
        if 0 <= step - 1 < len(chains):
            p[step - 1] = softmax(*s.pop(step - 1))
        if 0 <= step - 2 < len(chains):
            weighted_values(*chains[step - 2], *p.pop(step - 2))


def _neighbourhood_attention(q, k, v, kc, vc, rpb):
    b, n, _ = q.shape
    rows = n // GRID_W
    steps = rows // (NA_Q_ROWS * NA_GROUPS_PER_STEP)
    ctx_len = kc.shape[1]
    tq = NA_GROUPS_PER_STEP * NA_Q_ROWS * GRID_W
    tables = _natten_bias_tables(rpb, rows)
    lanes = 2 * NA_HEAD_DIM
    seq = lambda bi, hp, i: (bi, 0, hp)
    return pl.pallas_call(
        functools.partial(_natten_kernel, rows=rows),
        grid=(b, NA_HEADS // 2, steps),
        in_specs=[
            pl.BlockSpec((None, tq, lanes), lambda bi, hp, i: (bi, i, hp)),
            pl.BlockSpec((None, n, lanes), seq),
            pl.BlockSpec((None, n, lanes), seq),
            pl.BlockSpec((None, ctx_len, lanes), seq),
            pl.BlockSpec((None, ctx_len, lanes), seq),
            pl.BlockSpec((None,) + tables.shape[1:], lambda bi, hp, i: (hp, 0, 0, 0)),
        ],
        out_specs=pl.BlockSpec((None, tq, lanes), lambda bi, hp, i: (bi, i, hp)),
        out_shape=jax.ShapeDtypeStruct((b, n, NA_WIDTH), BF16),
        compiler_params=_params("parallel", "parallel", "arbitrary"),
        name="neighbourhood_attention",
    )(q, k, v, kc, vc, tables)


def _mod_chunk(mod_ref, idx):
    return mod_ref[:, idx * D_MODEL:(idx + 1) * D_MODEL]


def _mlp_inputs(h, y, mod_ref, g2n_ref):
    h1 = h + _mod_chunk(mod_ref, 2) * y
    u = _norm_modulate(h1, g2n_ref[...], _mod_chunk(mod_ref, 3), _mod_chunk(mod_ref, 4)).astype(BF16)
    return h1, u


def _mlp_chunk(u, acc, c, w1_ref, w2_ref):
    lo, hi = c * MLP_CHUNK, (c + 1) * MLP_CHUNK
    a = jnp.maximum(_dot(u, w1_ref[:, lo:hi]), 0.0)
    part = _dot((a * a).astype(BF16), w2_ref[lo:hi, :])
    return part if acc is None else acc + part


def _pad_stages(stages, sizes=None):
    if sizes is None:
        sizes = [1] * (len(stages) - 1) + [0] * (MLP_CHUNKS - len(stages)) + [1]
    assert len(sizes) == MLP_CHUNKS and sum(sizes) == len(stages)

    def merged(group):
        def run():
            out = None
            for stage in group:
                out = stage()
            return out
        return run

    starts = [sum(sizes[:k]) for k in range(MLP_CHUNKS)]
    return [merged(stages[s0:s0 + n]) for s0, n in zip(starts, sizes)]


def _run_chains(prologues, mod_ref, w1_ref, w2_ref, o_ref, tc):
    n = len(prologues)
    for stage in prologues[0][:-1]:
        stage()
    h1, u = prologues[0][-1]()
    for i in range(n):
        acc, nxt = None, None
        for c in range(MLP_CHUNKS):
            acc = _mlp_chunk(u, acc, c, w1_ref, w2_ref)
            if i + 1 < n:
                nxt = prologues[i + 1][c]()
        o_ref[i * tc:(i + 1) * tc, :] = h1 + _mod_chunk(mod_ref, 5) * acc
        if i + 1 < n:
            h1, u = nxt


def _even_tail_kernel(h_ref, of_ref, on_ref, mod_ref, g2n_ref, wo_hbm, w1_hbm, w2_hbm, o_ref,
                      wo_ref, w1_ref, w2_ref, sem):
    _load_weights_once([(wo_hbm, wo_ref), (w1_hbm, w1_ref), (w2_hbm, w2_ref)], sem)
    tc = h_ref.shape[0] // TOKEN_CHAINS

    def prologue(i):
        rows = slice(i * tc, (i + 1) * tc)
        st = {}

        def s0():
            st["y"] = _dot(of_ref[rows, :], wo_ref[:FOUR_WIDTH, :])

        def s1():
            st["y"] = st["y"] + _dot(on_ref[rows, :], wo_ref[FOUR_WIDTH:, :])

        def s3():
            return _mlp_inputs(h_ref[rows, :], st["y"], mod_ref, g2n_ref)

        return _pad_stages([s0, s1, s3])

    _run_chains([prologue(i) for i in range(TOKEN_CHAINS)], mod_ref, w1_ref, w2_ref, o_ref, tc)


def _even_tail(h, o_four, o_na, mod_rows, w_out, norm2_g, w1, w2, tm):
    b, n, _ = h.shape
    tok = lambda bi, j: (bi, j, 0)
    const2 = lambda bi, j: (0, 0)
    return pl.pallas_call(
        _even_tail_kernel,
        grid=(b, n // tm),
        in_specs=[
            pl.BlockSpec((None, tm, D_MODEL), tok),
            pl.BlockSpec((None, tm, FOUR_WIDTH), tok),
            pl.BlockSpec((None, tm, NA_WIDTH), tok),
            pl.BlockSpec((None, 1, N_MOD * D_MODEL), lambda bi, j: (bi, 0, 0)),
            pl.BlockSpec((1, D_MODEL), const2),
            pl.BlockSpec(memory_space=pl.ANY),
            pl.BlockSpec(memory_space=pl.ANY),
            pl.BlockSpec(memory_space=pl.ANY),
        ],
        out_specs=pl.BlockSpec((None, tm, D_MODEL), tok),
        out_shape=jax.ShapeDtypeStruct(h.shape, F32),
        scratch_shapes=[pltpu.VMEM(w_out.shape, BF16), pltpu.VMEM(w1.shape, BF16), pltpu.VMEM(w2.shape, BF16),
                        pltpu.SemaphoreType.DMA((3,))],
        compiler_params=_params("arbitrary", "arbitrary"),
        name="even_tail",
    )(h, o_four, o_na, mod_rows, norm2_g, w_out, w1, w2)


HALO = 8


def _shift_rows(a, s):
    return pltpu.roll(a, s % a.shape[0], 0)


def _pool_kernel(hp_ref, h_ref, hn_ref, mod_ref, g1n_ref, ps_ref, g2n_ref, wp_hbm, w1_hbm, w2_hbm, o_ref,
                 wp_ref, w1_ref, w2_ref, sem, *, seq_len):
    _load_weights_once([(wp_hbm, wp_ref), (w1_hbm, w1_ref), (w2_hbm, w2_ref)], sem)
    j = pl.program_id(1)
    tm = h_ref.shape[0]
    tc = tm // TOKEN_CHAINS
    gc = POOL_GC
    gain, shift, scale = g1n_ref[...], _mod_chunk(mod_ref, 0), _mod_chunk(mod_ref, 1)
    unorm = lambda x: _norm_modulate(x, gain, shift, scale)

    def prologue(i):
        r0 = i * tc
        st = {}

        def normed():
            st["u"] = unorm(h_ref[r0:r0 + tc, :])

        def with_halo():
            if i == 0:
                u_prev = jnp.where(j > 0, unorm(hp_ref[...]), 0.0)
            else:
                u_prev = unorm(h_ref[r0 - HALO:r0, :])
            if i == TOKEN_CHAINS - 1:
                u_next = jnp.where(j < pl.num_programs(1) - 1, unorm(hn_ref[...]), 0.0)
            else:
                u_next = unorm(h_ref[r0 + tc:r0 + tc + HALO, :])
            st["ue"] = jnp.concatenate([u_prev, st["u"], u_next], axis=0)

        def sum2():
            st["s2"] = st["ue"] + _shift_rows(st["ue"], 1)

        def sum4():
            s2 = st["s2"][:, gc:]
            st["s4"] = _shift_rows(s2, 1) + _shift_rows(s2, -1)

        def sum8_16():
            s4 = st["s4"][:, gc:]
            s8 = _shift_rows(s4, 2) + _shift_rows(s4, -2)
            s16 = _shift_rows(s8[:, gc:], 4) + _shift_rows(s8[:, gc:], -4)
            st["sums"] = (st["s2"][:, :gc], st["s4"][:, :gc], s8[:, :gc], s16)

        def group_linear(groups):
            def run():
                t = j * tm + r0 + lax.broadcasted_iota(jnp.int32, (tc, 1), 0)
                for gi in groups:
                    w = POOL_SIZES[gi]
                    lo = jnp.maximum(t - w // 2, 0)
                    hi = jnp.minimum(t + w - w // 2, seq_len)
                    inv_cnt = 1.0 / (hi - lo).astype(F32)
                    pooled = st["sums"][gi][HALO:HALO + tc] * inv_cnt - st["u"][:, gi * gc:(gi + 1) * gc]
                    st["y%d" % gi] = _dot(pooled.astype(BF16), wp_ref[gi]) * ps_ref[:, gi * gc:(gi + 1) * gc]
            return run

        def mlp_inputs():
            y = jnp.concatenate([st["y%d" % gi] for gi in range(len(POOL_SIZES))], axis=1)
            return _mlp_inputs(h_ref[r0:r0 + tc, :], y, mod_ref, g2n_ref)

        return _pad_stages([normed, with_halo, sum2, sum4, sum8_16, group_linear((0, 1)), group_linear((2, 3)),
                            mlp_inputs], sizes=POOL_STAGE_SIZES)

    _run_chains([prologue(i) for i in range(TOKEN_CHAINS)], mod_ref, w1_ref, w2_ref, o_ref, tc)


def _odd_layer(h, mod_rows, norm1_g, w_pool, pool_scale, norm2_g, w1, w2, tm):
    b, n, _ = h.shape
    tok = lambda bi, j: (bi, j, 0)
    const2 = lambda bi, j: (0, 0)
    per_tile = tm // HALO
    last = n // HALO - 1
    return pl.pallas_call(
        functools.partial(_pool_kernel, seq_len=n),
        grid=(b, n // tm),
        in_specs=[
            pl.BlockSpec((None, HALO, D_MODEL), lambda bi, j: (bi, jnp.maximum(j * per_tile - 1, 0), 0)),
            pl.BlockSpec((None, tm, D_MODEL), tok),
            pl.BlockSpec((None, HALO, D_MODEL), lambda bi, j: (bi, jnp.minimum((j + 1) * per_tile, last), 0)),
            pl.BlockSpec((None, 1, N_MOD * D_MODEL), lambda bi, j: (bi, 0, 0)),
            pl.BlockSpec((1, D_MODEL), const2),
            pl.BlockSpec((1, D_MODEL), const2),
            pl.BlockSpec((1, D_MODEL), const2),
            pl.BlockSpec(memory_space=pl.ANY),
            pl.BlockSpec(memory_space=pl.ANY),
            pl.BlockSpec(memory_space=pl.ANY),
        ],
        out_specs=pl.BlockSpec((None, tm, D_MODEL), tok),
        out_shape=jax.ShapeDtypeStruct(h.shape, F32),
        scratch_shapes=[pltpu.VMEM(w_pool.shape, BF16), pltpu.VMEM(w1.shape, BF16), pltpu.VMEM(w2.shape, BF16),
                        pltpu.SemaphoreType.DMA((3,))],
        compiler_params=_params("arbitrary", "arbitrary"),
        name="odd_layer",
    )(h, h, h, mod_rows, norm1_g, pool_scale, norm2_g, w_pool, w1, w2)


def _head_segments():
    lanes = 256
    seg = (np.arange(lanes)[:, None] // NA_HEAD_DIM) == (np.arange(lanes)[None, :] // NA_HEAD_DIM)
    return jnp.asarray(seg.astype(np.float32)).astype(BF16)


def kernel(x, c, ctx, c_ctx, w_mod, b_mod, norm1_g, norm2_g, w_in_even, w_four, q_norm_g, k_norm_g, rpb, w_out_even, w_pool, pool_scale, w_mlp1, w_mlp2):
    b = x.shape[0]
    depth = w_mod.shape[0]
    assert depth == 2, "layer 0 is the even (Fourier + attention) layer, layer 1 the odd (pooling) layer"
    ctx_row = b
    pad = (-(b + 1)) % SUBLANES
    c_rows = jnp.concatenate([c, c_ctx[None, :], jnp.zeros((pad, D_MODEL), F32)], axis=0)
    mod = _modulation(c_rows, w_mod, b_mod)
    mod = mod.reshape(depth, mod.shape[1], 1, N_MOD * D_MODEL)

    row = lambda a: a.reshape(1, -1)
    heads = lambda g: jnp.tile(g, NA_HEADS).reshape(1, NA_WIDTH)
    seg = _head_segments()

    w_in = w_in_even[0].astype(BF16)
    qg, kg = heads(q_norm_g[0]), heads(k_norm_g[0])
    f, q, k, v = _in_projection(x, mod[0], lambda bi: bi, row(norm1_g[0]), w_in, qg, kg, seg, tm=1024)
    _, _, kc, vc = _in_projection(ctx, mod[0], lambda bi: ctx_row, row(norm1_g[0]), w_in, qg, kg, seg,
                                  tm=ctx.shape[1])
    o_four = _fourier_mix(f, w_four[0])
    o_na = _neighbourhood_attention(q, k, v, kc, vc, rpb[0])
    h = _even_tail(x, o_four, o_na, mod[0], w_out_even[0].astype(BF16), row(norm2_g[0]),
                   w_mlp1[0].astype(BF16), w_mlp2[0].astype(BF16), tm=1024)

    h = _odd_layer(h, mod[1], row(norm1_g[1]), w_pool[0].astype(BF16), row(pool_scale[0]), row(norm2_g[1]),
                   w_mlp1[1].astype(BF16), w_mlp2[1].astype(BF16), tm=1024)
    return h
```

```python
import functools

import numpy as np
import jax
import jax.numpy as jnp
from jax import lax
from jax.experimental import pallas as pl
from jax.experimental.pallas import tpu as pltpu

D_MODEL = 1024
GRID_W = 64
FOUR_GROUPS = 4
FOUR_GC = 128
FOUR_WIDTH = 512
NA_HEADS = 8
NA_HEAD_DIM = 64
NA_WIDTH = 512
NA_WIN_ROWS = 8
NA_WIN_COLS = 16
POOL_SIZES = (2, 4, 8, 16)
POOL_GC = 256
MLP_HIDDEN = 4096
N_MOD = 6
RMS_EPS = 1e-6

VMEM_LIMIT_BYTES = 56 * 1024 * 1024
SUBLANES = 8
WEIGHT_STAGE_BYTES = 1024 * 1024

F32 = jnp.float32
BF16 = jnp.bfloat16
MASK_VALUE = -1e30
LOG2_E = 1.4426950408889634

FFT_N1 = 128
FFT_N2 = 64
Y1_PITCH = 2 * FFT_N1 + 8
Z_PITCH = FFT_N1 + 8
FFT_UNROLL = 32

NA_Q_ROWS = 8
NA_BAND_ROWS = NA_Q_ROWS + NA_WIN_ROWS
NA_Q_COLS = 16
NA_K_COLS = NA_Q_COLS + NA_WIN_COLS
NA_GROUPS_PER_STEP = 4
TOKEN_CHAINS = 4
INPROJ_CHAIN_ROWS = 256
MLP_CHUNK = 1024
MLP_CHUNKS = MLP_HIDDEN // MLP_CHUNK
POOL_STAGE_SIZES = (2, 2, 2, 2)


def _dot(a, b):
    return jnp.dot(a, b, preferred_element_type=F32)


def _dot_f32(a, b):
    return jnp.dot(a, b, preferred_element_type=F32, precision=lax.Precision.HIGHEST)


def _dot_nt(a, b):
    return lax.dot_general(a, b, (((1,), (1,)), ((), ())), preferred_element_type=F32)


def _params(*sem):
    return pltpu.CompilerParams(dimension_semantics=sem, vmem_limit_bytes=VMEM_LIMIT_BYTES)


def _resident(shape, index_map):
    return pl.BlockSpec(shape, index_map, pipeline_mode=pl.Buffered(1))


def _weight_scratch(shapes):
    scratch = [pltpu.VMEM(shape, BF16) for shape in shapes]
    for k, n in shapes:
        rows = min(k, max(2 * SUBLANES, WEIGHT_STAGE_BYTES // (4 * n)))
        assert k % rows == 0
        scratch.append(pltpu.VMEM((2, rows, n), F32))
    return scratch + [pltpu.SemaphoreType.DMA((2 * len(shapes),))]


def _load_weights_once(sources, weights, stages, sem):
    @pl.when((pl.program_id(0) == 0) & (pl.program_id(1) == 0))
    def _():
        for i, (src, dst, stage) in enumerate(zip(sources, weights, stages)):
            rows = stage.shape[1]
            chunks = dst.shape[0] // rows

            def copy(c):
                return pltpu.make_async_copy(src.at[pl.ds(c * rows, rows), :], stage.at[c % 2], sem.at[2 * i + c % 2])

            copy(0).start()
            for c in range(chunks):
                copy(c).wait()
                if c + 1 < chunks:
                    copy(c + 1).start()
                dst[c * rows:(c + 1) * rows, :] = stage[c % 2].astype(BF16)


def _mod_kernel(c_ref, w_ref, b_ref, o_ref):
    c = c_ref[...]
    s = c * (1.0 / (1.0 + jnp.exp(-c)))
    s_hi = s.astype(BF16)
    s_lo = (s - s_hi.astype(F32)).astype(BF16)
    w = w_ref[...].astype(BF16)
    o_ref[...] = _dot(s_hi, w) + _dot(s_lo, w) + b_ref[...]


def _modulation(c_rows, w_mod, b_mod):
    depth = w_mod.shape[0]
    rows = c_rows.shape[0]
    tn = 2 * D_MODEL
    return pl.pallas_call(
        _mod_kernel,
        grid=(depth, N_MOD * D_MODEL // tn),
        in_specs=[
            pl.BlockSpec((rows, D_MODEL), lambda l, j: (0, 0)),
            pl.BlockSpec((None, D_MODEL, tn), lambda l, j: (l, 0, j)),
            pl.BlockSpec((None, 1, tn), lambda l, j: (l, 0, j)),
        ],
        out_specs=pl.BlockSpec((None, rows, tn), lambda l, j: (l, 0, j)),
        out_shape=jax.ShapeDtypeStruct((depth, rows, N_MOD * D_MODEL), F32),
        compiler_params=_params("parallel", "parallel"),
        name="modulation",
    )(c_rows, w_mod, b_mod.reshape(depth, 1, N_MOD * D_MODEL))


def _norm_modulate(x, gain, shift, scale):
    ms = jnp.mean(x * x, axis=-1, keepdims=True)
    return (x * lax.rsqrt(ms + RMS_EPS)) * (gain * (1.0 + scale)) + shift


def _head_norm(a, gain, seg_ref):
    sq = (a * a).astype(BF16)
    half = seg_ref.shape[0]
    ms = jnp.concatenate([_dot(sq[:, :half], seg_ref[...]), _dot(sq[:, half:], seg_ref[...])], axis=1)
    return a * lax.rsqrt(ms * (1.0 / NA_HEAD_DIM) + RMS_EPS) * gain


def _inproj_kernel(x_ref, mod_ref, g_ref, w_ref, qg_ref, kg_ref, seg_ref, f_ref, q_ref, k_ref, v_ref):
    tm = x_ref.shape[0]
    chains = max(1, tm // INPROJ_CHAIN_ROWS)
    tc = tm // chains

    def normed(i):
        x = x_ref[i * tc:(i + 1) * tc, :]
        return _norm_modulate(x, g_ref[...], mod_ref[:, 0:D_MODEL], mod_ref[:, D_MODEL:2 * D_MODEL]).astype(BF16)

    def outputs(i, p):
        rows = slice(i * tc, (i + 1) * tc)
        f_ref[rows, :] = p[:, :FOUR_WIDTH]
        q = p[:, FOUR_WIDTH:FOUR_WIDTH + NA_WIDTH]
        k = p[:, FOUR_WIDTH + NA_WIDTH:FOUR_WIDTH + 2 * NA_WIDTH]
        q_ref[rows, :] = (_head_norm(q, qg_ref[...], seg_ref) * (NA_HEAD_DIM ** -0.5 * LOG2_E)).astype(BF16)
        k_ref[rows, :] = _head_norm(k, kg_ref[...], seg_ref)
        v_ref[rows, :] = p[:, FOUR_WIDTH + 2 * NA_WIDTH:]

    u, p = {}, {}
    for step in range(chains + 2):
        if step < chains:
            u[step] = normed(step)
        if 0 <= step - 1 < chains:
            p[step - 1] = _dot(u.pop(step - 1), w_ref[...])
        if 0 <= step - 2 < chains:
            outputs(step - 2, p.pop(step - 2))


def _in_projection(x, mod_rows, row_of_batch, norm_g, w_in, q_g, k_g, seg, tm):
    b, n, _ = x.shape
    width = w_in.shape[1]
    tok = lambda bi, j: (bi, j, 0)
    const2 = lambda bi, j: (0, 0)
    out_tok = lambda w: pl.BlockSpec((None, tm, w), tok)
    return pl.pallas_call(
        _inproj_kernel,
        grid=(b, n // tm),
        in_specs=[
            pl.BlockSpec((None, tm, D_MODEL), tok),
            pl.BlockSpec((None, 1, N_MOD * D_MODEL), lambda bi, j: (row_of_batch(bi), 0, 0)),
            pl.BlockSpec((1, D_MODEL), const2),
            _resident((D_MODEL, width), const2),
            pl.BlockSpec((1, NA_WIDTH), const2),
            pl.BlockSpec((1, NA_WIDTH), const2),
            pl.BlockSpec(seg.shape, const2),
        ],
        out_specs=[out_tok(FOUR_WIDTH), out_tok(NA_WIDTH), out_tok(NA_WIDTH), out_tok(NA_WIDTH)],
        out_shape=[
            jax.ShapeDtypeStruct((b, n, FOUR_WIDTH), F32),
            jax.ShapeDtypeStruct((b, n, NA_WIDTH), BF16),
            jax.ShapeDtypeStruct((b, n, NA_WIDTH), F32),
            jax.ShapeDtypeStruct((b, n, NA_WIDTH), F32),
        ],
        compiler_params=_params("parallel", "parallel"),
        name="in_projection",
    )(x, mod_rows, norm_g, w_in, q_g, k_g, seg)


def _fft_tables(n):
    n1, n2 = FFT_N1, FFT_N2
    assert n1 * n2 == n
    k1 = np.arange(n1)[None, :, None]
    pos = (n2 * np.arange(n1)[None, None, :] + np.arange(n2)[:, None, None])
    ang = 2.0 * np.pi * ((k1 * pos) % n) / n
    l1 = np.concatenate([np.cos(ang), -np.sin(ang)], axis=1)
    a2 = 2.0 * np.pi * ((np.arange(n2)[:, None] * np.arange(n2)[None, :]) % n2) / n2
    c2, s2 = np.cos(a2), np.sin(a2)
    l2 = np.block([[c2, s2], [-s2, c2]])
    ac = 2.0 * np.pi * ((np.arange(FOUR_GC)[:, None] * np.arange(FOUR_GC)[None, :]) % FOUR_GC) / FOUR_GC
    cs = np.concatenate([np.cos(ac), np.sin(ac)], axis=0)
    return l1.astype(np.float32), l2.astype(np.float32), cs.astype(np.float32)


def _fft_kernel(f_ref, l1_ref, l2_ref, cs_ref, w_ref, o_ref, y1_ref, z_ref, *, ortho):
    n1, n2 = FFT_N1, FFT_N2

    def stage1(j, carry):
        r = f_ref[pl.ds(j, n1, stride=n2), :].astype(BF16)
        y1_ref[pl.ds(pl.multiple_of(j * Y1_PITCH, SUBLANES), 2 * n1), :] = _dot(l1_ref[j], r)
        return carry

    lax.fori_loop(0, n2, stage1, 0, unroll=FFT_UNROLL)

    l2 = l2_ref[...]

    def stage2(k1, carry):
        yr = y1_ref[pl.ds(k1, n2, stride=Y1_PITCH), :]
        yi = y1_ref[pl.ds(n1 + k1, n2, stride=Y1_PITCH), :]
        z = _dot(l2, jnp.concatenate([yr, yi], axis=0).astype(BF16))
        z_ref[0, pl.ds(k1, n2, stride=Z_PITCH), :] = z[:n2]
        z_ref[1, pl.ds(k1, n2, stride=Z_PITCH), :] = z[n2:]
        return carry

    lax.fori_loop(0, n1, stage2, 0, unroll=FFT_UNROLL)

    wf = (_dot_f32(cs_ref[...], w_ref[...]) * ortho).astype(BF16)
    wr, wi = wf[:FOUR_GC], wf[FOUR_GC:]

    def stage3(k2, carry):
        base = pl.multiple_of(k2 * Z_PITCH, SUBLANES)
        zr = z_ref[0, pl.ds(base, n1), :].astype(BF16)
        zi = z_ref[1, pl.ds(base, n1), :].astype(BF16)
        o_ref[pl.ds(pl.multiple_of(k2 * n1, n1), n1), :] = (_dot(zr, wr) + _dot(zi, wi)).astype(o_ref.dtype)
        return carry

    lax.fori_loop(0, n2, stage3, 0, unroll=FFT_UNROLL)


def _fourier_mix(f, w_four):
    b, n, _ = f.shape
    l1, l2, cs = _fft_tables(n)
    l1 = jnp.asarray(l1).astype(BF16)
    l2 = jnp.asarray(l2).astype(BF16)
    cs = jnp.asarray(cs)
    ortho = float(1.0 / np.sqrt(n * FOUR_GC))
    slab = lambda bi, g: (bi, 0, g)
    return pl.pallas_call(
        functools.partial(_fft_kernel, ortho=ortho),
        grid=(b, FOUR_GROUPS),
        in_specs=[
            pl.BlockSpec((None, n, FOUR_GC), slab),
            _resident(l1.shape, lambda bi, g: (0, 0, 0)),
            _resident(l2.shape, lambda bi, g: (0, 0)),
            _resident(cs.shape, lambda bi, g: (0, 0)),
            pl.BlockSpec((None, FOUR_GC, FOUR_GC), lambda bi, g: (g, 0, 0)),
        ],
        out_specs=pl.BlockSpec((None, n, FOUR_GC), slab),
        out_shape=jax.ShapeDtypeStruct((b, n, FOUR_WIDTH), BF16),
        scratch_shapes=[
            pltpu.VMEM((FFT_N2 * Y1_PITCH, FOUR_GC), F32),
            pltpu.VMEM((2, FFT_N2 * Z_PITCH, FOUR_GC), F32),
        ],
        compiler_params=_params("parallel", "parallel"),
        name="fourier_mix",
    )(f, l1, l2, cs, w_four)


def _band_start(g, rows):
    return jnp.clip(NA_Q_ROWS * g - (NA_BAND_ROWS - NA_Q_ROWS) // 2, 0, rows - NA_BAND_ROWS)


def _patch_col_start(cb):
    return int(np.clip(NA_Q_COLS * cb - (NA_K_COLS - NA_Q_COLS) // 2, 0, GRID_W - NA_K_COLS))


def _natten_bias_tables(rpb, rows):
    heads = rpb.shape[0]
    groups = rows // NA_Q_ROWS
    n_dr, n_dc = 2 * NA_WIN_ROWS - 1, 2 * NA_WIN_COLS - 1
    n_cb = GRID_W // NA_Q_COLS
    nq, nk = NA_Q_ROWS * NA_Q_COLS, NA_BAND_ROWS * NA_K_COLS
    lane_tiles = 128 // NA_K_COLS
    lead = NA_WIN_ROWS
    strip_tiles = 2 * NA_BAND_ROWS
    col = np.arange(GRID_W)
    col_start = np.clip(col - NA_WIN_COLS // 2, 0, GRID_W - NA_WIN_COLS)
    col_ok = (col[None, :] >= col_start[:, None]) & (col[None, :] < col_start[:, None] + NA_WIN_COLS)
    dc = np.clip(col[None, :] - col[:, None] + (NA_WIN_COLS - 1), 0, n_dc - 1)
    onehot_dc = (dc[None] == np.arange(n_dc)[:, None, None]).astype(np.float32)
    tiles = jnp.einsum('hre,eqk->hrqk', rpb.astype(F32), jnp.asarray(onehot_dc), precision=lax.Precision.HIGHEST)
    tiles = jnp.where(jnp.asarray(col_ok), tiles * LOG2_E, MASK_VALUE)
    tail = strip_tiles + lane_tiles - 1 - lead - n_dr
    tiles = jnp.pad(tiles, ((0, 0), (lead, tail), (0, 0), (0, 0)), constant_values=MASK_VALUE)
    strips = []
    for cb in range(n_cb):
        q0, k0 = NA_Q_COLS * cb, _patch_col_start(cb)
        assert col_start[q0] >= k0 and col_start[q0 + NA_Q_COLS - 1] + NA_WIN_COLS <= k0 + NA_K_COLS
        sub = tiles[:, :, q0:q0 + NA_Q_COLS, k0:k0 + NA_K_COLS].transpose(0, 2, 1, 3)
        strips.append(jnp.stack([sub[:, :, r:r + strip_tiles].reshape(heads, NA_Q_COLS, strip_tiles * NA_K_COLS)
                                 for r in range(lane_tiles)], axis=1))
    strips = jnp.stack(strips, axis=1)

    plan = []
    for kind, g in enumerate((0, groups // 2, groups - 1)):
        band0 = int(np.clip(NA_Q_ROWS * g - (NA_BAND_ROWS - NA_Q_ROWS) // 2, 0, rows - NA_BAND_ROWS))
        for a in range(NA_Q_ROWS):
            qr = NA_Q_ROWS * g + a
            row_start = int(np.clip(qr - NA_WIN_ROWS // 2, 0, rows - NA_WIN_ROWS))
            assert band0 <= row_start and row_start + NA_WIN_ROWS <= band0 + NA_BAND_ROWS
            first_tile = band0 - qr + NA_WIN_ROWS - 1 + lead
            assert 0 <= first_tile and first_tile + NA_BAND_ROWS <= strip_tiles + lane_tiles - 1
            for cb in range(n_cb):
                for hd in range(2):
                    plan.append((kind * n_cb + cb, hd, cb, a, first_tile, row_start - band0))

    def build(strip_ref, o_ref):
        patch_row = lax.broadcasted_iota(jnp.int32, (1, nk), 1) // NA_K_COLS
        for idx, hd, cb, a, first_tile, first_row in plan:
            lane0 = (first_tile // lane_tiles) * 128
            val = strip_ref[hd, cb, first_tile % lane_tiles, :, lane0:lane0 + nk]
            in_window = (patch_row >= first_row) & (patch_row < first_row + NA_WIN_ROWS)
            r0 = hd * nq + a * NA_Q_COLS
            o_ref[idx, r0:r0 + NA_Q_COLS, :] = jnp.where(in_window, val, MASK_VALUE)

    return pl.pallas_call(
        build,
        grid=(heads // 2,),
        in_specs=[pl.BlockSpec((2,) + strips.shape[1:], lambda hp: (hp, 0, 0, 0, 0))],
        out_specs=pl.BlockSpec((None, 3 * n_cb, 2 * nq, nk), lambda hp: (hp, 0, 0, 0)),
        out_shape=jax.ShapeDtypeStruct((heads // 2, 3 * n_cb, 2 * nq, nk), F32),
        compiler_params=_params("parallel"),
        name="bias_table",
    )(strips)


def _natten_kernel(q_ref, k_ref, v_ref, kc_ref, vc_ref, tab_ref, o_ref, *, rows):
    nq = NA_Q_ROWS * NA_Q_COLS
    groups = rows // NA_Q_ROWS
    n_cb = GRID_W // NA_Q_COLS
    first_head = lax.broadcasted_iota(jnp.int32, (1, 2 * NA_HEAD_DIM), 1) < NA_HEAD_DIM
    kc, vc = kc_ref[...].astype(BF16), vc_ref[...].astype(BF16)
    chains = [(g, cb) for g in range(NA_GROUPS_PER_STEP) for cb in range(n_cb)]

    def group_index(g):
        return pl.program_id(2) * NA_GROUPS_PER_STEP + g

    def q_rows(g, cb, a):
        start = (g * NA_Q_ROWS + a) * GRID_W + cb * NA_Q_COLS
        return slice(start, start + NA_Q_COLS)

    def patch(ref, g, cb):
        tok0 = _band_start(group_index(g), rows) * GRID_W + _patch_col_start(cb)
        parts = [ref[pl.ds(pl.multiple_of(tok0 + bb * GRID_W, SUBLANES), NA_K_COLS), :] for bb in range(NA_BAND_ROWS)]
        return jnp.concatenate(parts, axis=0).astype(BF16)

    def scores(g, cb):
        gi = group_index(g)
        kind = jnp.where(gi == 0, 0, jnp.where(gi == groups - 1, 2, 1))
        q2 = jnp.concatenate([q_ref[q_rows(g, cb, a), :] for a in range(NA_Q_ROWS)], axis=0)
        zero = jnp.zeros_like(q2)
        qs = jnp.concatenate([jnp.where(first_head, q2, zero), jnp.where(first_head, zero, q2)], axis=0)
        return _dot_nt(qs, patch(k_ref, g, cb)) + tab_ref[kind * n_cb + cb], _dot_nt(qs, kc)

    def softmax(s_win, s_ctx):
        m = jnp.maximum(jnp.max(s_win, axis=-1, keepdims=True), jnp.max(s_ctx, axis=-1, keepdims=True))
        p_win = jnp.exp2(s_win - m)
        p_ctx = jnp.exp2(s_ctx - m)
        l = jnp.sum(p_win, axis=-1, keepdims=True) + jnp.sum(p_ctx, axis=-1, keepdims=True)
        return p_win.astype(BF16), p_ctx.astype(BF16), l

    def weighted_values(g, cb, p_win, p_ctx, l):
        o = (_dot(p_win, patch(v_ref, g, cb)) + _dot(p_ctx, vc)) / l
        o = jnp.where(first_head, o[:nq], o[nq:]).astype(o_ref.dtype)
        for a in range(NA_Q_ROWS):
            o_ref[q_rows(g, cb, a), :] = o[a * NA_Q_COLS:(a + 1) * NA_Q_COLS]

    s, p = {}, {}
    for step in range(len(chains) + 2):
        if step < len(chains):
            s[step] = scores(*chains[step])
        if 0 <= step - 1 < len(chains):
            p[step - 1] = softmax(*s.pop(step - 1))
        if 0 <= step - 2 < len(chains):
            weighted_values(*chains[step - 2], *p.pop(step - 2))


def _neighbourhood_attention(q, k, v, kc, vc, rpb):
    b, n, _ = q.shape
    rows = n // GRID_W
    steps = rows // (NA_Q_ROWS * NA_GROUPS_PER_STEP)
    ctx_len = kc.shape[1]
    tq = NA_GROUPS_PER_STEP * NA_Q_ROWS * GRID_W
    tables = _natten_bias_tables(rpb, rows)
    lanes = 2 * NA_HEAD_DIM
    seq = lambda bi, hp, i: (bi, 0, hp)
    return pl.pallas_call(
        functools.partial(_natten_kernel, rows=rows),
        grid=(b, NA_HEADS // 2, steps),
        in_specs=[
            pl.BlockSpec((None, tq, lanes), lambda bi, hp, i: (bi, i, hp)),
            pl.BlockSpec((None, n, lanes), seq),
            pl.BlockSpec((None, n, lanes), seq),
            pl.BlockSpec((None, ctx_len, lanes), seq),
            pl.BlockSpec((None, ctx_len, lanes), seq),
            pl.BlockSpec((None,) + tables.shape[1:], lambda bi, hp, i: (hp, 0, 0, 0)),
        ],
        out_specs=pl.BlockSpec((None, tq, lanes), lambda bi, hp, i: (bi, i, hp)),
        out_shape=jax.ShapeDtypeStruct((b, n, NA_WIDTH), BF16),
        compiler_params=_params("parallel", "parallel", "arbitrary"),
        name="neighbourhood_attention",
    )(q, k, v, kc, vc, tables)


def _mod_chunk(mod_ref, idx):
    return mod_ref[:, idx * D_MODEL:(idx + 1) * D_MODEL]


def _mlp_inputs(h, y, mod_ref, g2n_ref):
    h1 = h + _mod_chunk(mod_ref, 2) * y
    u = _norm_modulate(h1, g2n_ref[...], _mod_chunk(mod_ref, 3), _mod_chunk(mod_ref, 4)).astype(BF16)
    return h1, u


def _mlp_chunk(u, acc, c, w1_ref, w2_ref):
    lo, hi = c * MLP_CHUNK, (c + 1) * MLP_CHUNK
    a = jnp.maximum(_dot(u, w1_ref[:, lo:hi]), 0.0)
    part = _dot((a * a).astype(BF16), w2_ref[lo:hi, :])
    return part if acc is None else acc + part


def _pad_stages(stages, sizes=None):
    if sizes is None:
        sizes = [1] * (len(stages) - 1) + [0] * (MLP_CHUNKS - len(stages)) + [1]
    assert len(sizes) == MLP_CHUNKS and sum(sizes) == len(stages)

    def merged(group):
        def run():
            out = None
            for stage in group:
                out = stage()
            return out
        return run

    starts = [sum(sizes[:k]) for k in range(MLP_CHUNKS)]
    return [merged(stages[s0:s0 + n]) for s0, n in zip(starts, sizes)]


def _run_chains(prologues, mod_ref, w1_ref, w2_ref, o_ref, tc):
    n = len(prologues)
    for stage in prologues[0][:-1]:
        stage()
    h1, u = prologues[0][-1]()
    for i in range(n):
        acc, nxt = None, None
        for c in range(MLP_CHUNKS):
            acc = _mlp_chunk(u, acc, c, w1_ref, w2_ref)
            if i + 1 < n:
                nxt = prologues[i + 1][c]()
        o_ref[i * tc:(i + 1) * tc, :] = h1 + _mod_chunk(mod_ref, 5) * acc
        if i + 1 < n:
            h1, u = nxt


def _even_tail_kernel(h_ref, of_ref, on_ref, mod_ref, g2n_ref, wo_hbm, w1_hbm, w2_hbm, o_ref,
                      wo_ref, w1_ref, w2_ref, wo_stage, w1_stage, w2_stage, sem, *, layer):
    _load_weights_once([wo_hbm.at[layer // 2], w1_hbm.at[layer], w2_hbm.at[layer]], [wo_ref, w1_ref, w2_ref],
                       [wo_stage, w1_stage, w2_stage], sem)
    tc = h_ref.shape[0] // TOKEN_CHAINS

    def prologue(i):
        rows = slice(i * tc, (i + 1) * tc)
        st = {}

        def s0():
            st["y"] = _dot(of_ref[rows, :], wo_ref[:FOUR_WIDTH, :])

        def s1():
            st["y"] = st["y"] + _dot(on_ref[rows, :], wo_ref[FOUR_WIDTH:, :])

        def s3():
            return _mlp_inputs(h_ref[rows, :], st["y"], mod_ref, g2n_ref)

        return _pad_stages([s0, s1, s3])

    _run_chains([prologue(i) for i in range(TOKEN_CHAINS)], mod_ref, w1_ref, w2_ref, o_ref, tc)


def _even_tail(h, o_four, o_na, mod_rows, w_out, norm2_g, w1, w2, layer, tm):
    b, n, _ = h.shape
    tok = lambda bi, j: (bi, j, 0)
    const2 = lambda bi, j: (0, 0)
    return pl.pallas_call(
        functools.partial(_even_tail_kernel, layer=layer),
        grid=(b, n // tm),
        in_specs=[
            pl.BlockSpec((None, tm, D_MODEL), tok),
            pl.BlockSpec((None, tm, FOUR_WIDTH), tok),
            pl.BlockSpec((None, tm, NA_WIDTH), tok),
            pl.BlockSpec((None, 1, N_MOD * D_MODEL), lambda bi, j: (bi, 0, 0)),
            pl.BlockSpec((1, D_MODEL), const2),
            pl.BlockSpec(memory_space=pl.ANY),
            pl.BlockSpec(memory_space=pl.ANY),
            pl.BlockSpec(memory_space=pl.ANY),
        ],
        out_specs=pl.BlockSpec((None, tm, D_MODEL), tok),
        out_shape=jax.ShapeDtypeStruct(h.shape, F32),
        scratch_shapes=_weight_scratch([w_out.shape[1:], w1.shape[1:], w2.shape[1:]]),
        compiler_params=_params("arbitrary", "arbitrary"),
        name="even_tail",
    )(h, o_four, o_na, mod_rows, norm2_g, w_out, w1, w2)


HALO = 8


def _shift_rows(a, s):
    return pltpu.roll(a, s % a.shape[0], 0)


def _pool_kernel(hp_ref, h_ref, hn_ref, mod_ref, g1n_ref, ps_ref, g2n_ref, wp_hbm, w1_hbm, w2_hbm, o_ref,
                 wp_ref, w1_ref, w2_ref, wp_stage, w1_stage, w2_stage, sem, *, seq_len, layer):
    _load_weights_once([wp_hbm.at[layer // 2], w1_hbm.at[layer], w2_hbm.at[layer]], [wp_ref, w1_ref, w2_ref],
                       [wp_stage, w1_stage, w2_stage], sem)
    j = pl.program_id(1)
    tm = h_ref.shape[0]
    tc = tm // TOKEN_CHAINS
    gc = POOL_GC
    gain, shift, scale = g1n_ref[...], _mod_chunk(mod_ref, 0), _mod_chunk(mod_ref, 1)
    unorm = lambda x: _norm_modulate(x, gain, shift, scale)

    def prologue(i):
        r0 = i * tc
        st = {}

        def normed():
            st["u"] = unorm(h_ref[r0:r0 + tc, :])

        def with_halo():
            if i == 0:
                u_prev = jnp.where(j > 0, unorm(hp_ref[...]), 0.0)
            else:
                u_prev = unorm(h_ref[r0 - HALO:r0, :])
            if i == TOKEN_CHAINS - 1:
                u_next = jnp.where(j < pl.num_programs(1) - 1, unorm(hn_ref[...]), 0.0)
            else:
                u_next = unorm(h_ref[r0 + tc:r0 + tc + HALO, :])
            st["ue"] = jnp.concatenate([u_prev, st["u"], u_next], axis=0)

        def sum2():
            st["s2"] = st["ue"] + _shift_rows(st["ue"], 1)

        def sum4():
            s2 = st["s2"][:, gc:]
            st["s4"] = _shift_rows(s2, 1) + _shift_rows(s2, -1)

        def sum8_16():
            s4 = st["s4"][:, gc:]
            s8 = _shift_rows(s4, 2) + _shift_rows(s4, -2)
            s16 = _shift_rows(s8[:, gc:], 4) + _shift_rows(s8[:, gc:], -4)
            st["sums"] = (st["s2"][:, :gc], st["s4"][:, :gc], s8[:, :gc], s16)

        def group_linear(groups):
            def run():
                t = j * tm + r0 + lax.broadcasted_iota(jnp.int32, (tc, 1), 0)
                for gi in groups:
                    w = POOL_SIZES[gi]
                    lo = jnp.maximum(t - w // 2, 0)
                    hi = jnp.minimum(t + w - w // 2, seq_len)
                    inv_cnt = 1.0 / (hi - lo).astype(F32)
                    pooled = st["sums"][gi][HALO:HALO + tc] * inv_cnt - st["u"][:, gi * gc:(gi + 1) * gc]
                    st["y%d" % gi] = _dot(pooled.astype(BF16), wp_ref[gi * gc:(gi + 1) * gc, :]) * ps_ref[:, gi * gc:(gi + 1) * gc]
            return run

        def mlp_inputs():
            y = jnp.concatenate([st["y%d" % gi] for gi in range(len(POOL_SIZES))], axis=1)
            return _mlp_inputs(h_ref[r0:r0 + tc, :], y, mod_ref, g2n_ref)

        return _pad_stages([normed, with_halo, sum2, sum4, sum8_16, group_linear((0, 1)), group_linear((2, 3)),
                            mlp_inputs], sizes=POOL_STAGE_SIZES)

    _run_chains([prologue(i) for i in range(TOKEN_CHAINS)], mod_ref, w1_ref, w2_ref, o_ref, tc)


def _odd_layer(h, mod_rows, norm1_g, w_pool, pool_scale, norm2_g, w1, w2, layer, tm):
    b, n, _ = h.shape
    tok = lambda bi, j: (bi, j, 0)
    const2 = lambda bi, j: (0, 0)
    per_tile = tm // HALO
    last = n // HALO - 1
    return pl.pallas_call(
        functools.partial(_pool_kernel, seq_len=n, layer=layer),
        grid=(b, n // tm),
        in_specs=[
            pl.BlockSpec((None, HALO, D_MODEL), lambda bi, j: (bi, jnp.maximum(j * per_tile - 1, 0), 0)),
            pl.BlockSpec((None, tm, D_MODEL), tok),
            pl.BlockSpec((None, HALO, D_MODEL), lambda bi, j: (bi, jnp.minimum((j + 1) * per_tile, last), 0)),
            pl.BlockSpec((None, 1, N_MOD * D_MODEL), lambda bi, j: (bi, 0, 0)),
            pl.BlockSpec((1, D_MODEL), const2),
            pl.BlockSpec((1, D_MODEL), const2),
            pl.BlockSpec((1, D_MODEL), const2),
            pl.BlockSpec(memory_space=pl.ANY),
            pl.BlockSpec(memory_space=pl.ANY),
            pl.BlockSpec(memory_space=pl.ANY),
        ],
        out_specs=pl.BlockSpec((None, tm, D_MODEL), tok),
        out_shape=jax.ShapeDtypeStruct(h.shape, F32),
        scratch_shapes=_weight_scratch([w_pool.shape[1:], w1.shape[1:], w2.shape[1:]]),
        compiler_params=_params("arbitrary", "arbitrary"),
        name="odd_layer",
    )(h, h, h, mod_rows, norm1_g, pool_scale, norm2_g, w_pool, w1, w2)


def _head_segments():
    lanes = 256
    seg = (np.arange(lanes)[:, None] // NA_HEAD_DIM) == (np.arange(lanes)[None, :] // NA_HEAD_DIM)
    return jnp.asarray(seg.astype(np.float32)).astype(BF16)


def kernel(x, c, ctx, c_ctx, w_mod, b_mod, norm1_g, norm2_g, w_in_even, w_four, q_norm_g, k_norm_g, rpb, w_out_even, w_pool, pool_scale, w_mlp1, w_mlp2):
    b = x.shape[0]
    depth = w_mod.shape[0]
    assert depth == 2, "layer 0 is the even (Fourier + attention) layer, layer 1 the odd (pooling) layer"
    ctx_row = b
    pad = (-(b + 1)) % SUBLANES
    c_rows = jnp.concatenate([c, c_ctx[None, :], jnp.zeros((pad, D_MODEL), F32)], axis=0)
    mod = _modulation(c_rows, w_mod, b_mod)
    mod = mod.reshape(depth, mod.shape[1], 1, N_MOD * D_MODEL)

    row = lambda a: a.reshape(1, -1)
    heads = lambda g: jnp.tile(g, NA_HEADS).reshape(1, NA_WIDTH)
    seg = _head_segments()

    w_in = w_in_even[0].astype(BF16)
    qg, kg = heads(q_norm_g[0]), heads(k_norm_g[0])
    f, q, k, v = _in_projection(x, mod[0], lambda bi: bi, row(norm1_g[0]), w_in, qg, kg, seg, tm=1024)
    _, _, kc, vc = _in_projection(ctx, mod[0], lambda bi: ctx_row, row(norm1_g[0]), w_in, qg, kg, seg,
                                  tm=ctx.shape[1])
    o_four = _fourier_mix(f, w_four[0])
    o_na = _neighbourhood_attention(q, k, v, kc, vc, rpb[0])
    h = _even_tail(x, o_four, o_na, mod[0], w_out_even, row(norm2_g[0]), w_mlp1, w_mlp2, layer=0, tm=1024)

    w_pool_rows = w_pool.reshape(w_pool.shape[0], len(POOL_SIZES) * POOL_GC, POOL_GC)
    h = _odd_layer(h, mod[1], row(norm1_g[1]), w_pool_rows, row(pool_scale[0]), row(norm2_g[1]), w_mlp1, w_mlp2,
                   layer=1, tm=1024)
    return h
```

```python
import functools

import numpy as np
import jax
import jax.numpy as jnp
from jax import lax
from jax.experimental import pallas as pl
from jax.experimental.pallas import tpu as pltpu

D_MODEL = 1024
GRID_W = 64
FOUR_GROUPS = 4
FOUR_GC = 128
FOUR_WIDTH = 512
NA_HEADS = 8
NA_HEAD_DIM = 64
NA_WIDTH = 512
NA_WIN_ROWS = 8
NA_WIN_COLS = 16
POOL_SIZES = (2, 4, 8, 16)
POOL_GC = 256
MLP_HIDDEN = 4096
N_MOD = 6
RMS_EPS = 1e-6

VMEM_LIMIT_BYTES = 56 * 1024 * 1024
SUBLANES = 8
WEIGHT_STAGE_BYTES = 1024 * 1024
WEIGHT_STAGE_SLOTS = 4

F32 = jnp.float32
BF16 = jnp.bfloat16
MASK_VALUE = -1e30
LOG2_E = 1.4426950408889634

FFT_N1 = 128
FFT_N2 = 64
Y1_PITCH = 2 * FFT_N1 + 8
Z_PITCH = FFT_N1 + 8
FFT_UNROLL = 32

NA_Q_ROWS = 8
NA_BAND_ROWS = NA_Q_ROWS + NA_WIN_ROWS
NA_Q_COLS = 16
NA_K_COLS = NA_Q_COLS + NA_WIN_COLS
NA_GROUPS_PER_STEP = 4
TOKEN_CHAINS = 4
INPROJ_CHAIN_ROWS = 256
MLP_CHUNK = 1024
MLP_CHUNKS = MLP_HIDDEN // MLP_CHUNK
POOL_STAGE_SIZES = (2, 2, 2, 2)


def _dot(a, b):
    return jnp.dot(a, b, preferred_element_type=F32)


def _dot_f32(a, b):
    return jnp.dot(a, b, preferred_element_type=F32, precision=lax.Precision.HIGHEST)


def _dot_nt(a, b):
    return lax.dot_general(a, b, (((1,), (1,)), ((), ())), preferred_element_type=F32)


def _params(*sem):
    return pltpu.CompilerParams(dimension_semantics=sem, vmem_limit_bytes=VMEM_LIMIT_BYTES)


def _resident(shape, index_map):
    return pl.BlockSpec(shape, index_map, pipeline_mode=pl.Buffered(1))


def _stage_rows(shape):
    k, n = shape
    rows = min(k, max(2 * SUBLANES, WEIGHT_STAGE_BYTES // (4 * n)))
    assert k % rows == 0
    return rows


def _weight_scratch(shapes):
    scratch = [pltpu.VMEM(shape, BF16) for shape in shapes]
    for k, n in shapes:
        rows = _stage_rows((k, n))
        scratch.append(pltpu.VMEM((min(WEIGHT_STAGE_SLOTS, k // rows), rows, n), F32))
    return scratch + [pltpu.SemaphoreType.DMA((WEIGHT_STAGE_SLOTS * len(shapes),))]


def _load_weights_once(sources, weights, stages, sem):
    @pl.when((pl.program_id(0) == 0) & (pl.program_id(1) == 0))
    def _():
        for i, (src, dst, stage) in enumerate(zip(sources, weights, stages)):
            slots, rows = stage.shape[0], stage.shape[1]
            chunks = dst.shape[0] // rows
            ahead = max(slots - 1, 1)

            def copy(c):
                return pltpu.make_async_copy(src.at[pl.ds(c * rows, rows), :], stage.at[c % slots],
                                             sem.at[WEIGHT_STAGE_SLOTS * i + c % slots])

            for c in range(min(ahead, chunks)):
                copy(c).start()
            for c in range(chunks):
                copy(c).wait()
                dst[c * rows:(c + 1) * rows, :] = stage[c % slots].astype(BF16)
                if c + ahead < chunks:
                    copy(c + ahead).start()


def _mod_kernel(c_ref, w_ref, b_ref, o_ref):
    c = c_ref[...]
    s = c * (1.0 / (1.0 + jnp.exp(-c)))
    s_hi = s.astype(BF16)
    s_lo = (s - s_hi.astype(F32)).astype(BF16)
    w = w_ref[...].astype(BF16)
    o_ref[...] = _dot(s_hi, w) + _dot(s_lo, w) + b_ref[...]


def _modulation(c_rows, w_mod, b_mod):
    depth = w_mod.shape[0]
    rows = c_rows.shape[0]
    tn = 2 * D_MODEL
    return pl.pallas_call(
        _mod_kernel,
        grid=(depth, N_MOD * D_MODEL // tn),
        in_specs=[
            pl.BlockSpec((rows, D_MODEL), lambda l, j: (0, 0)),
            pl.BlockSpec((None, D_MODEL, tn), lambda l, j: (l, 0, j)),
            pl.BlockSpec((None, 1, tn), lambda l, j: (l, 0, j)),
        ],
        out_specs=pl.BlockSpec((None, rows, tn), lambda l, j: (l, 0, j)),
        out_shape=jax.ShapeDtypeStruct((depth, rows, N_MOD * D_MODEL), F32),
        compiler_params=_params("parallel", "parallel"),
        name="modulation",
    )(c_rows, w_mod, b_mod.reshape(depth, 1, N_MOD * D_MODEL))


def _norm_modulate(x, gain, shift, scale):
    ms = jnp.mean(x * x, axis=-1, keepdims=True)
    return (x * lax.rsqrt(ms + RMS_EPS)) * (gain * (1.0 + scale)) + shift


def _head_norm(a, gain, seg_ref):
    sq = (a * a).astype(BF16)
    half = seg_ref.shape[0]
    ms = jnp.concatenate([_dot(sq[:, :half], seg_ref[...]), _dot(sq[:, half:], seg_ref[...])], axis=1)
    return a * lax.rsqrt(ms * (1.0 / NA_HEAD_DIM) + RMS_EPS) * gain


def _inproj_kernel(x_ref, mod_ref, g_ref, w_ref, qg_ref, kg_ref, seg_ref, f_ref, q_ref, k_ref, v_ref):
    tm = x_ref.shape[0]
    chains = max(1, tm // INPROJ_CHAIN_ROWS)
    tc = tm // chains

    def normed(i):
        x = x_ref[i * tc:(i + 1) * tc, :]
        return _norm_modulate(x, g_ref[...], mod_ref[:, 0:D_MODEL], mod_ref[:, D_MODEL:2 * D_MODEL]).astype(BF16)

    def outputs(i, p):
        rows = slice(i * tc, (i + 1) * tc)
        f_ref[rows, :] = p[:, :FOUR_WIDTH]
        q = p[:, FOUR_WIDTH:FOUR_WIDTH + NA_WIDTH]
        k = p[:, FOUR_WIDTH + NA_WIDTH:FOUR_WIDTH + 2 * NA_WIDTH]
        q_ref[rows, :] = (_head_norm(q, qg_ref[...], seg_ref) * (NA_HEAD_DIM ** -0.5 * LOG2_E)).astype(BF16)
        k_ref[rows, :] = _head_norm(k, kg_ref[...], seg_ref)
        v_ref[rows, :] = p[:, FOUR_WIDTH + 2 * NA_WIDTH:]

    u, p = {}, {}
    for step in range(chains + 2):
        if step < chains:
            u[step] = normed(step)
        if 0 <= step - 1 < chains:
            p[step - 1] = _dot(u.pop(step - 1), w_ref[...])
        if 0 <= step - 2 < chains:
            outputs(step - 2, p.pop(step - 2))


def _in_projection(x, mod_rows, row_of_batch, norm_g, w_in, q_g, k_g, seg, tm):
    b, n, _ = x.shape
    width = w_in.shape[1]
    tok = lambda bi, j: (bi, j, 0)
    const2 = lambda bi, j: (0, 0)
    out_tok = lambda w: pl.BlockSpec((None, tm, w), tok)
    return pl.pallas_call(
        _inproj_kernel,
        grid=(b, n // tm),
        in_specs=[
            pl.BlockSpec((None, tm, D_MODEL), tok),
            pl.BlockSpec((None, 1, N_MOD * D_MODEL), lambda bi, j: (row_of_batch(bi), 0, 0)),
            pl.BlockSpec((1, D_MODEL), const2),
            _resident((D_MODEL, width), const2),
            pl.BlockSpec((1, NA_WIDTH), const2),
            pl.BlockSpec((1, NA_WIDTH), const2),
            pl.BlockSpec(seg.shape, const2),
        ],
        out_specs=[out_tok(FOUR_WIDTH), out_tok(NA_WIDTH), out_tok(NA_WIDTH), out_tok(NA_WIDTH)],
        out_shape=[
            jax.ShapeDtypeStruct((b, n, FOUR_WIDTH), F32),
            jax.ShapeDtypeStruct((b, n, NA_WIDTH), BF16),
            jax.ShapeDtypeStruct((b, n, NA_WIDTH), F32),
            jax.ShapeDtypeStruct((b, n, NA_WIDTH), F32),
        ],
        compiler_params=_params("parallel", "parallel"),
        name="in_projection",
    )(x, mod_rows, norm_g, w_in, q_g, k_g, seg)


def _fft_tables(n):
    n1, n2 = FFT_N1, FFT_N2
    assert n1 * n2 == n
    k1 = np.arange(n1)[None, :, None]
    pos = (n2 * np.arange(n1)[None, None, :] + np.arange(n2)[:, None, None])
    ang = 2.0 * np.pi * ((k1 * pos) % n) / n
    l1 = np.concatenate([np.cos(ang), -np.sin(ang)], axis=1)
    a2 = 2.0 * np.pi * ((np.arange(n2)[:, None] * np.arange(n2)[None, :]) % n2) / n2
    c2, s2 = np.cos(a2), np.sin(a2)
    l2 = np.block([[c2, s2], [-s2, c2]])
    ac = 2.0 * np.pi * ((np.arange(FOUR_GC)[:, None] * np.arange(FOUR_GC)[None, :]) % FOUR_GC) / FOUR_GC
    cs = np.concatenate([np.cos(ac), np.sin(ac)], axis=0)
    return l1.astype(np.float32), l2.astype(np.float32), cs.astype(np.float32)


def _fft_kernel(f_ref, l1_ref, l2_ref, cs_ref, w_ref, o_ref, y1_ref, z_ref, *, ortho):
    n1, n2 = FFT_N1, FFT_N2

    def stage1(j, carry):
        r = f_ref[pl.ds(j, n1, stride=n2), :].astype(BF16)
        y1_ref[pl.ds(pl.multiple_of(j * Y1_PITCH, SUBLANES), 2 * n1), :] = _dot(l1_ref[j], r)
        return carry

    lax.fori_loop(0, n2, stage1, 0, unroll=FFT_UNROLL)

    l2 = l2_ref[...]

    def stage2(k1, carry):
        yr = y1_ref[pl.ds(k1, n2, stride=Y1_PITCH), :]
        yi = y1_ref[pl.ds(n1 + k1, n2, stride=Y1_PITCH), :]
        z = _dot(l2, jnp.concatenate([yr, yi], axis=0).astype(BF16))
        z_ref[0, pl.ds(k1, n2, stride=Z_PITCH), :] = z[:n2]
        z_ref[1, pl.ds(k1, n2, stride=Z_PITCH), :] = z[n2:]
        return carry

    lax.fori_loop(0, n1, stage2, 0, unroll=FFT_UNROLL)

    wf = (_dot_f32(cs_ref[...], w_ref[...]) * ortho).astype(BF16)
    wr, wi = wf[:FOUR_GC], wf[FOUR_GC:]

    def stage3(k2, carry):
        base = pl.multiple_of(k2 * Z_PITCH, SUBLANES)
        zr = z_ref[0, pl.ds(base, n1), :].astype(BF16)
        zi = z_ref[1, pl.ds(base, n1), :].astype(BF16)
        o_ref[pl.ds(pl.multiple_of(k2 * n1, n1), n1), :] = (_dot(zr, wr) + _dot(zi, wi)).astype(o_ref.dtype)
        return carry

    lax.fori_loop(0, n2, stage3, 0, unroll=FFT_UNROLL)


def _fourier_mix(f, w_four):
    b, n, _ = f.shape
    l1, l2, cs = _fft_tables(n)
    l1 = jnp.asarray(l1).astype(BF16)
    l2 = jnp.asarray(l2).astype(BF16)
    cs = jnp.asarray(cs)
    ortho = float(1.0 / np.sqrt(n * FOUR_GC))
    slab = lambda bi, g: (bi, 0, g)
    return pl.pallas_call(
        functools.partial(_fft_kernel, ortho=ortho),
        grid=(b, FOUR_GROUPS),
        in_specs=[
            pl.BlockSpec((None, n, FOUR_GC), slab),
            _resident(l1.shape, lambda bi, g: (0, 0, 0)),
            _resident(l2.shape, lambda bi, g: (0, 0)),
            _resident(cs.shape, lambda bi, g: (0, 0)),
            pl.BlockSpec((None, FOUR_GC, FOUR_GC), lambda bi, g: (g, 0, 0)),
        ],
        out_specs=pl.BlockSpec((None, n, FOUR_GC), slab),
        out_shape=jax.ShapeDtypeStruct((b, n, FOUR_WIDTH), BF16),
        scratch_shapes=[
            pltpu.VMEM((FFT_N2 * Y1_PITCH, FOUR_GC), F32),
            pltpu.VMEM((2, FFT_N2 * Z_PITCH, FOUR_GC), F32),
        ],
        compiler_params=_params("parallel", "parallel"),
        name="fourier_mix",
    )(f, l1, l2, cs, w_four)


def _band_start(g, rows):
    return jnp.clip(NA_Q_ROWS * g - (NA_BAND_ROWS - NA_Q_ROWS) // 2, 0, rows - NA_BAND_ROWS)


def _patch_col_start(cb):
    return int(np.clip(NA_Q_COLS * cb - (NA_K_COLS - NA_Q_COLS) // 2, 0, GRID_W - NA_K_COLS))


def _natten_bias_tables(rpb, rows):
    heads = rpb.shape[0]
    groups = rows // NA_Q_ROWS
    n_dr, n_dc = 2 * NA_WIN_ROWS - 1, 2 * NA_WIN_COLS - 1
    n_cb = GRID_W // NA_Q_COLS
    nq, nk = NA_Q_ROWS * NA_Q_COLS, NA_BAND_ROWS * NA_K_COLS
    lane_tiles = 128 // NA_K_COLS
    lead = NA_WIN_ROWS
    strip_tiles = 2 * NA_BAND_ROWS
    col = np.arange(GRID_W)
    col_start = np.clip(col - NA_WIN_COLS // 2, 0, GRID_W - NA_WIN_COLS)
    col_ok = (col[None, :] >= col_start[:, None]) & (col[None, :] < col_start[:, None] + NA_WIN_COLS)
    dc = np.clip(col[None, :] - col[:, None] + (NA_WIN_COLS - 1), 0, n_dc - 1)
    onehot_dc = (dc[None] == np.arange(n_dc)[:, None, None]).astype(np.float32)
    tiles = jnp.einsum('hre,eqk->hrqk', rpb.astype(F32), jnp.asarray(onehot_dc), precision=lax.Precision.HIGHEST)
    tiles = jnp.where(jnp.asarray(col_ok), tiles * LOG2_E, MASK_VALUE)
    tail = strip_tiles + lane_tiles - 1 - lead - n_dr
    tiles = jnp.pad(tiles, ((0, 0), (lead, tail), (0, 0), (0, 0)), constant_values=MASK_VALUE)
    strips = []
    for cb in range(n_cb):
        q0, k0 = NA_Q_COLS * cb, _patch_col_start(cb)
        assert col_start[q0] >= k0 and col_start[q0 + NA_Q_COLS - 1] + NA_WIN_COLS <= k0 + NA_K_COLS
        sub = tiles[:, :, q0:q0 + NA_Q_COLS, k0:k0 + NA_K_COLS].transpose(0, 2, 1, 3)
        strips.append(jnp.stack([sub[:, :, r:r + strip_tiles].reshape(heads, NA_Q_COLS, strip_tiles * NA_K_COLS)
                                 for r in range(lane_tiles)], axis=1))
    strips = jnp.stack(strips, axis=1)

    plan = []
    for kind, g in enumerate((0, groups // 2, groups - 1)):
        band0 = int(np.clip(NA_Q_ROWS * g - (NA_BAND_ROWS - NA_Q_ROWS) // 2, 0, rows - NA_BAND_ROWS))
        for a in range(NA_Q_ROWS):
            qr = NA_Q_ROWS * g + a
            row_start = int(np.clip(qr - NA_WIN_ROWS // 2, 0, rows - NA_WIN_ROWS))
            assert band0 <= row_start and row_start + NA_WIN_ROWS <= band0 + NA_BAND_ROWS
            first_tile = band0 - qr + NA_WIN_ROWS - 1 + lead
            assert 0 <= first_tile and first_tile + NA_BAND_ROWS <= strip_tiles + lane_tiles - 1
            for cb in range(n_cb):
                for hd in range(2):
                    plan.append((kind * n_cb + cb, hd, cb, a, first_tile, row_start - band0))

    def build(strip_ref, o_ref):
        patch_row = lax.broadcasted_iota(jnp.int32, (1, nk), 1) // NA_K_COLS
        for idx, hd, cb, a, first_tile, first_row in plan:
            lane0 = (first_tile // lane_tiles) * 128
            val = strip_ref[hd, cb, first_tile % lane_tiles, :, lane0:lane0 + nk]
            in_window = (patch_row >= first_row) & (patch_row < first_row + NA_WIN_ROWS)
            r0 = hd * nq + a * NA_Q_COLS
            o_ref[idx, r0:r0 + NA_Q_COLS, :] = jnp.where(in_window, val, MASK_VALUE)

    return pl.pallas_call(
        build,
        grid=(heads // 2,),
        in_specs=[pl.BlockSpec((2,) + strips.shape[1:], lambda hp: (hp, 0, 0, 0, 0))],
        out_specs=pl.BlockSpec((None, 3 * n_cb, 2 * nq, nk), lambda hp: (hp, 0, 0, 0)),
        out_shape=jax.ShapeDtypeStruct((heads // 2, 3 * n_cb, 2 * nq, nk), F32),
        compiler_params=_params("parallel"),
        name="bias_table",
    )(strips)


def _natten_kernel(q_ref, k_ref, v_ref, kc_ref, vc_ref, tab_ref, o_ref, *, rows):
    nq = NA_Q_ROWS * NA_Q_COLS
    groups = rows // NA_Q_ROWS
    n_cb = GRID_W // NA_Q_COLS
    first_head = lax.broadcasted_iota(jnp.int32, (1, 2 * NA_HEAD_DIM), 1) < NA_HEAD_DIM
    kc, vc = kc_ref[...].astype(BF16), vc_ref[...].astype(BF16)
    chains = [(g, cb) for g in range(NA_GROUPS_PER_STEP) for cb in range(n_cb)]

    def group_index(g):
        return pl.program_id(2) * NA_GROUPS_PER_STEP + g

    def q_rows(g, cb, a):
        start = (g * NA_Q_ROWS + a) * GRID_W + cb * NA_Q_COLS
        return slice(start, start + NA_Q_COLS)

    def patch(ref, g, cb):
        tok0 = _band_start(group_index(g), rows) * GRID_W + _patch_col_start(cb)
        parts = [ref[pl.ds(pl.multiple_of(tok0 + bb * GRID_W, SUBLANES), NA_K_COLS), :] for bb in range(NA_BAND_ROWS)]
        return jnp.concatenate(parts, axis=0).astype(BF16)

    def scores(g, cb):
        gi = group_index(g)
        kind = jnp.where(gi == 0, 0, jnp.where(gi == groups - 1, 2, 1))
        q2 = jnp.concatenate([q_ref[q_rows(g, cb, a), :] for a in range(NA_Q_ROWS)], axis=0)
        zero = jnp.zeros_like(q2)
        qs = jnp.concatenate([jnp.where(first_head, q2, zero), jnp.where(first_head, zero, q2)], axis=0)
        return _dot_nt(qs, patch(k_ref, g, cb)) + tab_ref[kind * n_cb + cb], _dot_nt(qs, kc)

    def softmax(s_win, s_ctx):
        m = jnp.maximum(jnp.max(s_win, axis=-1, keepdims=True), jnp.max(s_ctx, axis=-1, keepdims=True))
        p_win = jnp.exp2(s_win - m)
        p_ctx = jnp.exp2(s_ctx - m)
        l = jnp.sum(p_win, axis=-1, keepdims=True) + jnp.sum(p_ctx, axis=-1, keepdims=True)
        return p_win.astype(BF16), p_ctx.astype(BF16), l

    def weighted_values(g, cb, p_win, p_ctx, l):
        o = (_dot(p_win, patch(v_ref, g, cb)) + _dot(p_ctx, vc)) / l
        o = jnp.where(first_head, o[:nq], o[nq:]).astype(o_ref.dtype)
        for a in range(NA_Q_ROWS):
            o_ref[q_rows(g, cb, a), :] = o[a * NA_Q_COLS:(a + 1) * NA_Q_COLS]

    s, p = {}, {}
    for step in range(len(chains) + 2):
        if step < len(chains):
            s[step] = scores(*chains[step])
        if 0 <= step - 1 < len(chains):
            p[step - 1] = softmax(*s.pop(step - 1))
        if 0 <= step - 2 < len(chains):
            weighted_values(*chains[step - 2], *p.pop(step - 2))


def _neighbourhood_attention(q, k, v, kc, vc, rpb):
    b, n, _ = q.shape
    rows = n // GRID_W
    steps = rows // (NA_Q_ROWS * NA_GROUPS_PER_STEP)
    ctx_len = kc.shape[1]
    tq = NA_GROUPS_PER_STEP * NA_Q_ROWS * GRID_W
    tables = _natten_bias_tables(rpb, rows)
    lanes = 2 * NA_HEAD_DIM
    seq = lambda bi, hp, i: (bi, 0, hp)
    return pl.pallas_call(
        functools.partial(_natten_kernel, rows=rows),
        grid=(b, NA_HEADS // 2, steps),
        in_specs=[
            pl.BlockSpec((None, tq, lanes), lambda bi, hp, i: (bi, i, hp)),
            pl.BlockSpec((None, n, lanes), seq),
            pl.BlockSpec((None, n, lanes), seq),
            pl.BlockSpec((None, ctx_len, lanes), seq),
            pl.BlockSpec((None, ctx_len, lanes), seq),
            pl.BlockSpec((None,) + tables.shape[1:], lambda bi, hp, i: (hp, 0, 0, 0)),
        ],
        out_specs=pl.BlockSpec((None, tq, lanes), lambda bi, hp, i: (bi, i, hp)),
        out_shape=jax.ShapeDtypeStruct((b, n, NA_WIDTH), BF16),
        compiler_params=_params("parallel", "parallel", "arbitrary"),
        name="neighbourhood_attention",
    )(q, k, v, kc, vc, tables)


def _mod_chunk(mod_ref, idx):
    return mod_ref[:, idx * D_MODEL:(idx + 1) * D_MODEL]


def _mlp_inputs(h, y, mod_ref, g2n_ref):
    h1 = h + _mod_chunk(mod_ref, 2) * y
    u = _norm_modulate(h1, g2n_ref[...], _mod_chunk(mod_ref, 3), _mod_chunk(mod_ref, 4)).astype(BF16)
    return h1, u


def _mlp_chunk(u, acc, c, w1_ref, w2_ref):
    lo, hi = c * MLP_CHUNK, (c + 1) * MLP_CHUNK
    a = jnp.maximum(_dot(u, w1_ref[:, lo:hi]), 0.0)
    part = _dot((a * a).astype(BF16), w2_ref[lo:hi, :])
    return part if acc is None else acc + part


def _pad_stages(stages, sizes=None):
    if sizes is None:
        sizes = [1] * (len(stages) - 1) + [0] * (MLP_CHUNKS - len(stages)) + [1]
    assert len(sizes) == MLP_CHUNKS and sum(sizes) == len(stages)

    def merged(group):
        def run():
            out = None
            for stage in group:
                out = stage()
            return out
        return run

    starts = [sum(sizes[:k]) for k in range(MLP_CHUNKS)]
    return [merged(stages[s0:s0 + n]) for s0, n in zip(starts, sizes)]


def _run_chains(prologues, mod_ref, w1_ref, w2_ref, o_ref, tc):
    n = len(prologues)
    for stage in prologues[0][:-1]:
        stage()
    h1, u = prologues[0][-1]()
    for i in range(n):
        acc, nxt = None, None
        for c in range(MLP_CHUNKS):
            acc = _mlp_chunk(u, acc, c, w1_ref, w2_ref)
            if i + 1 < n:
                nxt = prologues[i + 1][c]()
        o_ref[i * tc:(i + 1) * tc, :] = h1 + _mod_chunk(mod_ref, 5) * acc
        if i + 1 < n:
            h1, u = nxt


def _even_tail_kernel(h_ref, of_ref, on_ref, mod_ref, g2n_ref, wo_hbm, w1_hbm, w2_hbm, o_ref,
                      wo_ref, w1_ref, w2_ref, wo_stage, w1_stage, w2_stage, sem, *, layer):
    _load_weights_once([wo_hbm.at[layer // 2], w1_hbm.at[layer], w2_hbm.at[layer]], [wo_ref, w1_ref, w2_ref],
                       [wo_stage, w1_stage, w2_stage], sem)
    tc = h_ref.shape[0] // TOKEN_CHAINS

    def prologue(i):
        rows = slice(i * tc, (i + 1) * tc)
        st = {}

        def s0():
            st["y"] = _dot(of_ref[rows, :], wo_ref[:FOUR_WIDTH, :])

        def s1():
            st["y"] = st["y"] + _dot(on_ref[rows, :], wo_ref[FOUR_WIDTH:, :])

        def s3():
            return _mlp_inputs(h_ref[rows, :], st["y"], mod_ref, g2n_ref)

        return _pad_stages([s0, s1, s3])

    _run_chains([prologue(i) for i in range(TOKEN_CHAINS)], mod_ref, w1_ref, w2_ref, o_ref, tc)


def _even_tail(h, o_four, o_na, mod_rows, w_out, norm2_g, w1, w2, layer, tm):
    b, n, _ = h.shape
    tok = lambda bi, j: (bi, j, 0)
    const2 = lambda bi, j: (0, 0)
    return pl.pallas_call(
        functools.partial(_even_tail_kernel, layer=layer),
        grid=(b, n // tm),
        in_specs=[
            pl.BlockSpec((None, tm, D_MODEL), tok),
            pl.BlockSpec((None, tm, FOUR_WIDTH), tok),
            pl.BlockSpec((None, tm, NA_WIDTH), tok),
            pl.BlockSpec((None, 1, N_MOD * D_MODEL), lambda bi, j: (bi, 0, 0)),
            pl.BlockSpec((1, D_MODEL), const2),
            pl.BlockSpec(memory_space=pl.ANY),
            pl.BlockSpec(memory_space=pl.ANY),
            pl.BlockSpec(memory_space=pl.ANY),
        ],
        out_specs=pl.BlockSpec((None, tm, D_MODEL), tok),
        out_shape=jax.ShapeDtypeStruct(h.shape, F32),
        scratch_shapes=_weight_scratch([w_out.shape[1:], w1.shape[1:], w2.shape[1:]]),
        compiler_params=_params("arbitrary", "arbitrary"),
        name="even_tail",
    )(h, o_four, o_na, mod_rows, norm2_g, w_out, w1, w2)


HALO = 8


def _shift_rows(a, s):
    return pltpu.roll(a, s % a.shape[0], 0)


def _pool_kernel(hp_ref, h_ref, hn_ref, mod_ref, g1n_ref, ps_ref, g2n_ref, wp_hbm, w1_hbm, w2_hbm, o_ref,
                 wp_ref, w1_ref, w2_ref, wp_stage, w1_stage, w2_stage, sem, *, seq_len, layer):
    _load_weights_once([wp_hbm.at[layer // 2], w1_hbm.at[layer], w2_hbm.at[layer]], [wp_ref, w1_ref, w2_ref],
                       [wp_stage, w1_stage, w2_stage], sem)
    j = pl.program_id(1)
    tm = h_ref.shape[0]
    tc = tm // TOKEN_CHAINS
    gc = POOL_GC
    gain, shift, scale = g1n_ref[...], _mod_chunk(mod_ref, 0), _mod_chunk(mod_ref, 1)
    unorm = lambda x: _norm_modulate(x, gain, shift, scale)

    def prologue(i):
        r0 = i * tc
        st = {}

        def normed():
            st["u"] = unorm(h_ref[r0:r0 + tc, :])

        def with_halo():
            if i == 0:
                u_prev = jnp.where(j > 0, unorm(hp_ref[...]), 0.0)
            else:
                u_prev = unorm(h_ref[r0 - HALO:r0, :])
            if i == TOKEN_CHAINS - 1:
                u_next = jnp.where(j < pl.num_programs(1) - 1, unorm(hn_ref[...]), 0.0)
            else:
                u_next = unorm(h_ref[r0 + tc:r0 + tc + HALO, :])
            st["ue"] = jnp.concatenate([u_prev, st["u"], u_next], axis=0)

        def sum2():
            st["s2"] = st["ue"] + _shift_rows(st["ue"], 1)

        def sum4():
            s2 = st["s2"][:, gc:]
            st["s4"] = _shift_rows(s2, 1) + _shift_rows(s2, -1)

        def sum8_16():
            s4 = st["s4"][:, gc:]
            s8 = _shift_rows(s4, 2) + _shift_rows(s4, -2)
            s16 = _shift_rows(s8[:, gc:], 4) + _shift_rows(s8[:, gc:], -4)
            st["sums"] = (st["s2"][:, :gc], st["s4"][:, :gc], s8[:, :gc], s16)

        def group_linear(groups):
            def run():
                t = j * tm + r0 + lax.broadcasted_iota(jnp.int32, (tc, 1), 0)
                for gi in groups:
                    w = POOL_SIZES[gi]
                    lo = jnp.maximum(t - w // 2, 0)
                    hi = jnp.minimum(t + w - w // 2, seq_len)
                    inv_cnt = 1.0 / (hi - lo).astype(F32)
                    pooled = st["sums"][gi][HALO:HALO + tc] * inv_cnt - st["u"][:, gi * gc:(gi + 1) * gc]
                    st["y%d" % gi] = _dot(pooled.astype(BF16), wp_ref[gi * gc:(gi + 1) * gc, :]) * ps_ref[:, gi * gc:(gi + 1) * gc]
            return run

        def mlp_inputs():
            y = jnp.concatenate([st["y%d" % gi] for gi in range(len(POOL_SIZES))], axis=1)
            return _mlp_inputs(h_ref[r0:r0 + tc, :], y, mod_ref, g2n_ref)

        return _pad_stages([normed, with_halo, sum2, sum4, sum8_16, group_linear((0, 1)), group_linear((2, 3)),
                            mlp_inputs], sizes=POOL_STAGE_SIZES)

    _run_chains([prologue(i) for i in range(TOKEN_CHAINS)], mod_ref, w1_ref, w2_ref, o_ref, tc)


def _odd_layer(h, mod_rows, norm1_g, w_pool, pool_scale, norm2_g, w1, w2, layer, tm):
    b, n, _ = h.shape
    tok = lambda bi, j: (bi, j, 0)
    const2 = lambda bi, j: (0, 0)
    per_tile = tm // HALO
    last = n // HALO - 1
    return pl.pallas_call(
        functools.partial(_pool_kernel, seq_len=n, layer=layer),
        grid=(b, n // tm),
        in_specs=[
            pl.BlockSpec((None, HALO, D_MODEL), lambda bi, j: (bi, jnp.maximum(j * per_tile - 1, 0), 0)),
            pl.BlockSpec((None, tm, D_MODEL), tok),
            pl.BlockSpec((None, HALO, D_MODEL), lambda bi, j: (bi, jnp.minimum((j + 1) * per_tile, last), 0)),
            pl.BlockSpec((None, 1, N_MOD * D_MODEL), lambda bi, j: (bi, 0, 0)),
            pl.BlockSpec((1, D_MODEL), const2),
            pl.BlockSpec((1, D_MODEL), const2),
            pl.BlockSpec((1, D_MODEL), const2),
            pl.BlockSpec(memory_space=pl.ANY),
            pl.BlockSpec(memory_space=pl.ANY),
            pl.BlockSpec(memory_space=pl.ANY),
        ],
        out_specs=pl.BlockSpec((None, tm, D_MODEL), tok),
        out_shape=jax.ShapeDtypeStruct(h.shape, F32),
        scratch_shapes=_weight_scratch([w_pool.shape[1:], w1.shape[1:], w2.shape[1:]]),
        compiler_params=_params("arbitrary", "arbitrary"),
        name="odd_layer",
    )(h, h, h, mod_rows, norm1_g, pool_scale, norm2_g, w_pool, w1, w2)


def _head_segments():
    lanes = 256
    seg = (np.arange(lanes)[:, None] // NA_HEAD_DIM) == (np.arange(lanes)[None, :] // NA_HEAD_DIM)
    return jnp.asarray(seg.astype(np.float32)).astype(BF16)


def kernel(x, c, ctx, c_ctx, w_mod, b_mod, norm1_g, norm2_g, w_in_even, w_four, q_norm_g, k_norm_g, rpb, w_out_even, w_pool, pool_scale, w_mlp1, w_mlp2):
    b = x.shape[0]
    depth = w_mod.shape[0]
    assert depth == 2, "layer 0 is the even (Fourier + attention) layer, layer 1 the odd (pooling) layer"
    ctx_row = b
    pad = (-(b + 1)) % SUBLANES
    c_rows = jnp.concatenate([c, c_ctx[None, :], jnp.zeros((pad, D_MODEL), F32)], axis=0)
    mod = _modulation(c_rows, w_mod, b_mod)
    mod = mod.reshape(depth, mod.shape[1], 1, N_MOD * D_MODEL)

    row = lambda a: a.reshape(1, -1)
    heads = lambda g: jnp.tile(g, NA_HEADS).reshape(1, NA_WIDTH)
    seg = _head_segments()

    w_in = w_in_even[0].astype(BF16)
    qg, kg = heads(q_norm_g[0]), heads(k_norm_g[0])
    f, q, k, v = _in_projection(x, mod[0], lambda bi: bi, row(norm1_g[0]), w_in, qg, kg, seg, tm=1024)
    _, _, kc, vc = _in_projection(ctx, mod[0], lambda bi: ctx_row, row(norm1_g[0]), w_in, qg, kg, seg,
                                  tm=ctx.shape[1])
    o_four = _fourier_mix(f, w_four[0])
    o_na = _neighbourhood_attention(q, k, v, kc, vc, rpb[0])
    h = _even_tail(x, o_four, o_na, mod[0], w_out_even, row(norm2_g[0]), w_mlp1, w_mlp2, layer=0, tm=1024)

    w_pool_rows = w_pool.reshape(w_pool.shape[0], len(POOL_SIZES) * POOL_GC, POOL_GC)
    h = _odd_layer(h, mod[1], row(norm1_g[1]), w_pool_rows, row(pool_scale[0]), row(norm2_g[1]), w_mlp1, w_mlp2,
                   layer=1, tm=1024)
    return h
```

```python
import functools

import numpy as np
import jax
import jax.numpy as jnp
from jax import lax
from jax.experimental import pallas as pl
from jax.experimental.pallas import tpu as pltpu

D_MODEL = 1024
GRID_W = 64
FOUR_GROUPS = 4
FOUR_GC = 128
FOUR_WIDTH = 512
NA_HEADS = 8
NA_HEAD_DIM = 64
NA_WIDTH = 512
NA_WIN_ROWS = 8
NA_WIN_COLS = 16
POOL_SIZES = (2, 4, 8, 16)
POOL_GC = 256
MLP_HIDDEN = 4096
N_MOD = 6
RMS_EPS = 1e-6

VMEM_LIMIT_BYTES = 56 * 1024 * 1024
SUBLANES = 8
WEIGHT_STAGE_BYTES = 1024 * 1024
WEIGHT_STAGE_SLOTS = 4

F32 = jnp.float32
BF16 = jnp.bfloat16
MASK_VALUE = -1e30
LOG2_E = 1.4426950408889634

FFT_N1 = 128
FFT_N2 = 64
Y1_PITCH = 2 * FFT_N1 + 8
Z_PITCH = FFT_N1 + 8
FFT_UNROLL = 32

NA_Q_ROWS = 8
NA_BAND_ROWS = NA_Q_ROWS + NA_WIN_ROWS
NA_Q_COLS = 16
NA_K_COLS = NA_Q_COLS + NA_WIN_COLS
NA_GROUPS_PER_STEP = 4
TOKEN_CHAINS = 4
INPROJ_CHAIN_ROWS = 256
MLP_CHUNK = 1024
MLP_CHUNKS = MLP_HIDDEN // MLP_CHUNK
POOL_STAGE_SIZES = (2, 2, 2, 2)


def _dot(a, b):
    return jnp.dot(a, b, preferred_element_type=F32)


def _dot_f32(a, b):
    return jnp.dot(a, b, preferred_element_type=F32, precision=lax.Precision.HIGHEST)


def _dot_nt(a, b):
    return lax.dot_general(a, b, (((1,), (1,)), ((), ())), preferred_element_type=F32)


def _params(*sem):
    return pltpu.CompilerParams(dimension_semantics=sem, vmem_limit_bytes=VMEM_LIMIT_BYTES)


def _resident(shape, index_map):
    return pl.BlockSpec(shape, index_map, pipeline_mode=pl.Buffered(1))


def _stage_rows(shape):
    k, n = shape
    rows = min(k, max(2 * SUBLANES, WEIGHT_STAGE_BYTES // (4 * n)))
    assert k % rows == 0
    return rows


def _weight_scratch(shapes):
    scratch = [pltpu.VMEM(shape, BF16) for shape in shapes]
    for k, n in shapes:
        rows = _stage_rows((k, n))
        scratch.append(pltpu.VMEM((min(WEIGHT_STAGE_SLOTS, k // rows), rows, n), F32))
    return scratch + [pltpu.SemaphoreType.DMA((WEIGHT_STAGE_SLOTS * len(shapes),))]


def _load_weights_once(sources, weights, stages, sem):
    @pl.when((pl.program_id(0) == 0) & (pl.program_id(1) == 0))
    def _():
        for i, (src, dst, stage) in enumerate(zip(sources, weights, stages)):
            slots, rows = stage.shape[0], stage.shape[1]
            chunks = dst.shape[0] // rows
            ahead = max(slots - 1, 1)

            def copy(c):
                return pltpu.make_async_copy(src.at[pl.ds(c * rows, rows), :], stage.at[c % slots],
                                             sem.at[WEIGHT_STAGE_SLOTS * i + c % slots])

            for c in range(min(ahead, chunks)):
                copy(c).start()
            for c in range(chunks):
                copy(c).wait()
                dst[c * rows:(c + 1) * rows, :] = stage[c % slots].astype(BF16)
                if c + ahead < chunks:
                    copy(c + ahead).start()


def _mod_kernel(c_ref, w_ref, b_ref, o_ref):
    c = c_ref[...]
    s = c * (1.0 / (1.0 + jnp.exp(-c)))
    s_hi = s.astype(BF16)
    s_lo = (s - s_hi.astype(F32)).astype(BF16)
    w = w_ref[...].astype(BF16)
    o_ref[...] = _dot(s_hi, w) + _dot(s_lo, w) + b_ref[...]


def _modulation(c_rows, w_mod, b_mod):
    depth = w_mod.shape[0]
    rows = c_rows.shape[0]
    tn = 2 * D_MODEL
    return pl.pallas_call(
        _mod_kernel,
        grid=(depth, N_MOD * D_MODEL // tn),
        in_specs=[
            pl.BlockSpec((rows, D_MODEL), lambda l, j: (0, 0)),
            pl.BlockSpec((None, D_MODEL, tn), lambda l, j: (l, 0, j)),
            pl.BlockSpec((None, 1, tn), lambda l, j: (l, 0, j)),
        ],
        out_specs=pl.BlockSpec((None, rows, tn), lambda l, j: (l, 0, j)),
        out_shape=jax.ShapeDtypeStruct((depth, rows, N_MOD * D_MODEL), F32),
        compiler_params=_params("parallel", "parallel"),
        name="modulation",
    )(c_rows, w_mod, b_mod.reshape(depth, 1, N_MOD * D_MODEL))


def _norm_modulate(x, gain, shift, scale):
    ms = jnp.mean(x * x, axis=-1, keepdims=True)
    return (x * lax.rsqrt(ms + RMS_EPS)) * (gain * (1.0 + scale)) + shift


def _head_norm(a, gain, seg_ref):
    sq = (a * a).astype(BF16)
    half = seg_ref.shape[0]
    ms = jnp.concatenate([_dot(sq[:, :half], seg_ref[...]), _dot(sq[:, half:], seg_ref[...])], axis=1)
    return a * lax.rsqrt(ms * (1.0 / NA_HEAD_DIM) + RMS_EPS) * gain


def _inproj_kernel(x_ref, mod_ref, g_ref, w_ref, qg_ref, kg_ref, seg_ref, f_ref, q_ref, k_ref, v_ref):
    tm = x_ref.shape[0]
    chains = max(1, tm // INPROJ_CHAIN_ROWS)
    tc = tm // chains

    def normed(i):
        x = x_ref[i * tc:(i + 1) * tc, :]
        return _norm_modulate(x, g_ref[...], mod_ref[:, 0:D_MODEL], mod_ref[:, D_MODEL:2 * D_MODEL]).astype(BF16)

    def outputs(i, p):
        rows = slice(i * tc, (i + 1) * tc)
        f_ref[rows, :] = p[:, :FOUR_WIDTH]
        q = p[:, FOUR_WIDTH:FOUR_WIDTH + NA_WIDTH]
        k = p[:, FOUR_WIDTH + NA_WIDTH:FOUR_WIDTH + 2 * NA_WIDTH]
        q_ref[rows, :] = (_head_norm(q, qg_ref[...], seg_ref) * (NA_HEAD_DIM ** -0.5 * LOG2_E)).astype(BF16)
        k_ref[rows, :] = _head_norm(k, kg_ref[...], seg_ref)
        v_ref[rows, :] = p[:, FOUR_WIDTH + 2 * NA_WIDTH:]

    u, p = {}, {}
    for step in range(chains + 2):
        if step < chains:
            u[step] = normed(step)
        if 0 <= step - 1 < chains:
            p[step - 1] = _dot(u.pop(step - 1), w_ref[...])
        if 0 <= step - 2 < chains:
            outputs(step - 2, p.pop(step - 2))


def _in_projection(x, mod_rows, row_of_batch, norm_g, w_in, q_g, k_g, seg, tm):
    b, n, _ = x.shape
    width = w_in.shape[1]
    tok = lambda bi, j: (bi, j, 0)
    const2 = lambda bi, j: (0, 0)
    out_tok = lambda w: pl.BlockSpec((None, tm, w), tok)
    return pl.pallas_call(
        _inproj_kernel,
        grid=(b, n // tm),
        in_specs=[
            pl.BlockSpec((None, tm, D_MODEL), tok),
            pl.BlockSpec((None, 1, N_MOD * D_MODEL), lambda bi, j: (row_of_batch(bi), 0, 0)),
            pl.BlockSpec((1, D_MODEL), const2),
            _resident((D_MODEL, width), const2),
            pl.BlockSpec((1, NA_WIDTH), const2),
            pl.BlockSpec((1, NA_WIDTH), const2),
            pl.BlockSpec(seg.shape, const2),
        ],
        out_specs=[out_tok(FOUR_WIDTH), out_tok(NA_WIDTH), out_tok(NA_WIDTH), out_tok(NA_WIDTH)],
        out_shape=[
            jax.ShapeDtypeStruct((b, n, FOUR_WIDTH), F32),
            jax.ShapeDtypeStruct((b, n, NA_WIDTH), BF16),
            jax.ShapeDtypeStruct((b, n, NA_WIDTH), F32),
            jax.ShapeDtypeStruct((b, n, NA_WIDTH), F32),
        ],
        compiler_params=_params("parallel", "parallel"),
        name="in_projection",
    )(x, mod_rows, norm_g, w_in, q_g, k_g, seg)


def _fft_tables(n):
    n1, n2 = FFT_N1, FFT_N2
    assert n1 * n2 == n
    k1 = np.arange(n1)[None, :, None]
    pos = (n2 * np.arange(n1)[None, None, :] + np.arange(n2)[:, None, None])
    ang = 2.0 * np.pi * ((k1 * pos) % n) / n
    l1 = np.concatenate([np.cos(ang), -np.sin(ang)], axis=1)
    a2 = 2.0 * np.pi * ((np.arange(n2)[:, None] * np.arange(n2)[None, :]) % n2) / n2
    c2, s2 = np.cos(a2), np.sin(a2)
    l2 = np.block([[c2, s2], [-s2, c2]])
    ac = 2.0 * np.pi * ((np.arange(FOUR_GC)[:, None] * np.arange(FOUR_GC)[None, :]) % FOUR_GC) / FOUR_GC
    cs = np.concatenate([np.cos(ac), np.sin(ac)], axis=0)
    return l1.astype(np.float32), l2.astype(np.float32), cs.astype(np.float32)


def _fft_kernel(f_ref, l1_ref, l2_ref, cs_ref, w_ref, o_ref, y1_ref, z_ref, *, ortho):
    n1, n2 = FFT_N1, FFT_N2

    def stage1(j, carry):
        r = f_ref[pl.ds(j, n1, stride=n2), :].astype(BF16)
        y1_ref[pl.ds(pl.multiple_of(j * Y1_PITCH, SUBLANES), 2 * n1), :] = _dot(l1_ref[j], r)
        return carry

    lax.fori_loop(0, n2, stage1, 0, unroll=FFT_UNROLL)

    l2 = l2_ref[...]

    def stage2(k1, carry):
        yr = y1_ref[pl.ds(k1, n2, stride=Y1_PITCH), :]
        yi = y1_ref[pl.ds(n1 + k1, n2, stride=Y1_PITCH), :]
        z = _dot(l2, jnp.concatenate([yr, yi], axis=0).astype(BF16))
        z_ref[0, pl.ds(k1, n2, stride=Z_PITCH), :] = z[:n2]
        z_ref[1, pl.ds(k1, n2, stride=Z_PITCH), :] = z[n2:]
        return carry

    lax.fori_loop(0, n1, stage2, 0, unroll=FFT_UNROLL)

    wf = (_dot_f32(cs_ref[...], w_ref[...]) * ortho).astype(BF16)
    wr, wi = wf[:FOUR_GC], wf[FOUR_GC:]

    def stage3(k2, carry):
        base = pl.multiple_of(k2 * Z_PITCH, SUBLANES)
        zr = z_ref[0, pl.ds(base, n1), :].astype(BF16)
        zi = z_ref[1, pl.ds(base, n1), :].astype(BF16)
        o_ref[pl.ds(pl.multiple_of(k2 * n1, n1), n1), :] = (_dot(zr, wr) + _dot(zi, wi)).astype(o_ref.dtype)
        return carry

    lax.fori_loop(0, n2, stage3, 0, unroll=FFT_UNROLL)


def _fourier_mix(f, w_four):
    b, n, _ = f.shape
    l1, l2, cs = _fft_tables(n)
    l1 = jnp.asarray(l1).astype(BF16)
    l2 = jnp.asarray(l2).astype(BF16)
    cs = jnp.asarray(cs)
    ortho = float(1.0 / np.sqrt(n * FOUR_GC))
    slab = lambda bi, g: (bi, 0, g)
    return pl.pallas_call(
        functools.partial(_fft_kernel, ortho=ortho),
        grid=(b, FOUR_GROUPS),
        in_specs=[
            pl.BlockSpec((None, n, FOUR_GC), slab),
            _resident(l1.shape, lambda bi, g: (0, 0, 0)),
            _resident(l2.shape, lambda bi, g: (0, 0)),
            _resident(cs.shape, lambda bi, g: (0, 0)),
            pl.BlockSpec((None, FOUR_GC, FOUR_GC), lambda bi, g: (g, 0, 0)),
        ],
        out_specs=pl.BlockSpec((None, n, FOUR_GC), slab),
        out_shape=jax.ShapeDtypeStruct((b, n, FOUR_WIDTH), BF16),
        scratch_shapes=[
            pltpu.VMEM((FFT_N2 * Y1_PITCH, FOUR_GC), F32),
            pltpu.VMEM((2, FFT_N2 * Z_PITCH, FOUR_GC), F32),
        ],
        compiler_params=_params("parallel", "parallel"),
        name="fourier_mix",
    )(f, l1, l2, cs, w_four)


def _band_start(g, rows):
    return jnp.clip(NA_Q_ROWS * g - (NA_BAND_ROWS - NA_Q_ROWS) // 2, 0, rows - NA_BAND_ROWS)


def _patch_col_start(cb):
    return int(np.clip(NA_Q_COLS * cb - (NA_K_COLS - NA_Q_COLS) // 2, 0, GRID_W - NA_K_COLS))


def _natten_bias_tables(rpb, rows):
    heads = rpb.shape[0]
    groups = rows // NA_Q_ROWS
    n_dr, n_dc = 2 * NA_WIN_ROWS - 1, 2 * NA_WIN_COLS - 1
    n_cb = GRID_W // NA_Q_COLS
    nq, nk = NA_Q_ROWS * NA_Q_COLS, NA_BAND_ROWS * NA_K_COLS
    lane_tiles = 128 // NA_K_COLS
    lead = NA_WIN_ROWS
    strip_tiles = 2 * NA_BAND_ROWS
    col = np.arange(GRID_W)
    col_start = np.clip(col - NA_WIN_COLS // 2, 0, GRID_W - NA_WIN_COLS)
    col_ok = (col[None, :] >= col_start[:, None]) & (col[None, :] < col_start[:, None] + NA_WIN_COLS)
    dc = np.clip(col[None, :] - col[:, None] + (NA_WIN_COLS - 1), 0, n_dc - 1)
    dr_of = np.arange(lane_tiles)[:, None] + np.arange(strip_tiles)[None, :] - lead
    pick_dr = (dr_of[..., None] == np.arange(n_dr)).astype(np.float32)
    pick_dc = np.zeros((n_cb, NA_Q_COLS, NA_K_COLS, n_dc), np.float32)
    pair_ok = np.zeros((n_cb, NA_Q_COLS, NA_K_COLS), bool)
    for cb in range(n_cb):
        q0, k0 = NA_Q_COLS * cb, _patch_col_start(cb)
        assert col_start[q0] >= k0 and col_start[q0 + NA_Q_COLS - 1] + NA_WIN_COLS <= k0 + NA_K_COLS
        pick_dc[cb] = dc[q0:q0 + NA_Q_COLS, k0:k0 + NA_K_COLS, None] == np.arange(n_dc)
        pair_ok[cb] = col_ok[q0:q0 + NA_Q_COLS, k0:k0 + NA_K_COLS]
    strips = jnp.einsum('rtd,hde,bqke->hbrqtk', jnp.asarray(pick_dr), rpb.astype(F32), jnp.asarray(pick_dc),
                        precision=lax.Precision.HIGHEST)
    valid = ((dr_of >= 0) & (dr_of < n_dr))[None, :, None, :, None] & pair_ok[:, None, :, None, :]
    strips = jnp.where(jnp.asarray(valid), strips * LOG2_E, MASK_VALUE)
    strips = strips.reshape(heads, n_cb, lane_tiles, NA_Q_COLS, strip_tiles * NA_K_COLS)

    plan = []
    for kind, g in enumerate((0, groups // 2, groups - 1)):
        band0 = int(np.clip(NA_Q_ROWS * g - (NA_BAND_ROWS - NA_Q_ROWS) // 2, 0, rows - NA_BAND_ROWS))
        for a in range(NA_Q_ROWS):
            qr = NA_Q_ROWS * g + a
            row_start = int(np.clip(qr - NA_WIN_ROWS // 2, 0, rows - NA_WIN_ROWS))
            assert band0 <= row_start and row_start + NA_WIN_ROWS <= band0 + NA_BAND_ROWS
            first_tile = band0 - qr + NA_WIN_ROWS - 1 + lead
            assert 0 <= first_tile and first_tile + NA_BAND_ROWS <= strip_tiles + lane_tiles - 1
            for cb in range(n_cb):
                for hd in range(2):
                    plan.append((kind * n_cb + cb, hd, cb, a, first_tile, row_start - band0))

    def build(strip_ref, o_ref):
        patch_row = lax.broadcasted_iota(jnp.int32, (1, nk), 1) // NA_K_COLS
        for idx, hd, cb, a, first_tile, first_row in plan:
            lane0 = (first_tile // lane_tiles) * 128
            val = strip_ref[hd, cb, first_tile % lane_tiles, :, lane0:lane0 + nk]
            in_window = (patch_row >= first_row) & (patch_row < first_row + NA_WIN_ROWS)
            r0 = hd * nq + a * NA_Q_COLS
            o_ref[idx, r0:r0 + NA_Q_COLS, :] = jnp.where(in_window, val, MASK_VALUE)

    return pl.pallas_call(
        build,
        grid=(heads // 2,),
        in_specs=[pl.BlockSpec((2,) + strips.shape[1:], lambda hp: (hp, 0, 0, 0, 0))],
        out_specs=pl.BlockSpec((None, 3 * n_cb, 2 * nq, nk), lambda hp: (hp, 0, 0, 0)),
        out_shape=jax.ShapeDtypeStruct((heads // 2, 3 * n_cb, 2 * nq, nk), F32),
        compiler_params=_params("parallel"),
        name="bias_table",
    )(strips)


def _natten_kernel(q_ref, k_ref, v_ref, kc_ref, vc_ref, tab_ref, o_ref, *, rows):
    nq = NA_Q_ROWS * NA_Q_COLS
    groups = rows // NA_Q_ROWS
    n_cb = GRID_W // NA_Q_COLS
    first_head = lax.broadcasted_iota(jnp.int32, (1, 2 * NA_HEAD_DIM), 1) < NA_HEAD_DIM
    kc, vc = kc_ref[...].astype(BF16), vc_ref[...].astype(BF16)
    chains = [(g, cb) for g in range(NA_GROUPS_PER_STEP) for cb in range(n_cb)]

    def group_index(g):
        return pl.program_id(2) * NA_GROUPS_PER_STEP + g

    def q_rows(g, cb, a):
        start = (g * NA_Q_ROWS + a) * GRID_W + cb * NA_Q_COLS
        return slice(start, start + NA_Q_COLS)

    def patch(ref, g, cb):
        tok0 = _band_start(group_index(g), rows) * GRID_W + _patch_col_start(cb)
        parts = [ref[pl.ds(pl.multiple_of(tok0 + bb * GRID_W, SUBLANES), NA_K_COLS), :] for bb in range(NA_BAND_ROWS)]
        return jnp.concatenate(parts, axis=0).astype(BF16)

    def scores(g, cb):
        gi = group_index(g)
        kind = jnp.where(gi == 0, 0, jnp.where(gi == groups - 1, 2, 1))
        q2 = jnp.concatenate([q_ref[q_rows(g, cb, a), :] for a in range(NA_Q_ROWS)], axis=0)
        zero = jnp.zeros_like(q2)
        qs = jnp.concatenate([jnp.where(first_head, q2, zero), jnp.where(first_head, zero, q2)], axis=0)
        return _dot_nt(qs, patch(k_ref, g, cb)) + tab_ref[kind * n_cb + cb], _dot_nt(qs, kc)

    def softmax(s_win, s_ctx):
        m = jnp.maximum(jnp.max(s_win, axis=-1, keepdims=True), jnp.max(s_ctx, axis=-1, keepdims=True))
        p_win = jnp.exp2(s_win - m)
        p_ctx = jnp.exp2(s_ctx - m)
        l = jnp.sum(p_win, axis=-1, keepdims=True) + jnp.sum(p_ctx, axis=-1, keepdims=True)
        return p_win.astype(BF16), p_ctx.astype(BF16), l

    def weighted_values(g, cb, p_win, p_ctx, l):
        o = (_dot(p_win, patch(v_ref, g, cb)) + _dot(p_ctx, vc)) / l
        o = jnp.where(first_head, o[:nq], o[nq:]).astype(o_ref.dtype)
        for a in range(NA_Q_ROWS):
            o_ref[q_rows(g, cb, a), :] = o[a * NA_Q_COLS:(a + 1) * NA_Q_COLS]

    s, p = {}, {}
    for step in range(len(chains) + 2):
        if step < len(chains):
            s[step] = scores(*chains[step])
        if 0 <= step - 1 < len(chains):
            p[step - 1] = softmax(*s.pop(step - 1))
        if 0 <= step - 2 < len(chains):
            weighted_values(*chains[step - 2], *p.pop(step - 2))


def _neighbourhood_attention(q, k, v, kc, vc, rpb):
    b, n, _ = q.shape
    rows = n // GRID_W
    steps = rows // (NA_Q_ROWS * NA_GROUPS_PER_STEP)
    ctx_len = kc.shape[1]
    tq = NA_GROUPS_PER_STEP * NA_Q_ROWS * GRID_W
    tables = _natten_bias_tables(rpb, rows)
    lanes = 2 * NA_HEAD_DIM
    seq = lambda bi, hp, i: (bi, 0, hp)
    return pl.pallas_call(
        functools.partial(_natten_kernel, rows=rows),
        grid=(b, NA_HEADS // 2, steps),
        in_specs=[
            pl.BlockSpec((None, tq, lanes), lambda bi, hp, i: (bi, i, hp)),
            pl.BlockSpec((None, n, lanes), seq),
            pl.BlockSpec((None, n, lanes), seq),
            pl.BlockSpec((None, ctx_len, lanes), seq),
            pl.BlockSpec((None, ctx_len, lanes), seq),
            pl.BlockSpec((None,) + tables.shape[1:], lambda bi, hp, i: (hp, 0, 0, 0)),
        ],
        out_specs=pl.BlockSpec((None, tq, lanes), lambda bi, hp, i: (bi, i, hp)),
        out_shape=jax.ShapeDtypeStruct((b, n, NA_WIDTH), BF16),
        compiler_params=_params("parallel", "parallel", "arbitrary"),
        name="neighbourhood_attention",
    )(q, k, v, kc, vc, tables)


def _mod_chunk(mod_ref, idx):
    return mod_ref[:, idx * D_MODEL:(idx + 1) * D_MODEL]


def _mlp_inputs(h, y, mod_ref, g2n_ref):
    h1 = h + _mod_chunk(mod_ref, 2) * y
    u = _norm_modulate(h1, g2n_ref[...], _mod_chunk(mod_ref, 3), _mod_chunk(mod_ref, 4)).astype(BF16)
    return h1, u


def _mlp_chunk(u, acc, c, w1_ref, w2_ref):
    lo, hi = c * MLP_CHUNK, (c + 1) * MLP_CHUNK
    a = jnp.maximum(_dot(u, w1_ref[:, lo:hi]), 0.0)
    part = _dot((a * a).astype(BF16), w2_ref[lo:hi, :])
    return part if acc is None else acc + part


def _pad_stages(stages, sizes=None):
    if sizes is None:
        sizes = [1] * (len(stages) - 1) + [0] * (MLP_CHUNKS - len(stages)) + [1]
    assert len(sizes) == MLP_CHUNKS and sum(sizes) == len(stages)

    def merged(group):
        def run():
            out = None
            for stage in group:
                out = stage()
            return out
        return run

    starts = [sum(sizes[:k]) for k in range(MLP_CHUNKS)]
    return [merged(stages[s0:s0 + n]) for s0, n in zip(starts, sizes)]


def _run_chains(prologues, mod_ref, w1_ref, w2_ref, o_ref, tc):
    n = len(prologues)
    for stage in prologues[0][:-1]:
        stage()
    h1, u = prologues[0][-1]()
    for i in range(n):
        acc, nxt = None, None
        for c in range(MLP_CHUNKS):
            acc = _mlp_chunk(u, acc, c, w1_ref, w2_ref)
            if i + 1 < n:
                nxt = prologues[i + 1][c]()
        o_ref[i * tc:(i + 1) * tc, :] = h1 + _mod_chunk(mod_ref, 5) * acc
        if i + 1 < n:
            h1, u = nxt


def _even_tail_kernel(h_ref, of_ref, on_ref, mod_ref, g2n_ref, wo_hbm, w1_hbm, w2_hbm, o_ref,
                      wo_ref, w1_ref, w2_ref, wo_stage, w1_stage, w2_stage, sem, *, layer):
    _load_weights_once([wo_hbm.at[layer // 2], w1_hbm.at[layer], w2_hbm.at[layer]], [wo_ref, w1_ref, w2_ref],
                       [wo_stage, w1_stage, w2_stage], sem)
    tc = h_ref.shape[0] // TOKEN_CHAINS

    def prologue(i):
        rows = slice(i * tc, (i + 1) * tc)
        st = {}

        def s0():
            st["y"] = _dot(of_ref[rows, :], wo_ref[:FOUR_WIDTH, :])

        def s1():
            st["y"] = st["y"] + _dot(on_ref[rows, :], wo_ref[FOUR_WIDTH:, :])

        def s3():
            return _mlp_inputs(h_ref[rows, :], st["y"], mod_ref, g2n_ref)

        return _pad_stages([s0, s1, s3])

    _run_chains([prologue(i) for i in range(TOKEN_CHAINS)], mod_ref, w1_ref, w2_ref, o_ref, tc)


def _even_tail(h, o_four, o_na, mod_rows, w_out, norm2_g, w1, w2, layer, tm):
    b, n, _ = h.shape
    tok = lambda bi, j: (bi, j, 0)
    const2 = lambda bi, j: (0, 0)
    return pl.pallas_call(
        functools.partial(_even_tail_kernel, layer=layer),
        grid=(b, n // tm),
        in_specs=[
            pl.BlockSpec((None, tm, D_MODEL), tok),
            pl.BlockSpec((None, tm, FOUR_WIDTH), tok),
            pl.BlockSpec((None, tm, NA_WIDTH), tok),
            pl.BlockSpec((None, 1, N_MOD * D_MODEL), lambda bi, j: (bi, 0, 0)),
            pl.BlockSpec((1, D_MODEL), const2),
            pl.BlockSpec(memory_space=pl.ANY),
            pl.BlockSpec(memory_space=pl.ANY),
            pl.BlockSpec(memory_space=pl.ANY),
        ],
        out_specs=pl.BlockSpec((None, tm, D_MODEL), tok),
        out_shape=jax.ShapeDtypeStruct(h.shape, F32),
        scratch_shapes=_weight_scratch([w_out.shape[1:], w1.shape[1:], w2.shape[1:]]),
        compiler_params=_params("arbitrary", "arbitrary"),
        name="even_tail",
    )(h, o_four, o_na, mod_rows, norm2_g, w_out, w1, w2)


HALO = 8


def _shift_rows(a, s):
    return pltpu.roll(a, s % a.shape[0], 0)


def _pool_kernel(hp_ref, h_ref, hn_ref, mod_ref, g1n_ref, ps_ref, g2n_ref, wp_hbm, w1_hbm, w2_hbm, o_ref,
                 wp_ref, w1_ref, w2_ref, wp_stage, w1_stage, w2_stage, sem, *, seq_len, layer):
    _load_weights_once([wp_hbm.at[layer // 2], w1_hbm.at[layer], w2_hbm.at[layer]], [wp_ref, w1_ref, w2_ref],
                       [wp_stage, w1_stage, w2_stage], sem)
    j = pl.program_id(1)
    tm = h_ref.shape[0]
    tc = tm // TOKEN_CHAINS
    gc = POOL_GC
    gain, shift, scale = g1n_ref[...], _mod_chunk(mod_ref, 0), _mod_chunk(mod_ref, 1)
    unorm = lambda x: _norm_modulate(x, gain, shift, scale)

    def prologue(i):
        r0 = i * tc
        st = {}

        def normed():
            st["u"] = unorm(h_ref[r0:r0 + tc, :])

        def with_halo():
            if i == 0:
                u_prev = jnp.where(j > 0, unorm(hp_ref[...]), 0.0)
            else:
                u_prev = unorm(h_ref[r0 - HALO:r0, :])
            if i == TOKEN_CHAINS - 1:
                u_next = jnp.where(j < pl.num_programs(1) - 1, unorm(hn_ref[...]), 0.0)
            else:
                u_next = unorm(h_ref[r0 + tc:r0 + tc + HALO, :])
            st["ue"] = jnp.concatenate([u_prev, st["u"], u_next], axis=0)

        def sum2():
            st["s2"] = st["ue"] + _shift_rows(st["ue"], 1)

        def sum4():
            s2 = st["s2"][:, gc:]
            st["s4"] = _shift_rows(s2, 1) + _shift_rows(s2, -1)

        def sum8_16():
            s4 = st["s4"][:, gc:]
            s8 = _shift_rows(s4, 2) + _shift_rows(s4, -2)
            s16 = _shift_rows(s8[:, gc:], 4) + _shift_rows(s8[:, gc:], -4)
            st["sums"] = (st["s2"][:, :gc], st["s4"][:, :gc], s8[:, :gc], s16)

        def group_linear(groups):
            def run():
                t = j * tm + r0 + lax.broadcasted_iota(jnp.int32, (tc, 1), 0)
                for gi in groups:
                    w = POOL_SIZES[gi]
                    lo = jnp.maximum(t - w // 2, 0)
                    hi = jnp.minimum(t + w - w // 2, seq_len)
                    inv_cnt = 1.0 / (hi - lo).astype(F32)
                    pooled = st["sums"][gi][HALO:HALO + tc] * inv_cnt - st["u"][:, gi * gc:(gi + 1) * gc]
                    st["y%d" % gi] = _dot(pooled.astype(BF16), wp_ref[gi * gc:(gi + 1) * gc, :]) * ps_ref[:, gi * gc:(gi + 1) * gc]
            return run

        def mlp_inputs():
            y = jnp.concatenate([st["y%d" % gi] for gi in range(len(POOL_SIZES))], axis=1)
            return _mlp_inputs(h_ref[r0:r0 + tc, :], y, mod_ref, g2n_ref)

        return _pad_stages([normed, with_halo, sum2, sum4, sum8_16, group_linear((0, 1)), group_linear((2, 3)),
                            mlp_inputs], sizes=POOL_STAGE_SIZES)

    _run_chains([prologue(i) for i in range(TOKEN_CHAINS)], mod_ref, w1_ref, w2_ref, o_ref, tc)


def _odd_layer(h, mod_rows, norm1_g, w_pool, pool_scale, norm2_g, w1, w2, layer, tm):
    b, n, _ = h.shape
    tok = lambda bi, j: (bi, j, 0)
    const2 = lambda bi, j: (0, 0)
    per_tile = tm // HALO
    last = n // HALO - 1
    return pl.pallas_call(
        functools.partial(_pool_kernel, seq_len=n, layer=layer),
        grid=(b, n // tm),
        in_specs=[
            pl.BlockSpec((None, HALO, D_MODEL), lambda bi, j: (bi, jnp.maximum(j * per_tile - 1, 0), 0)),
            pl.BlockSpec((None, tm, D_MODEL), tok),
            pl.BlockSpec((None, HALO, D_MODEL), lambda bi, j: (bi, jnp.minimum((j + 1) * per_tile, last), 0)),
            pl.BlockSpec((None, 1, N_MOD * D_MODEL), lambda bi, j: (bi, 0, 0)),
            pl.BlockSpec((1, D_MODEL), const2),
            pl.BlockSpec((1, D_MODEL), const2),
            pl.BlockSpec((1, D_MODEL), const2),
            pl.BlockSpec(memory_space=pl.ANY),
            pl.BlockSpec(memory_space=pl.ANY),
            pl.BlockSpec(memory_space=pl.ANY),
        ],
        out_specs=pl.BlockSpec((None, tm, D_MODEL), tok),
        out_shape=jax.ShapeDtypeStruct(h.shape, F32),
        scratch_shapes=_weight_scratch([w_pool.shape[1:], w1.shape[1:], w2.shape[1:]]),
        compiler_params=_params("arbitrary", "arbitrary"),
        name="odd_layer",
    )(h, h, h, mod_rows, norm1_g, pool_scale, norm2_g, w_pool, w1, w2)


def _head_segments():
    lanes = 256
    seg = (np.arange(lanes)[:, None] // NA_HEAD_DIM) == (np.arange(lanes)[None, :] // NA_HEAD_DIM)
    return jnp.asarray(seg.astype(np.float32)).astype(BF16)


def kernel(x, c, ctx, c_ctx, w_mod, b_mod, norm1_g, norm2_g, w_in_even, w_four, q_norm_g, k_norm_g, rpb, w_out_even, w_pool, pool_scale, w_mlp1, w_mlp2):
    b = x.shape[0]
    depth = w_mod.shape[0]
    assert depth == 2, "layer 0 is the even (Fourier + attention) layer, layer 1 the odd (pooling) layer"
    ctx_row = b
    pad = (-(b + 1)) % SUBLANES
    c_rows = jnp.concatenate([c, c_ctx[None, :], jnp.zeros((pad, D_MODEL), F32)], axis=0)
    mod = _modulation(c_rows, w_mod, b_mod)
    mod = mod.reshape(depth, mod.shape[1], 1, N_MOD * D_MODEL)

    row = lambda a: a.reshape(1, -1)
    heads = lambda g: jnp.tile(g, NA_HEADS).reshape(1, NA_WIDTH)
    seg = _head_segments()

    w_in = w_in_even[0].astype(BF16)
    qg, kg = heads(q_norm_g[0]), heads(k_norm_g[0])
    f, q, k, v = _in_projection(x, mod[0], lambda bi: bi, row(norm1_g[0]), w_in, qg, kg, seg, tm=1024)
    _, _, kc, vc = _in_projection(ctx, mod[0], lambda bi: ctx_row, row(norm1_g[0]), w_in, qg, kg, seg,
                                  tm=ctx.shape[1])
    o_four = _fourier_mix(f, w_four[0])
    o_na = _neighbourhood_attention(q, k, v, kc, vc, rpb[0])
    h = _even_tail(x, o_four, o_na, mod[0], w_out_even, row(norm2_g[0]), w_mlp1, w_mlp2, layer=0, tm=1024)

    w_pool_rows = w_pool.reshape(w_pool.shape[0], len(POOL_SIZES) * POOL_GC, POOL_GC)
    h = _odd_layer(h, mod[1], row(norm1_g[1]), w_pool_rows, row(pool_scale[0]), row(norm2_g[1]), w_mlp1, w_mlp2,
                   layer=1, tm=1024)
    return h
```

```python
import functools

import numpy as np
import jax
import jax.numpy as jnp
from jax import lax
from jax.experimental import pallas as pl
from jax.experimental.pallas import tpu as pltpu

D_MODEL = 1024
GRID_W = 64
FOUR_GROUPS = 4
FOUR_GC = 128
FOUR_WIDTH = 512
NA_HEADS = 8
NA_HEAD_DIM = 64
NA_WIDTH = 512
NA_WIN_ROWS = 8
NA_WIN_COLS = 16
POOL_SIZES = (2, 4, 8, 16)
POOL_GC = 256
MLP_HIDDEN = 4096
N_MOD = 6
RMS_EPS = 1e-6

VMEM_LIMIT_BYTES = 56 * 1024 * 1024
SUBLANES = 8
WEIGHT_STAGE_BYTES = 1024 * 1024
WEIGHT_STAGE_SLOTS = 4

F32 = jnp.float32
BF16 = jnp.bfloat16
MASK_VALUE = -1e30
LOG2_E = 1.4426950408889634

FFT_N1 = 128
FFT_N2 = 64
Y1_PITCH = 2 * FFT_N1 + 8
Z_PITCH = FFT_N1 + 8
FFT_UNROLL = 32

NA_Q_ROWS = 8
NA_BAND_ROWS = NA_Q_ROWS + NA_WIN_ROWS
NA_Q_COLS = 16
NA_K_COLS = NA_Q_COLS + NA_WIN_COLS
NA_GROUPS_PER_STEP = 4
TOKEN_CHAINS = 4
INPROJ_CHAIN_ROWS = 256
MLP_CHUNK = 1024
MLP_CHUNKS = MLP_HIDDEN // MLP_CHUNK
POOL_STAGE_SIZES = (2, 2, 2, 2)


def _dot(a, b):
    return jnp.dot(a, b, preferred_element_type=F32)


def _dot_f32(a, b):
    return jnp.dot(a, b, preferred_element_type=F32, precision=lax.Precision.HIGHEST)


def _dot_nt(a, b):
    return lax.dot_general(a, b, (((1,), (1,)), ((), ())), preferred_element_type=F32)


def _params(*sem):
    return pltpu.CompilerParams(dimension_semantics=sem, vmem_limit_bytes=VMEM_LIMIT_BYTES)


def _resident(shape, index_map):
    return pl.BlockSpec(shape, index_map, pipeline_mode=pl.Buffered(1))


def _stage_rows(shape):
    k, n = shape
    rows = min(k, max(2 * SUBLANES, WEIGHT_STAGE_BYTES // (4 * n)))
    assert k % rows == 0
    return rows


def _weight_scratch(shapes):
    scratch = [pltpu.VMEM(shape, BF16) for shape in shapes]
    for k, n in shapes:
        rows = _stage_rows((k, n))
        scratch.append(pltpu.VMEM((min(WEIGHT_STAGE_SLOTS, k // rows), rows, n), F32))
    return scratch + [pltpu.SemaphoreType.DMA((WEIGHT_STAGE_SLOTS * len(shapes),))]


def _load_weights_once(sources, weights, stages, sem):
    @pl.when((pl.program_id(0) == 0) & (pl.program_id(1) == 0))
    def _():
        for i, (src, dst, stage) in enumerate(zip(sources, weights, stages)):
            slots, rows = stage.shape[0], stage.shape[1]
            chunks = dst.shape[0] // rows
            ahead = max(slots - 1, 1)

            def copy(c):
                return pltpu.make_async_copy(src.at[pl.ds(c * rows, rows), :], stage.at[c % slots],
                                             sem.at[WEIGHT_STAGE_SLOTS * i + c % slots])

            for c in range(min(ahead, chunks)):
                copy(c).start()
            for c in range(chunks):
                copy(c).wait()
                dst[c * rows:(c + 1) * rows, :] = stage[c % slots].astype(BF16)
                if c + ahead < chunks:
                    copy(c + ahead).start()


def _mod_kernel(c_ref, w_ref, b_ref, o_ref):
    c = c_ref[...]
    s = c * (1.0 / (1.0 + jnp.exp(-c)))
    s_hi = s.astype(BF16)
    s_lo = (s - s_hi.astype(F32)).astype(BF16)
    w = w_ref[...].astype(BF16)
    o_ref[...] = _dot(s_hi, w) + _dot(s_lo, w) + b_ref[...]


def _modulation(c_rows, w_mod, b_mod):
    depth = w_mod.shape[0]
    rows = c_rows.shape[0]
    tn = 2 * D_MODEL
    return pl.pallas_call(
        _mod_kernel,
        grid=(depth, N_MOD * D_MODEL // tn),
        in_specs=[
            pl.BlockSpec((rows, D_MODEL), lambda l, j: (0, 0)),
            pl.BlockSpec((None, D_MODEL, tn), lambda l, j: (l, 0, j)),
            pl.BlockSpec((None, 1, tn), lambda l, j: (l, 0, j)),
        ],
        out_specs=pl.BlockSpec((None, rows, tn), lambda l, j: (l, 0, j)),
        out_shape=jax.ShapeDtypeStruct((depth, rows, N_MOD * D_MODEL), F32),
        compiler_params=_params("parallel", "parallel"),
        name="modulation",
    )(c_rows, w_mod, b_mod.reshape(depth, 1, N_MOD * D_MODEL))


def _norm_modulate(x, gain, shift, scale):
    ms = jnp.mean(x * x, axis=-1, keepdims=True)
    return (x * lax.rsqrt(ms + RMS_EPS)) * (gain * (1.0 + scale)) + shift


def _head_norm(a, gain, seg_ref):
    sq = (a * a).astype(BF16)
    half = seg_ref.shape[0]
    ms = jnp.concatenate([_dot(sq[:, :half], seg_ref[...]), _dot(sq[:, half:], seg_ref[...])], axis=1)
    return a * lax.rsqrt(ms * (1.0 / NA_HEAD_DIM) + RMS_EPS) * gain


def _inproj_kernel(x_ref, mod_ref, g_ref, w_ref, qg_ref, kg_ref, seg_ref, f_ref, q_ref, k_ref, v_ref):
    tm = x_ref.shape[0]
    chains = max(1, tm // INPROJ_CHAIN_ROWS)
    tc = tm // chains

    def normed(i):
        x = x_ref[i * tc:(i + 1) * tc, :]
        return _norm_modulate(x, g_ref[...], mod_ref[:, 0:D_MODEL], mod_ref[:, D_MODEL:2 * D_MODEL]).astype(BF16)

    def outputs(i, p):
        rows = slice(i * tc, (i + 1) * tc)
        f_ref[rows, :] = p[:, :FOUR_WIDTH]
        q = p[:, FOUR_WIDTH:FOUR_WIDTH + NA_WIDTH]
        k = p[:, FOUR_WIDTH + NA_WIDTH:FOUR_WIDTH + 2 * NA_WIDTH]
        q_ref[rows, :] = (_head_norm(q, qg_ref[...], seg_ref) * (NA_HEAD_DIM ** -0.5 * LOG2_E)).astype(BF16)
        k_ref[rows, :] = _head_norm(k, kg_ref[...], seg_ref)
        v_ref[rows, :] = p[:, FOUR_WIDTH + 2 * NA_WIDTH:]

    u, p = {}, {}
    for step in range(chains + 2):
        if step < chains:
            u[step] = normed(step)
        if 0 <= step - 1 < chains:
            p[step - 1] = _dot(u.pop(step - 1), w_ref[...])
        if 0 <= step - 2 < chains:
            outputs(step - 2, p.pop(step - 2))


def _in_projection(x, mod_rows, row_of_batch, norm_g, w_in, q_g, k_g, seg, tm):
    b, n, _ = x.shape
    width = w_in.shape[1]
    tok = lambda bi, j: (bi, j, 0)
    const2 = lambda bi, j: (0, 0)
    out_tok = lambda w: pl.BlockSpec((None, tm, w), tok)
    return pl.pallas_call(
        _inproj_kernel,
        grid=(b, n // tm),
        in_specs=[
            pl.BlockSpec((None, tm, D_MODEL), tok),
            pl.BlockSpec((None, 1, N_MOD * D_MODEL), lambda bi, j: (row_of_batch(bi), 0, 0)),
            pl.BlockSpec((1, D_MODEL), const2),
            _resident((D_MODEL, width), const2),
            pl.BlockSpec((1, NA_WIDTH), const2),
            pl.BlockSpec((1, NA_WIDTH), const2),
            pl.BlockSpec(seg.shape, const2),
        ],
        out_specs=[out_tok(FOUR_WIDTH), out_tok(NA_WIDTH), out_tok(NA_WIDTH), out_tok(NA_WIDTH)],
        out_shape=[
            jax.ShapeDtypeStruct((b, n, FOUR_WIDTH), F32),
            jax.ShapeDtypeStruct((b, n, NA_WIDTH), BF16),
            jax.ShapeDtypeStruct((b, n, NA_WIDTH), F32),
            jax.ShapeDtypeStruct((b, n, NA_WIDTH), F32),
        ],
        compiler_params=_params("parallel", "parallel"),
        name="in_projection",
    )(x, mod_rows, norm_g, w_in, q_g, k_g, seg)


def _fft_tables(n):
    n1, n2 = FFT_N1, FFT_N2
    assert n1 * n2 == n
    k1 = np.arange(n1)[None, :, None]
    pos = (n2 * np.arange(n1)[None, None, :] + np.arange(n2)[:, None, None])
    ang = 2.0 * np.pi * ((k1 * pos) % n) / n
    l1 = np.concatenate([np.cos(ang), -np.sin(ang)], axis=1)
    a2 = 2.0 * np.pi * ((np.arange(n2)[:, None] * np.arange(n2)[None, :]) % n2) / n2
    c2, s2 = np.cos(a2), np.sin(a2)
    l2 = np.block([[c2, s2], [-s2, c2]])
    ac = 2.0 * np.pi * ((np.arange(FOUR_GC)[:, None] * np.arange(FOUR_GC)[None, :]) % FOUR_GC) / FOUR_GC
    cs = np.concatenate([np.cos(ac), np.sin(ac)], axis=0)
    return l1.astype(np.float32), l2.astype(np.float32), cs.astype(np.float32)


def _fft_kernel(f_ref, l1_ref, l2_ref, cs_ref, w_ref, o_ref, y1_ref, z_ref, *, ortho):
    n1, n2 = FFT_N1, FFT_N2

    def stage1(j, carry):
        r = f_ref[pl.ds(j, n1, stride=n2), :].astype(BF16)
        y1_ref[pl.ds(pl.multiple_of(j * Y1_PITCH, SUBLANES), 2 * n1), :] = _dot(l1_ref[j], r)
        return carry

    lax.fori_loop(0, n2, stage1, 0, unroll=FFT_UNROLL)

    l2 = l2_ref[...]

    def stage2(k1, carry):
        yr = y1_ref[pl.ds(k1, n2, stride=Y1_PITCH), :]
        yi = y1_ref[pl.ds(n1 + k1, n2, stride=Y1_PITCH), :]
        z = _dot(l2, jnp.concatenate([yr, yi], axis=0).astype(BF16))
        z_ref[0, pl.ds(k1, n2, stride=Z_PITCH), :] = z[:n2]
        z_ref[1, pl.ds(k1, n2, stride=Z_PITCH), :] = z[n2:]
        return carry

    lax.fori_loop(0, n1, stage2, 0, unroll=FFT_UNROLL)

    wf = (_dot_f32(cs_ref[...], w_ref[...]) * ortho).astype(BF16)
    wr, wi = wf[:FOUR_GC], wf[FOUR_GC:]

    def stage3(k2, carry):
        base = pl.multiple_of(k2 * Z_PITCH, SUBLANES)
        zr = z_ref[0, pl.ds(base, n1), :].astype(BF16)
        zi = z_ref[1, pl.ds(base, n1), :].astype(BF16)
        o_ref[pl.ds(pl.multiple_of(k2 * n1, n1), n1), :] = (_dot(zr, wr) + _dot(zi, wi)).astype(o_ref.dtype)
        return carry

    lax.fori_loop(0, n2, stage3, 0, unroll=FFT_UNROLL)


def _fourier_mix(f, w_four):
    b, n, _ = f.shape
    l1, l2, cs = _fft_tables(n)
    l1 = jnp.asarray(l1).astype(BF16)
    l2 = jnp.asarray(l2).astype(BF16)
    cs = jnp.asarray(cs)
    ortho = float(1.0 / np.sqrt(n * FOUR_GC))
    slab = lambda bi, g: (bi, 0, g)
    return pl.pallas_call(
        functools.partial(_fft_kernel, ortho=ortho),
        grid=(b, FOUR_GROUPS),
        in_specs=[
            pl.BlockSpec((None, n, FOUR_GC), slab),
            _resident(l1.shape, lambda bi, g: (0, 0, 0)),
            _resident(l2.shape, lambda bi, g: (0, 0)),
            _resident(cs.shape, lambda bi, g: (0, 0)),
            pl.BlockSpec((None, FOUR_GC, FOUR_GC), lambda bi, g: (g, 0, 0)),
        ],
        out_specs=pl.BlockSpec((None, n, FOUR_GC), slab),
        out_shape=jax.ShapeDtypeStruct((b, n, FOUR_WIDTH), BF16),
        scratch_shapes=[
            pltpu.VMEM((FFT_N2 * Y1_PITCH, FOUR_GC), F32),
            pltpu.VMEM((2, FFT_N2 * Z_PITCH, FOUR_GC), F32),
        ],
        compiler_params=_params("parallel", "parallel"),
        name="fourier_mix",
    )(f, l1, l2, cs, w_four)


def _band_start(g, rows):
    return jnp.clip(NA_Q_ROWS * g - (NA_BAND_ROWS - NA_Q_ROWS) // 2, 0, rows - NA_BAND_ROWS)


def _patch_col_start(cb):
    return int(np.clip(NA_Q_COLS * cb - (NA_K_COLS - NA_Q_COLS) // 2, 0, GRID_W - NA_K_COLS))


def _natten_bias_tables(rpb, rows):
    heads = rpb.shape[0]
    groups = rows // NA_Q_ROWS
    n_dr, n_dc = 2 * NA_WIN_ROWS - 1, 2 * NA_WIN_COLS - 1
    n_cb = GRID_W // NA_Q_COLS
    nq, nk = NA_Q_ROWS * NA_Q_COLS, NA_BAND_ROWS * NA_K_COLS
    lane_tiles = 128 // NA_K_COLS
    lead = NA_WIN_ROWS
    strip_tiles = 2 * NA_BAND_ROWS
    col = np.arange(GRID_W)
    col_start = np.clip(col - NA_WIN_COLS // 2, 0, GRID_W - NA_WIN_COLS)
    col_ok = (col[None, :] >= col_start[:, None]) & (col[None, :] < col_start[:, None] + NA_WIN_COLS)
    dc = np.clip(col[None, :] - col[:, None] + (NA_WIN_COLS - 1), 0, n_dc - 1)
    onehot_dc = (dc[None] == np.arange(n_dc)[:, None, None]).astype(np.float32)
    tiles = jnp.einsum('hre,eqk->hrqk', rpb.astype(F32), jnp.asarray(onehot_dc), precision=lax.Precision.HIGHEST)
    tiles = jnp.where(jnp.asarray(col_ok), tiles * LOG2_E, MASK_VALUE)
    total_tiles = -(-(lead + n_dr + NA_BAND_ROWS) // lane_tiles) * lane_tiles
    tail = total_tiles - lead - n_dr
    tiles = jnp.pad(tiles, ((0, 0), (lead, tail), (0, 0), (0, 0)), constant_values=MASK_VALUE)
    strips = []
    for cb in range(n_cb):
        q0, k0 = NA_Q_COLS * cb, _patch_col_start(cb)
        assert col_start[q0] >= k0 and col_start[q0 + NA_Q_COLS - 1] + NA_WIN_COLS <= k0 + NA_K_COLS
        sub = tiles[:, :, q0:q0 + NA_Q_COLS, k0:k0 + NA_K_COLS].transpose(0, 2, 1, 3)
        strips.append(sub.reshape(heads, NA_Q_COLS, total_tiles * NA_K_COLS))
    strips = jnp.stack(strips, axis=1)

    plan = []
    for kind, g in enumerate((0, groups // 2, groups - 1)):
        band0 = int(np.clip(NA_Q_ROWS * g - (NA_BAND_ROWS - NA_Q_ROWS) // 2, 0, rows - NA_BAND_ROWS))
        for a in range(NA_Q_ROWS):
            qr = NA_Q_ROWS * g + a
            row_start = int(np.clip(qr - NA_WIN_ROWS // 2, 0, rows - NA_WIN_ROWS))
            assert band0 <= row_start and row_start + NA_WIN_ROWS <= band0 + NA_BAND_ROWS
            first_tile = band0 - qr + NA_WIN_ROWS - 1 + lead
            assert 0 <= first_tile and first_tile + NA_BAND_ROWS <= total_tiles
            for cb in range(n_cb):
                for hd in range(2):
                    plan.append((kind * n_cb + cb, hd, cb, a, first_tile, row_start - band0))

    def build(strip_ref, o_ref):
        patch_row = lax.broadcasted_iota(jnp.int32, (1, nk), 1) // NA_K_COLS
        for idx, hd, cb, a, first_tile, first_row in plan:
            lane0 = first_tile * NA_K_COLS
            val = strip_ref[hd, cb, :, lane0:lane0 + nk]
            in_window = (patch_row >= first_row) & (patch_row < first_row + NA_WIN_ROWS)
            r0 = hd * nq + a * NA_Q_COLS
            o_ref[idx, r0:r0 + NA_Q_COLS, :] = jnp.where(in_window, val, MASK_VALUE)

    return pl.pallas_call(
        build,
        grid=(heads // 2,),
        in_specs=[pl.BlockSpec((2,) + strips.shape[1:], lambda hp: (hp, 0, 0, 0))],
        out_specs=pl.BlockSpec((None, 3 * n_cb, 2 * nq, nk), lambda hp: (hp, 0, 0, 0)),
        out_shape=jax.ShapeDtypeStruct((heads // 2, 3 * n_cb, 2 * nq, nk), F32),
        compiler_params=_params("parallel"),
        name="bias_table",
    )(strips)


def _natten_kernel(q_ref, k_ref, v_ref, kc_ref, vc_ref, tab_ref, o_ref, *, rows):
    nq = NA_Q_ROWS * NA_Q_COLS
    groups = rows // NA_Q_ROWS
    n_cb = GRID_W // NA_Q_COLS
    first_head = lax.broadcasted_iota(jnp.int32, (1, 2 * NA_HEAD_DIM), 1) < NA_HEAD_DIM
    kc, vc = kc_ref[...].astype(BF16), vc_ref[...].astype(BF16)
    chains = [(g, cb) for g in range(NA_GROUPS_PER_STEP) for cb in range(n_cb)]

    def group_index(g):
        return pl.program_id(2) * NA_GROUPS_PER_STEP + g

    def q_rows(g, cb, a):
        start = (g * NA_Q_ROWS + a) * GRID_W + cb * NA_Q_COLS
        return slice(start, start + NA_Q_COLS)

    def patch(ref, g, cb):
        tok0 = _band_start(group_index(g), rows) * GRID_W + _patch_col_start(cb)
        parts = [ref[pl.ds(pl.multiple_of(tok0 + bb * GRID_W, SUBLANES), NA_K_COLS), :] for bb in range(NA_BAND_ROWS)]
        return jnp.concatenate(parts, axis=0).astype(BF16)

    def scores(g, cb):
        gi = group_index(g)
        kind = jnp.where(gi == 0, 0, jnp.where(gi == groups - 1, 2, 1))
        q2 = jnp.concatenate([q_ref[q_rows(g, cb, a), :] for a in range(NA_Q_ROWS)], axis=0)
        zero = jnp.zeros_like(q2)
        qs = jnp.concatenate([jnp.where(first_head, q2, zero), jnp.where(first_head, zero, q2)], axis=0)
        return _dot_nt(qs, patch(k_ref, g, cb)) + tab_ref[kind * n_cb + cb], _dot_nt(qs, kc)

    def softmax(s_win, s_ctx):
        m = jnp.maximum(jnp.max(s_win, axis=-1, keepdims=True), jnp.max(s_ctx, axis=-1, keepdims=True))
        p_win = jnp.exp2(s_win - m)
        p_ctx = jnp.exp2(s_ctx - m)
        l = jnp.sum(p_win, axis=-1, keepdims=True) + jnp.sum(p_ctx, axis=-1, keepdims=True)
        return p_win.astype(BF16), p_ctx.astype(BF16), l

    def weighted_values(g, cb, p_win, p_ctx, l):
        o = (_dot(p_win, patch(v_ref, g, cb)) + _dot(p_ctx, vc)) / l
        o = jnp.where(first_head, o[:nq], o[nq:]).astype(o_ref.dtype)
        for a in range(NA_Q_ROWS):
            o_ref[q_rows(g, cb, a), :] = o[a * NA_Q_COLS:(a + 1) * NA_Q_COLS]

    s, p = {}, {}
    for step in range(len(chains) + 2):
        if step < len(chains):
            s[step] = scores(*chains[step])
        if 0 <= step - 1 < len(chains):
            p[step - 1] = softmax(*s.pop(step - 1))
        if 0 <= step - 2 < len(chains):
            weighted_values(*chains[step - 2], *p.pop(step - 2))


def _neighbourhood_attention(q, k, v, kc, vc, rpb):
    b, n, _ = q.shape
    rows = n // GRID_W
    steps = rows // (NA_Q_ROWS * NA_GROUPS_PER_STEP)
    ctx_len = kc.shape[1]
    tq = NA_GROUPS_PER_STEP * NA_Q_ROWS * GRID_W
    tables = _natten_bias_tables(rpb, rows)
    lanes = 2 * NA_HEAD_DIM
    seq = lambda bi, hp, i: (bi, 0, hp)
    return pl.pallas_call(
        functools.partial(_natten_kernel, rows=rows),
        grid=(b, NA_HEADS // 2, steps),
        in_specs=[
            pl.BlockSpec((None, tq, lanes), lambda bi, hp, i: (bi, i, hp)),
            pl.BlockSpec((None, n, lanes), seq),
            pl.BlockSpec((None, n, lanes), seq),
            pl.BlockSpec((None, ctx_len, lanes), seq),
            pl.BlockSpec((None, ctx_len, lanes), seq),
            pl.BlockSpec((None,) + tables.shape[1:], lambda bi, hp, i: (hp, 0, 0, 0)),
        ],
        out_specs=pl.BlockSpec((None, tq, lanes), lambda bi, hp, i: (bi, i, hp)),
        out_shape=jax.ShapeDtypeStruct((b, n, NA_WIDTH), BF16),
        compiler_params=_params("parallel", "parallel", "arbitrary"),
        name="neighbourhood_attention",
    )(q, k, v, kc, vc, tables)


def _mod_chunk(mod_ref, idx):
    return mod_ref[:, idx * D_MODEL:(idx + 1) * D_MODEL]


def _mlp_inputs(h, y, mod_ref, g2n_ref):
    h1 = h + _mod_chunk(mod_ref, 2) * y
    u = _norm_modulate(h1, g2n_ref[...], _mod_chunk(mod_ref, 3), _mod_chunk(mod_ref, 4)).astype(BF16)
    return h1, u


def _mlp_chunk(u, acc, c, w1_ref, w2_ref):
    lo, hi = c * MLP_CHUNK, (c + 1) * MLP_CHUNK
    a = jnp.maximum(_dot(u, w1_ref[:, lo:hi]), 0.0)
    part = _dot((a * a).astype(BF16), w2_ref[lo:hi, :])
    return part if acc is None else acc + part


def _pad_stages(stages, sizes=None):
    if sizes is None:
        sizes = [1] * (len(stages) - 1) + [0] * (MLP_CHUNKS - len(stages)) + [1]
    assert len(sizes) == MLP_CHUNKS and sum(sizes) == len(stages)

    def merged(group):
        def run():
            out = None
            for stage in group:
                out = stage()
            return out
        return run

    starts = [sum(sizes[:k]) for k in range(MLP_CHUNKS)]
    return [merged(stages[s0:s0 + n]) for s0, n in zip(starts, sizes)]


def _run_chains(prologues, mod_ref, w1_ref, w2_ref, o_ref, tc):
    n = len(prologues)
    for stage in prologues[0][:-1]:
        stage()
    h1, u = prologues[0][-1]()
    for i in range(n):
        acc, nxt = None, None
        for c in range(MLP_CHUNKS):
            acc = _mlp_chunk(u, acc, c, w1_ref, w2_ref)
            if i + 1 < n:
                nxt = prologues[i + 1][c]()
        o_ref[i * tc:(i + 1) * tc, :] = h1 + _mod_chunk(mod_ref, 5) * acc
        if i + 1 < n:
            h1, u = nxt


def _even_tail_kernel(h_ref, of_ref, on_ref, mod_ref, g2n_ref, wo_hbm, w1_hbm, w2_hbm, o_ref,
                      wo_ref, w1_ref, w2_ref, wo_stage, w1_stage, w2_stage, sem, *, layer):
    _load_weights_once([wo_hbm.at[layer // 2], w1_hbm.at[layer], w2_hbm.at[layer]], [wo_ref, w1_ref, w2_ref],
                       [wo_stage, w1_stage, w2_stage], sem)
    tc = h_ref.shape[0] // TOKEN_CHAINS

    def prologue(i):
        rows = slice(i * tc, (i + 1) * tc)
        st = {}

        def s0():
            st["y"] = _dot(of_ref[rows, :], wo_ref[:FOUR_WIDTH, :])

        def s1():
            st["y"] = st["y"] + _dot(on_ref[rows, :], wo_ref[FOUR_WIDTH:, :])

        def s3():
            return _mlp_inputs(h_ref[rows, :], st["y"], mod_ref, g2n_ref)

        return _pad_stages([s0, s1, s3])

    _run_chains([prologue(i) for i in range(TOKEN_CHAINS)], mod_ref, w1_ref, w2_ref, o_ref, tc)


def _even_tail(h, o_four, o_na, mod_rows, w_out, norm2_g, w1, w2, layer, tm):
    b, n, _ = h.shape
    tok = lambda bi, j: (bi, j, 0)
    const2 = lambda bi, j: (0, 0)
    return pl.pallas_call(
        functools.partial(_even_tail_kernel, layer=layer),
        grid=(b, n // tm),
        in_specs=[
            pl.BlockSpec((None, tm, D_MODEL), tok),
            pl.BlockSpec((None, tm, FOUR_WIDTH), tok),
            pl.BlockSpec((None, tm, NA_WIDTH), tok),
            pl.BlockSpec((None, 1, N_MOD * D_MODEL), lambda bi, j: (bi, 0, 0)),
            pl.BlockSpec((1, D_MODEL), const2),
            pl.BlockSpec(memory_space=pl.ANY),
            pl.BlockSpec(memory_space=pl.ANY),
            pl.BlockSpec(memory_space=pl.ANY),
        ],
        out_specs=pl.BlockSpec((None, tm, D_MODEL), tok),
        out_shape=jax.ShapeDtypeStruct(h.shape, F32),
        scratch_shapes=_weight_scratch([w_out.shape[1:], w1.shape[1:], w2.shape[1:]]),
        compiler_params=_params("arbitrary", "arbitrary"),
        name="even_tail",
    )(h, o_four, o_na, mod_rows, norm2_g, w_out, w1, w2)


HALO = 8


def _shift_rows(a, s):
    return pltpu.roll(a, s % a.shape[0], 0)


def _pool_kernel(hp_ref, h_ref, hn_ref, mod_ref, g1n_ref, ps_ref, g2n_ref, wp_hbm, w1_hbm, w2_hbm, o_ref,
                 wp_ref, w1_ref, w2_ref, wp_stage, w1_stage, w2_stage, sem, *, seq_len, layer):
    _load_weights_once([wp_hbm.at[layer // 2], w1_hbm.at[layer], w2_hbm.at[layer]], [wp_ref, w1_ref, w2_ref],
                       [wp_stage, w1_stage, w2_stage], sem)
    j = pl.program_id(1)
    tm = h_ref.shape[0]
    tc = tm // TOKEN_CHAINS
    gc = POOL_GC
    gain, shift, scale = g1n_ref[...], _mod_chunk(mod_ref, 0), _mod_chunk(mod_ref, 1)
    unorm = lambda x: _norm_modulate(x, gain, shift, scale)

    def prologue(i):
        r0 = i * tc
        st = {}

        def normed():
            st["u"] = unorm(h_ref[r0:r0 + tc, :])

        def with_halo():
            if i == 0:
                u_prev = jnp.where(j > 0, unorm(hp_ref[...]), 0.0)
            else:
                u_prev = unorm(h_ref[r0 - HALO:r0, :])
            if i == TOKEN_CHAINS - 1:
                u_next = jnp.where(j < pl.num_programs(1) - 1, unorm(hn_ref[...]), 0.0)
            else:
                u_next = unorm(h_ref[r0 + tc:r0 + tc + HALO, :])
            st["ue"] = jnp.concatenate([u_prev, st["u"], u_next], axis=0)

        def sum2():
            st["s2"] = st["ue"] + _shift_rows(st["ue"], 1)

        def sum4():
            s2 = st["s2"][:, gc:]
            st["s4"] = _shift_rows(s2, 1) + _shift_rows(s2, -1)

        def sum8_16():
            s4 = st["s4"][:, gc:]
            s8 = _shift_rows(s4, 2) + _shift_rows(s4, -2)
            s16 = _shift_rows(s8[:, gc:], 4) + _shift_rows(s8[:, gc:], -4)
            st["sums"] = (st["s2"][:, :gc], st["s4"][:, :gc], s8[:, :gc], s16)

        def group_linear(groups):
            def run():
                t = j * tm + r0 + lax.broadcasted_iota(jnp.int32, (tc, 1), 0)
                for gi in groups:
                    w = POOL_SIZES[gi]
                    lo = jnp.maximum(t - w // 2, 0)
                    hi = jnp.minimum(t + w - w // 2, seq_len)
                    inv_cnt = 1.0 / (hi - lo).astype(F32)
                    pooled = st["sums"][gi][HALO:HALO + tc] * inv_cnt - st["u"][:, gi * gc:(gi + 1) * gc]
                    st["y%d" % gi] = _dot(pooled.astype(BF16), wp_ref[gi * gc:(gi + 1) * gc, :]) * ps_ref[:, gi * gc:(gi + 1) * gc]
            return run

        def mlp_inputs():
            y = jnp.concatenate([st["y%d" % gi] for gi in range(len(POOL_SIZES))], axis=1)
            return _mlp_inputs(h_ref[r0:r0 + tc, :], y, mod_ref, g2n_ref)

        return _pad_stages([normed, with_halo, sum2, sum4, sum8_16, group_linear((0, 1)), group_linear((2, 3)),
                            mlp_inputs], sizes=POOL_STAGE_SIZES)

    _run_chains([prologue(i) for i in range(TOKEN_CHAINS)], mod_ref, w1_ref, w2_ref, o_ref, tc)


def _odd_layer(h, mod_rows, norm1_g, w_pool, pool_scale, norm2_g, w1, w2, layer, tm):
    b, n, _ = h.shape
    tok = lambda bi, j: (bi, j, 0)
    const2 = lambda bi, j: (0, 0)
    per_tile = tm // HALO
    last = n // HALO - 1
    return pl.pallas_call(
        functools.partial(_pool_kernel, seq_len=n, layer=layer),
        grid=(b, n // tm),
        in_specs=[
            pl.BlockSpec((None, HALO, D_MODEL), lambda bi, j: (bi, jnp.maximum(j * per_tile - 1, 0), 0)),
            pl.BlockSpec((None, tm, D_MODEL), tok),
            pl.BlockSpec((None, HALO, D_MODEL), lambda bi, j: (bi, jnp.minimum((j + 1) * per_tile, last), 0)),
            pl.BlockSpec((None, 1, N_MOD * D_MODEL), lambda bi, j: (bi, 0, 0)),
            pl.BlockSpec((1, D_MODEL), const2),
            pl.BlockSpec((1, D_MODEL), const2),
            pl.BlockSpec((1, D_MODEL), const2),
            pl.BlockSpec(memory_space=pl.ANY),
            pl.BlockSpec(memory_space=pl.ANY),
            pl.BlockSpec(memory_space=pl.ANY),
        ],
        out_specs=pl.BlockSpec((None, tm, D_MODEL), tok),
        out_shape=jax.ShapeDtypeStruct(h.shape, F32),
        scratch_shapes=_weight_scratch([w_pool.shape[1:], w1.shape[1:], w2.shape[1:]]),
        compiler_params=_params("arbitrary", "arbitrary"),
        name="odd_layer",
    )(h, h, h, mod_rows, norm1_g, pool_scale, norm2_g, w_pool, w1, w2)


def _head_segments():
    lanes = 256
    seg = (np.arange(lanes)[:, None] // NA_HEAD_DIM) == (np.arange(lanes)[None, :] // NA_HEAD_DIM)
    return jnp.asarray(seg.astype(np.float32)).astype(BF16)


def kernel(x, c, ctx, c_ctx, w_mod, b_mod, norm1_g, norm2_g, w_in_even, w_four, q_norm_g, k_norm_g, rpb, w_out_even, w_pool, pool_scale, w_mlp1, w_mlp2):
    b = x.shape[0]
    depth = w_mod.shape[0]
    assert depth == 2, "layer 0 is the even (Fourier + attention) layer, layer 1 the odd (pooling) layer"
    ctx_row = b
    pad = (-(b + 1)) % SUBLANES
    c_rows = jnp.concatenate([c, c_ctx[None, :], jnp.zeros((pad, D_MODEL), F32)], axis=0)
    mod = _modulation(c_rows, w_mod, b_mod)
    mod = mod.reshape(depth, mod.shape[1], 1, N_MOD * D_MODEL)

    row = lambda a: a.reshape(1, -1)
    heads = lambda g: jnp.tile(g, NA_HEADS).reshape(1, NA_WIDTH)
    seg = _head_segments()

    w_in = w_in_even[0].astype(BF16)
    qg, kg = heads(q_norm_g[0]), heads(k_norm_g[0])
    f, q, k, v = _in_projection(x, mod[0], lambda bi: bi, row(norm1_g[0]), w_in, qg, kg, seg, tm=1024)
    _, _, kc, vc = _in_projection(ctx, mod[0], lambda bi: ctx_row, row(norm1_g[0]), w_in, qg, kg, seg,
                                  tm=ctx.shape[1])
    o_four = _fourier_mix(f, w_four[0])
    o_na = _neighbourhood_attention(q, k, v, kc, vc, rpb[0])
    h = _even_tail(x, o_four, o_na, mod[0], w_out_even, row(norm2_g[0]), w_mlp1, w_mlp2, layer=0, tm=1024)

    w_pool_rows = w_pool.reshape(w_pool.shape[0], len(POOL_SIZES) * POOL_GC, POOL_GC)
    h = _odd_layer(h, mod[1], row(norm1_g[1]), w_pool_rows, row(pool_scale[0]), row(norm2_g[1]), w_mlp1, w_mlp2,
                   layer=1, tm=1024)
    return h
```

```python
import functools

import numpy as np
import jax
import jax.numpy as jnp
from jax import lax
from jax.experimental import pallas as pl
from jax.experimental.pallas import tpu as pltpu

D_MODEL = 1024
GRID_W = 64
FOUR_GROUPS = 4
FOUR_GC = 128
FOUR_WIDTH = 512
NA_HEADS = 8
NA_HEAD_DIM = 64
NA_WIDTH = 512
NA_WIN_ROWS = 8
NA_WIN_COLS = 16
POOL_SIZES = (2, 4, 8, 16)
POOL_GC = 256
MLP_HIDDEN = 4096
N_MOD = 6
RMS_EPS = 1e-6

VMEM_LIMIT_BYTES = 56 * 1024 * 1024
SUBLANES = 8
WEIGHT_STAGE_BYTES = 1024 * 1024
WEIGHT_STAGE_SLOTS = 4

F32 = jnp.float32
BF16 = jnp.bfloat16
MASK_VALUE = -1e30
LOG2_E = 1.4426950408889634

FFT_N1 = 128
FFT_N2 = 64
Y1_PITCH = 2 * FFT_N1 + 8
Z_PITCH = FFT_N1 + 8
FFT_UNROLL = 32

NA_Q_ROWS = 8
NA_BAND_ROWS = NA_Q_ROWS + NA_WIN_ROWS
NA_Q_COLS = 16
NA_K_COLS = NA_Q_COLS + NA_WIN_COLS
NA_GROUPS_PER_STEP = 4
TOKEN_CHAINS = 4
INPROJ_CHAIN_ROWS = 256
MLP_CHUNK = 1024
MLP_CHUNKS = MLP_HIDDEN // MLP_CHUNK
POOL_STAGE_SIZES = (2, 2, 2, 2)


def _dot(a, b):
    return jnp.dot(a, b, preferred_element_type=F32)


def _dot_f32(a, b):
    return jnp.dot(a, b, preferred_element_type=F32, precision=lax.Precision.HIGHEST)


def _dot_nt(a, b):
    return lax.dot_general(a, b, (((1,), (1,)), ((), ())), preferred_element_type=F32)


def _params(*sem):
    return pltpu.CompilerParams(dimension_semantics=sem, vmem_limit_bytes=VMEM_LIMIT_BYTES)


def _resident(shape, index_map):
    return pl.BlockSpec(shape, index_map, pipeline_mode=pl.Buffered(1))


def _stage_rows(shape):
    k, n = shape
    rows = min(k, max(2 * SUBLANES, WEIGHT_STAGE_BYTES // (4 * n)))
    assert k % rows == 0
    return rows


def _weight_scratch(shapes):
    scratch = [pltpu.VMEM(shape, BF16) for shape in shapes]
    for k, n in shapes:
        rows = _stage_rows((k, n))
        scratch.append(pltpu.VMEM((min(WEIGHT_STAGE_SLOTS, k // rows), rows, n), F32))
    return scratch + [pltpu.SemaphoreType.DMA((WEIGHT_STAGE_SLOTS * len(shapes),))]


def _load_weights_once(sources, weights, stages, sem):
    @pl.when((pl.program_id(0) == 0) & (pl.program_id(1) == 0))
    def _():
        for i, (src, dst, stage) in enumerate(zip(sources, weights, stages)):
            slots, rows = stage.shape[0], stage.shape[1]
            chunks = dst.shape[0] // rows
            ahead = max(slots - 1, 1)

            def copy(c):
                return pltpu.make_async_copy(src.at[pl.ds(c * rows, rows), :], stage.at[c % slots],
                                             sem.at[WEIGHT_STAGE_SLOTS * i + c % slots])

            for c in range(min(ahead, chunks)):
                copy(c).start()
            for c in range(chunks):
                copy(c).wait()
                dst[c * rows:(c + 1) * rows, :] = stage[c % slots].astype(BF16)
                if c + ahead < chunks:
                    copy(c + ahead).start()


def _mod_kernel(c_ref, w_ref, b_ref, o_ref):
    c = c_ref[...]
    s = c * (1.0 / (1.0 + jnp.exp(-c)))
    s_hi = s.astype(BF16)
    s_lo = (s - s_hi.astype(F32)).astype(BF16)
    w = w_ref[...].astype(BF16)
    o_ref[...] = _dot(s_hi, w) + _dot(s_lo, w) + b_ref[...]


def _modulation(c_rows, w_mod, b_mod):
    depth = w_mod.shape[0]
    rows = c_rows.shape[0]
    tn = 2 * D_MODEL
    return pl.pallas_call(
        _mod_kernel,
        grid=(depth, N_MOD * D_MODEL // tn),
        in_specs=[
            pl.BlockSpec((rows, D_MODEL), lambda l, j: (0, 0)),
            pl.BlockSpec((None, D_MODEL, tn), lambda l, j: (l, 0, j)),
            pl.BlockSpec((None, 1, tn), lambda l, j: (l, 0, j)),
        ],
        out_specs=pl.BlockSpec((None, rows, tn), lambda l, j: (l, 0, j)),
        out_shape=jax.ShapeDtypeStruct((depth, rows, N_MOD * D_MODEL), F32),
        compiler_params=_params("parallel", "parallel"),
        name="modulation",
    )(c_rows, w_mod, b_mod.reshape(depth, 1, N_MOD * D_MODEL))


def _norm_modulate(x, gain, shift, scale):
    ms = jnp.mean(x * x, axis=-1, keepdims=True)
    return (x * lax.rsqrt(ms + RMS_EPS)) * (gain * (1.0 + scale)) + shift


def _head_norm(a, gain, seg_ref):
    sq = (a * a).astype(BF16)
    half = seg_ref.shape[0]
    ms = jnp.concatenate([_dot(sq[:, :half], seg_ref[...]), _dot(sq[:, half:], seg_ref[...])], axis=1)
    return a * lax.rsqrt(ms * (1.0 / NA_HEAD_DIM) + RMS_EPS) * gain


def _inproj_kernel(x_ref, mod_ref, g_ref, w_ref, qg_ref, kg_ref, seg_ref, f_ref, q_ref, k_ref, v_ref):
    tm = x_ref.shape[0]
    chains = max(1, tm // INPROJ_CHAIN_ROWS)
    tc = tm // chains

    def normed(i):
        x = x_ref[i * tc:(i + 1) * tc, :]
        return _norm_modulate(x, g_ref[...], mod_ref[:, 0:D_MODEL], mod_ref[:, D_MODEL:2 * D_MODEL]).astype(BF16)

    def outputs(i, p):
        rows = slice(i * tc, (i + 1) * tc)
        f_ref[rows, :] = p[:, :FOUR_WIDTH]
        q = p[:, FOUR_WIDTH:FOUR_WIDTH + NA_WIDTH]
        k = p[:, FOUR_WIDTH + NA_WIDTH:FOUR_WIDTH + 2 * NA_WIDTH]
        q_ref[rows, :] = (_head_norm(q, qg_ref[...], seg_ref) * (NA_HEAD_DIM ** -0.5 * LOG2_E)).astype(BF16)
        k_ref[rows, :] = _head_norm(k, kg_ref[...], seg_ref)
        v_ref[rows, :] = p[:, FOUR_WIDTH + 2 * NA_WIDTH:]

    u, p = {}, {}
    for step in range(chains + 2):
        if step < chains:
            u[step] = normed(step)
        if 0 <= step - 1 < chains:
            p[step - 1] = _dot(u.pop(step - 1), w_ref[...])
        if 0 <= step - 2 < chains:
            outputs(step - 2, p.pop(step - 2))


def _in_projection(x, mod_rows, row_of_batch, norm_g, w_in, q_g, k_g, seg, tm):
    b, n, _ = x.shape
    width = w_in.shape[1]
    tok = lambda bi, j: (bi, j, 0)
    const2 = lambda bi, j: (0, 0)
    out_tok = lambda w: pl.BlockSpec((None, tm, w), tok)
    return pl.pallas_call(
        _inproj_kernel,
        grid=(b, n // tm),
        in_specs=[
            pl.BlockSpec((None, tm, D_MODEL), tok),
            pl.BlockSpec((None, 1, N_MOD * D_MODEL), lambda bi, j: (row_of_batch(bi), 0, 0)),
            pl.BlockSpec((1, D_MODEL), const2),
            _resident((D_MODEL, width), const2),
            pl.BlockSpec((1, NA_WIDTH), const2),
            pl.BlockSpec((1, NA_WIDTH), const2),
            pl.BlockSpec(seg.shape, const2),
        ],
        out_specs=[out_tok(FOUR_WIDTH), out_tok(NA_WIDTH), out_tok(NA_WIDTH), out_tok(NA_WIDTH)],
        out_shape=[
            jax.ShapeDtypeStruct((b, n, FOUR_WIDTH), F32),
            jax.ShapeDtypeStruct((b, n, NA_WIDTH), BF16),
            jax.ShapeDtypeStruct((b, n, NA_WIDTH), F32),
            jax.ShapeDtypeStruct((b, n, NA_WIDTH), F32),
        ],
        compiler_params=_params("parallel", "parallel"),
        name="in_projection",
    )(x, mod_rows, norm_g, w_in, q_g, k_g, seg)


def _fft_tables(n):
    n1, n2 = FFT_N1, FFT_N2
    assert n1 * n2 == n
    k1 = np.arange(n1)[None, :, None]
    pos = (n2 * np.arange(n1)[None, None, :] + np.arange(n2)[:, None, None])
    ang = 2.0 * np.pi * ((k1 * pos) % n) / n
    l1 = np.concatenate([np.cos(ang), -np.sin(ang)], axis=1)
    a2 = 2.0 * np.pi * ((np.arange(n2)[:, None] * np.arange(n2)[None, :]) % n2) / n2
    c2, s2 = np.cos(a2), np.sin(a2)
    l2 = np.block([[c2, s2], [-s2, c2]])
    ac = 2.0 * np.pi * ((np.arange(FOUR_GC)[:, None] * np.arange(FOUR_GC)[None, :]) % FOUR_GC) / FOUR_GC
    cs = np.concatenate([np.cos(ac), np.sin(ac)], axis=0)
    return l1.astype(np.float32), l2.astype(np.float32), cs.astype(np.float32)


def _fft_kernel(f_ref, l1_ref, l2_ref, cs_ref, w_ref, o_ref, y1_ref, z_ref, *, ortho):
    n1, n2 = FFT_N1, FFT_N2

    def stage1(j, carry):
        r = f_ref[pl.ds(j, n1, stride=n2), :].astype(BF16)
        y1_ref[pl.ds(pl.multiple_of(j * Y1_PITCH, SUBLANES), 2 * n1), :] = _dot(l1_ref[j], r)
        return carry

    lax.fori_loop(0, n2, stage1, 0, unroll=FFT_UNROLL)

    l2 = l2_ref[...]

    def stage2(k1, carry):
        yr = y1_ref[pl.ds(k1, n2, stride=Y1_PITCH), :]
        yi = y1_ref[pl.ds(n1 + k1, n2, stride=Y1_PITCH), :]
        z = _dot(l2, jnp.concatenate([yr, yi], axis=0).astype(BF16))
        z_ref[0, pl.ds(k1, n2, stride=Z_PITCH), :] = z[:n2]
        z_ref[1, pl.ds(k1, n2, stride=Z_PITCH), :] = z[n2:]
        return carry

    lax.fori_loop(0, n1, stage2, 0, unroll=FFT_UNROLL)

    wf = (_dot_f32(cs_ref[...], w_ref[...]) * ortho).astype(BF16)
    wr, wi = wf[:FOUR_GC], wf[FOUR_GC:]

    def stage3(k2, carry):
        base = pl.multiple_of(k2 * Z_PITCH, SUBLANES)
        zr = z_ref[0, pl.ds(base, n1), :].astype(BF16)
        zi = z_ref[1, pl.ds(base, n1), :].astype(BF16)
        o_ref[pl.ds(pl.multiple_of(k2 * n1, n1), n1), :] = (_dot(zr, wr) + _dot(zi, wi)).astype(o_ref.dtype)
        return carry

    lax.fori_loop(0, n2, stage3, 0, unroll=FFT_UNROLL)


def _fourier_mix(f, w_four):
    b, n, _ = f.shape
    l1, l2, cs = _fft_tables(n)
    l1 = jnp.asarray(l1).astype(BF16)
    l2 = jnp.asarray(l2).astype(BF16)
    cs = jnp.asarray(cs)
    ortho = float(1.0 / np.sqrt(n * FOUR_GC))
    slab = lambda bi, g: (bi, 0, g)
    return pl.pallas_call(
        functools.partial(_fft_kernel, ortho=ortho),
        grid=(b, FOUR_GROUPS),
        in_specs=[
            pl.BlockSpec((None, n, FOUR_GC), slab),
            _resident(l1.shape, lambda bi, g: (0, 0, 0)),
            _resident(l2.shape, lambda bi, g: (0, 0)),
            _resident(cs.shape, lambda bi, g: (0, 0)),
            pl.BlockSpec((None, FOUR_GC, FOUR_GC), lambda bi, g: (g, 0, 0)),
        ],
        out_specs=pl.BlockSpec((None, n, FOUR_GC), slab),
        out_shape=jax.ShapeDtypeStruct((b, n, FOUR_WIDTH), BF16),
        scratch_shapes=[
            pltpu.VMEM((FFT_N2 * Y1_PITCH, FOUR_GC), F32),
            pltpu.VMEM((2, FFT_N2 * Z_PITCH, FOUR_GC), F32),
        ],
        compiler_params=_params("parallel", "parallel"),
        name="fourier_mix",
    )(f, l1, l2, cs, w_four)


def _band_start(g, rows):
    return jnp.clip(NA_Q_ROWS * g - (NA_BAND_ROWS - NA_Q_ROWS) // 2, 0, rows - NA_BAND_ROWS)


def _patch_col_start(cb):
    return int(np.clip(NA_Q_COLS * cb - (NA_K_COLS - NA_Q_COLS) // 2, 0, GRID_W - NA_K_COLS))


def _natten_bias_tables(rpb, rows):
    heads = rpb.shape[0]
    groups = rows // NA_Q_ROWS
    n_dr, n_dc = 2 * NA_WIN_ROWS - 1, 2 * NA_WIN_COLS - 1
    n_cb = GRID_W // NA_Q_COLS
    nq, nk = NA_Q_ROWS * NA_Q_COLS, NA_BAND_ROWS * NA_K_COLS
    lane_tiles = 128 // NA_K_COLS
    lead = NA_WIN_ROWS
    strip_tiles = 2 * NA_BAND_ROWS
    col = np.arange(GRID_W)
    col_start = np.clip(col - NA_WIN_COLS // 2, 0, GRID_W - NA_WIN_COLS)
    col_ok = (col[None, :] >= col_start[:, None]) & (col[None, :] < col_start[:, None] + NA_WIN_COLS)
    dc = np.clip(col[None, :] - col[:, None] + (NA_WIN_COLS - 1), 0, n_dc - 1)
    onehot_dc = (dc[None] == np.arange(n_dc)[:, None, None]).astype(np.float32)
    tiles = jnp.einsum('hre,eqk->hrqk', rpb.astype(F32), jnp.asarray(onehot_dc), precision=lax.Precision.HIGHEST)
    tiles = jnp.where(jnp.asarray(col_ok), tiles * LOG2_E, MASK_VALUE)
    total_tiles = -(-(lead + n_dr + NA_BAND_ROWS) // lane_tiles) * lane_tiles
    tail = total_tiles - lead - n_dr
    tiles = jnp.pad(tiles, ((0, 0), (lead, tail), (0, 0), (0, 0)), constant_values=MASK_VALUE)
    strips = []
    for cb in range(n_cb):
        q0, k0 = NA_Q_COLS * cb, _patch_col_start(cb)
        assert col_start[q0] >= k0 and col_start[q0 + NA_Q_COLS - 1] + NA_WIN_COLS <= k0 + NA_K_COLS
        sub = tiles[:, :, q0:q0 + NA_Q_COLS, k0:k0 + NA_K_COLS].transpose(0, 2, 1, 3)
        strips.append(sub.reshape(heads, NA_Q_COLS, total_tiles * NA_K_COLS))
    strips = jnp.stack(strips, axis=1)

    plan = []
    for kind, g in enumerate((0, groups // 2, groups - 1)):
        band0 = int(np.clip(NA_Q_ROWS * g - (NA_BAND_ROWS - NA_Q_ROWS) // 2, 0, rows - NA_BAND_ROWS))
        for a in range(NA_Q_ROWS):
            qr = NA_Q_ROWS * g + a
            row_start = int(np.clip(qr - NA_WIN_ROWS // 2, 0, rows - NA_WIN_ROWS))
            assert band0 <= row_start and row_start + NA_WIN_ROWS <= band0 + NA_BAND_ROWS
            first_tile = band0 - qr + NA_WIN_ROWS - 1 + lead
            assert 0 <= first_tile and first_tile + NA_BAND_ROWS <= total_tiles
            for cb in range(n_cb):
                for hd in range(2):
                    plan.append((kind * n_cb + cb, hd, cb, a, first_tile, row_start - band0))

    def build(strip_ref, o_ref):
        patch_row = lax.broadcasted_iota(jnp.int32, (1, nk), 1) // NA_K_COLS
        for idx, hd, cb, a, first_tile, first_row in plan:
            lane0 = first_tile * NA_K_COLS
            val = strip_ref[hd, cb, :, lane0:lane0 + nk]
            in_window = (patch_row >= first_row) & (patch_row < first_row + NA_WIN_ROWS)
            r0 = hd * nq + a * NA_Q_COLS
            o_ref[idx, r0:r0 + NA_Q_COLS, :] = jnp.where(in_window, val, MASK_VALUE)

    return pl.pallas_call(
        build,
        grid=(heads // 2,),
        in_specs=[pl.BlockSpec((2,) + strips.shape[1:], lambda hp: (hp, 0, 0, 0))],
        out_specs=pl.BlockSpec((None, 3 * n_cb, 2 * nq, nk), lambda hp: (hp, 0, 0, 0)),
        out_shape=jax.ShapeDtypeStruct((heads // 2, 3 * n_cb, 2 * nq, nk), F32),
        compiler_params=_params("parallel"),
        name="bias_table",
    )(strips)


def _natten_kernel(q_ref, k_ref, v_ref, kc_ref, vc_ref, tab_ref, o_ref, *, rows):
    nq, nk = NA_Q_ROWS * NA_Q_COLS, NA_BAND_ROWS * NA_K_COLS
    groups = rows // NA_Q_ROWS
    n_cb = GRID_W // NA_Q_COLS
    first_head = lax.broadcasted_iota(jnp.int32, (1, 2 * NA_HEAD_DIM), 1) < NA_HEAD_DIM
    kc, vc = kc_ref[...].astype(BF16), vc_ref[...].astype(BF16)
    chains = [(g, cb) for g in range(NA_GROUPS_PER_STEP) for cb in range(n_cb)]

    def group_index(g):
        return pl.program_id(2) * NA_GROUPS_PER_STEP + g

    def q_rows(g, cb, a):
        start = (g * NA_Q_ROWS + a) * GRID_W + cb * NA_Q_COLS
        return slice(start, start + NA_Q_COLS)

    def patch(ref, g, cb):
        tok0 = _band_start(group_index(g), rows) * GRID_W + _patch_col_start(cb)
        parts = [ref[pl.ds(pl.multiple_of(tok0 + bb * GRID_W, SUBLANES), NA_K_COLS), :] for bb in range(NA_BAND_ROWS)]
        return jnp.concatenate(parts, axis=0).astype(BF16)

    def scores(g, cb):
        gi = group_index(g)
        kind = jnp.where(gi == 0, 0, jnp.where(gi == groups - 1, 2, 1))
        q2 = jnp.concatenate([q_ref[q_rows(g, cb, a), :] for a in range(NA_Q_ROWS)], axis=0)
        zero = jnp.zeros_like(q2)
        qs = jnp.concatenate([jnp.where(first_head, q2, zero), jnp.where(first_head, zero, q2)], axis=0)
        s_all = _dot_nt(qs, jnp.concatenate([patch(k_ref, g, cb), kc], axis=0))
        return s_all[:, :nk] + tab_ref[kind * n_cb + cb], s_all[:, nk:]

    def softmax(s_win, s_ctx):
        s_all = jnp.concatenate([s_win, s_ctx], axis=1)
        p = jnp.exp2(s_all - jnp.max(s_all, axis=-1, keepdims=True))
        return p.astype(BF16), jnp.sum(p, axis=-1, keepdims=True)

    def weighted_values(g, cb, p, l):
        values = jnp.concatenate([patch(v_ref, g, cb), vc], axis=0)
        o = _dot(p, values) / l
        o = jnp.where(first_head, o[:nq], o[nq:]).astype(o_ref.dtype)
        for a in range(NA_Q_ROWS):
            o_ref[q_rows(g, cb, a), :] = o[a * NA_Q_COLS:(a + 1) * NA_Q_COLS]

    s, p = {}, {}
    for step in range(len(chains) + 2):
        if step < len(chains):
            s[step] = scores(*chains[step])
        if 0 <= step - 1 < len(chains):
            p[step - 1] = softmax(*s.pop(step - 1))
        if 0 <= step - 2 < len(chains):
            weighted_values(*chains[step - 2], *p.pop(step - 2))


def _neighbourhood_attention(q, k, v, kc, vc, rpb):
    b, n, _ = q.shape
    rows = n // GRID_W
    steps = rows // (NA_Q_ROWS * NA_GROUPS_PER_STEP)
    ctx_len = kc.shape[1]
    tq = NA_GROUPS_PER_STEP * NA_Q_ROWS * GRID_W
    tables = _natten_bias_tables(rpb, rows)
    lanes = 2 * NA_HEAD_DIM
    seq = lambda bi, hp, i: (bi, 0, hp)
    return pl.pallas_call(
        functools.partial(_natten_kernel, rows=rows),
        grid=(b, NA_HEADS // 2, steps),
        in_specs=[
            pl.BlockSpec((None, tq, lanes), lambda bi, hp, i: (bi, i, hp)),
            pl.BlockSpec((None, n, lanes), seq),
            pl.BlockSpec((None, n, lanes), seq),
            pl.BlockSpec((None, ctx_len, lanes), seq),
            pl.BlockSpec((None, ctx_len, lanes), seq),
            pl.BlockSpec((None,) + tables.shape[1:], lambda bi, hp, i: (hp, 0, 0, 0)),
        ],
        out_specs=pl.BlockSpec((None, tq, lanes), lambda bi, hp, i: (bi, i, hp)),
        out_shape=jax.ShapeDtypeStruct((b, n, NA_WIDTH), BF16),
        compiler_params=_params("parallel", "parallel", "arbitrary"),
        name="neighbourhood_attention",
    )(q, k, v, kc, vc, tables)


def _mod_chunk(mod_ref, idx):
    return mod_ref[:, idx * D_MODEL:(idx + 1) * D_MODEL]


def _mlp_inputs(h, y, mod_ref, g2n_ref):
    h1 = h + _mod_chunk(mod_ref, 2) * y
    u = _norm_modulate(h1, g2n_ref[...], _mod_chunk(mod_ref, 3), _mod_chunk(mod_ref, 4)).astype(BF16)
    return h1, u


def _mlp_chunk(u, acc, c, w1_ref, w2_ref):
    lo, hi = c * MLP_CHUNK, (c + 1) * MLP_CHUNK
    a = jnp.maximum(_dot(u, w1_ref[:, lo:hi]), 0.0)
    part = _dot((a * a).astype(BF16), w2_ref[lo:hi, :])
    return part if acc is None else acc + part


def _pad_stages(stages, sizes=None):
    if sizes is None:
        sizes = [1] * (len(stages) - 1) + [0] * (MLP_CHUNKS - len(stages)) + [1]
    assert len(sizes) == MLP_CHUNKS and sum(sizes) == len(stages)

    def merged(group):
        def run():
            out = None
            for stage in group:
                out = stage()
            return out
        return run

    starts = [sum(sizes[:k]) for k in range(MLP_CHUNKS)]
    return [merged(stages[s0:s0 + n]) for s0, n in zip(starts, sizes)]


def _run_chains(prologues, mod_ref, w1_ref, w2_ref, o_ref, tc):
    n = len(prologues)
    for stage in prologues[0][:-1]:
        stage()
    h1, u = prologues[0][-1]()
    for i in range(n):
        acc, nxt = None, None
        for c in range(MLP_CHUNKS):
            acc = _mlp_chunk(u, acc, c, w1_ref, w2_ref)
            if i + 1 < n:
                nxt = prologues[i + 1][c]()
        o_ref[i * tc:(i + 1) * tc, :] = h1 + _mod_chunk(mod_ref, 5) * acc
        if i + 1 < n:
            h1, u = nxt


def _even_tail_kernel(h_ref, of_ref, on_ref, mod_ref, g2n_ref, wo_hbm, w1_hbm, w2_hbm, o_ref,
                      wo_ref, w1_ref, w2_ref, wo_stage, w1_stage, w2_stage, sem, *, layer):
    _load_weights_once([wo_hbm.at[layer // 2], w1_hbm.at[layer], w2_hbm.at[layer]], [wo_ref, w1_ref, w2_ref],
                       [wo_stage, w1_stage, w2_stage], sem)
    tc = h_ref.shape[0] // TOKEN_CHAINS

    def prologue(i):
        rows = slice(i * tc, (i + 1) * tc)
        st = {}

        def s0():
            st["y"] = _dot(of_ref[rows, :], wo_ref[:FOUR_WIDTH, :])

        def s1():
            st["y"] = st["y"] + _dot(on_ref[rows, :], wo_ref[FOUR_WIDTH:, :])

        def s3():
            return _mlp_inputs(h_ref[rows, :], st["y"], mod_ref, g2n_ref)

        return _pad_stages([s0, s1, s3])

    _run_chains([prologue(i) for i in range(TOKEN_CHAINS)], mod_ref, w1_ref, w2_ref, o_ref, tc)


def _even_tail(h, o_four, o_na, mod_rows, w_out, norm2_g, w1, w2, layer, tm):
    b, n, _ = h.shape
    tok = lambda bi, j: (bi, j, 0)
    const2 = lambda bi, j: (0, 0)
    return pl.pallas_call(
        functools.partial(_even_tail_kernel, layer=layer),
        grid=(b, n // tm),
        in_specs=[
            pl.BlockSpec((None, tm, D_MODEL), tok),
            pl.BlockSpec((None, tm, FOUR_WIDTH), tok),
            pl.BlockSpec((None, tm, NA_WIDTH), tok),
            pl.BlockSpec((None, 1, N_MOD * D_MODEL), lambda bi, j: (bi, 0, 0)),
            pl.BlockSpec((1, D_MODEL), const2),
            pl.BlockSpec(memory_space=pl.ANY),
            pl.BlockSpec(memory_space=pl.ANY),
            pl.BlockSpec(memory_space=pl.ANY),
        ],
        out_specs=pl.BlockSpec((None, tm, D_MODEL), tok),
        out_shape=jax.ShapeDtypeStruct(h.shape, F32),
        scratch_shapes=_weight_scratch([w_out.shape[1:], w1.shape[1:], w2.shape[1:]]),
        compiler_params=_params("arbitrary", "arbitrary"),
        name="even_tail",
    )(h, o_four, o_na, mod_rows, norm2_g, w_out, w1, w2)


HALO = 8


def _shift_rows(a, s):
    return pltpu.roll(a, s % a.shape[0], 0)


def _pool_kernel(hp_ref, h_ref, hn_ref, mod_ref, g1n_ref, ps_ref, g2n_ref, wp_hbm, w1_hbm, w2_hbm, o_ref,
                 wp_ref, w1_ref, w2_ref, wp_stage, w1_stage, w2_stage, sem, *, seq_len, layer):
    _load_weights_once([wp_hbm.at[layer // 2], w1_hbm.at[layer], w2_hbm.at[layer]], [wp_ref, w1_ref, w2_ref],
                       [wp_stage, w1_stage, w2_stage], sem)
    j = pl.program_id(1)
    tm = h_ref.shape[0]
    tc = tm // TOKEN_CHAINS
    gc = POOL_GC
    gain, shift, scale = g1n_ref[...], _mod_chunk(mod_ref, 0), _mod_chunk(mod_ref, 1)
    unorm = lambda x: _norm_modulate(x, gain, shift, scale)

    def prologue(i):
        r0 = i * tc
        st = {}

        def normed():
            st["u"] = unorm(h_ref[r0:r0 + tc, :])

        def with_halo():
            if i == 0:
                u_prev = jnp.where(j > 0, unorm(hp_ref[...]), 0.0)
            else:
                u_prev = unorm(h_ref[r0 - HALO:r0, :])
            if i == TOKEN_CHAINS - 1:
                u_next = jnp.where(j < pl.num_programs(1) - 1, unorm(hn_ref[...]), 0.0)
            else:
                u_next = unorm(h_ref[r0 + tc:r0 + tc + HALO, :])
            st["ue"] = jnp.concatenate([u_prev, st["u"], u_next], axis=0)

        def sum2():
            st["s2"] = st["ue"] + _shift_rows(st["ue"], 1)

        def sum4():
            s2 = st["s2"][:, gc:]
            st["s4"] = _shift_rows(s2, 1) + _shift_rows(s2, -1)

        def sum8_16():
            s4 = st["s4"][:, gc:]
            s8 = _shift_rows(s4, 2) + _shift_rows(s4, -2)
            s16 = _shift_rows(s8[:, gc:], 4) + _shift_rows(s8[:, gc:], -4)
            st["sums"] = (st["s2"][:, :gc], st["s4"][:, :gc], s8[:, :gc], s16)

        def group_linear(groups):
            def run():
                t = j * tm + r0 + lax.broadcasted_iota(jnp.int32, (tc, 1), 0)
                for gi in groups:
                    w = POOL_SIZES[gi]
                    lo = jnp.maximum(t - w // 2, 0)
                    hi = jnp.minimum(t + w - w // 2, seq_len)
                    inv_cnt = 1.0 / (hi - lo).astype(F32)
                    pooled = st["sums"][gi][HALO:HALO + tc] * inv_cnt - st["u"][:, gi * gc:(gi + 1) * gc]
                    st["y%d" % gi] = _dot(pooled.astype(BF16), wp_ref[gi * gc:(gi + 1) * gc, :]) * ps_ref[:, gi * gc:(gi + 1) * gc]
            return run

        def mlp_inputs():
            y = jnp.concatenate([st["y%d" % gi] for gi in range(len(POOL_SIZES))], axis=1)
            return _mlp_inputs(h_ref[r0:r0 + tc, :], y, mod_ref, g2n_ref)

        return _pad_stages([normed, with_halo, sum2, sum4, sum8_16, group_linear((0, 1)), group_linear((2, 3)),
                            mlp_inputs], sizes=POOL_STAGE_SIZES)

    _run_chains([prologue(i) for i in range(TOKEN_CHAINS)], mod_ref, w1_ref, w2_ref, o_ref, tc)


def _odd_layer(h, mod_rows, norm1_g, w_pool, pool_scale, norm2_g, w1, w2, layer, tm):
    b, n, _ = h.shape
    tok = lambda bi, j: (bi, j, 0)
    const2 = lambda bi, j: (0, 0)
    per_tile = tm // HALO
    last = n // HALO - 1
    return pl.pallas_call(
        functools.partial(_pool_kernel, seq_len=n, layer=layer),
        grid=(b, n // tm),
        in_specs=[
            pl.BlockSpec((None, HALO, D_MODEL), lambda bi, j: (bi, jnp.maximum(j * per_tile - 1, 0), 0)),
            pl.BlockSpec((None, tm, D_MODEL), tok),
            pl.BlockSpec((None, HALO, D_MODEL), lambda bi, j: (bi, jnp.minimum((j + 1) * per_tile, last), 0)),
            pl.BlockSpec((None, 1, N_MOD * D_MODEL), lambda bi, j: (bi, 0, 0)),
            pl.BlockSpec((1, D_MODEL), const2),
            pl.BlockSpec((1, D_MODEL), const2),
            pl.BlockSpec((1, D_MODEL), const2),
            pl.BlockSpec(memory_space=pl.ANY),
            pl.BlockSpec(memory_space=pl.ANY),
            pl.BlockSpec(memory_space=pl.ANY),
        ],
        out_specs=pl.BlockSpec((None, tm, D_MODEL), tok),
        out_shape=jax.ShapeDtypeStruct(h.shape, F32),
        scratch_shapes=_weight_scratch([w_pool.shape[1:], w1.shape[1:], w2.shape[1:]]),
        compiler_params=_params("arbitrary", "arbitrary"),
        name="odd_layer",
    )(h, h, h, mod_rows, norm1_g, pool_scale, norm2_g, w_pool, w1, w2)


def _head_segments():
    lanes = 256
    seg = (np.arange(lanes)[:, None] // NA_HEAD_DIM) == (np.arange(lanes)[None, :] // NA_HEAD_DIM)
    return jnp.asarray(seg.astype(np.float32)).astype(BF16)


def kernel(x, c, ctx, c_ctx, w_mod, b_mod, norm1_g, norm2_g, w_in_even, w_four, q_norm_g, k_norm_g, rpb, w_out_even, w_pool, pool_scale, w_mlp1, w_mlp2):
    b = x.shape[0]
    depth = w_mod.shape[0]
    assert depth == 2, "layer 0 is the even (Fourier + attention) layer, layer 1 the odd (pooling) layer"
    ctx_row = b
    pad = (-(b + 1)) % SUBLANES
    c_rows = jnp.concatenate([c, c_ctx[None, :], jnp.zeros((pad, D_MODEL), F32)], axis=0)
    mod = _modulation(c_rows, w_mod, b_mod)
    mod = mod.reshape(depth, mod.shape[1], 1, N_MOD * D_MODEL)

    row = lambda a: a.reshape(1, -1)
    heads = lambda g: jnp.tile(g, NA_HEADS).reshape(1, NA_WIDTH)
    seg = _head_segments()

    w_in = w_in_even[0].astype(BF16)
    qg, kg = heads(q_norm_g[0]), heads(k_norm_g[0])
    f, q, k, v = _in_projection(x, mod[0], lambda bi: bi, row(norm1_g[0]), w_in, qg, kg, seg, tm=1024)
    _, _, kc, vc = _in_projection(ctx, mod[0], lambda bi: ctx_row, row(norm1_g[0]), w_in, qg, kg, seg,
                                  tm=ctx.shape[1])
    o_four = _fourier_mix(f, w_four[0])
    o_na = _neighbourhood_attention(q, k, v, kc, vc, rpb[0])
    h = _even_tail(x, o_four, o_na, mod[0], w_out_even, row(norm2_g[0]), w_mlp1, w_mlp2, layer=0, tm=1024)

    w_pool_rows = w_pool.reshape(w_pool.shape[0], len(POOL_SIZES) * POOL_GC, POOL_GC)
    h = _odd_layer(h, mod[1], row(norm1_g[1]), w_pool_rows, row(pool_scale[0]), row(norm2_g[1]), w_mlp1, w_mlp2,
                   layer=1, tm=1024)
    return h
```

```python
import functools

import numpy as np
import jax
import jax.numpy as jnp
from jax import lax
from jax.experimental import pallas as pl
from jax.experimental.pallas import tpu as pltpu

D_MODEL = 1024
GRID_W = 64
FOUR_GROUPS = 4
FOUR_GC = 128
FOUR_WIDTH = 512
NA_HEADS = 8
NA_HEAD_DIM = 64
NA_WIDTH = 512
NA_WIN_ROWS = 8
NA_WIN_COLS = 16
POOL_SIZES = (2, 4, 8, 16)
POOL_GC = 256
MLP_HIDDEN = 4096
N_MOD = 6
RMS_EPS = 1e-6

VMEM_LIMIT_BYTES = 56 * 1024 * 1024
SUBLANES = 8
WEIGHT_STAGE_BYTES = 1024 * 1024
WEIGHT_STAGE_SLOTS = 4

F32 = jnp.float32
BF16 = jnp.bfloat16
MASK_VALUE = -1e30
LOG2_E = 1.4426950408889634

FFT_N1 = 128
FFT_N2 = 64
X_PITCH = FFT_N2 + 8
Y1_PITCH = 2 * FFT_N1 + 8
Z_PITCH = FFT_N1 + 8
FFT_UNROLL = 32

NA_Q_ROWS = 8
NA_BAND_ROWS = NA_Q_ROWS + NA_WIN_ROWS
NA_Q_COLS = 16
NA_K_COLS = NA_Q_COLS + NA_WIN_COLS
NA_GROUPS_PER_STEP = 8
TAIL_CHUNK_ROWS = (256, 256, 256, 256)
INPROJ_CHAIN_ROWS = 256
MLP_CHUNK = 1024
MLP_CHUNKS = MLP_HIDDEN // MLP_CHUNK
POOL_STAGE_SIZES = (2, 2, 2, 2)


def _dot(a, b):
    return jnp.dot(a, b, preferred_element_type=F32)


def _dot_f32(a, b):
    return jnp.dot(a, b, preferred_element_type=F32, precision=lax.Precision.HIGHEST)


def _dot_nt(a, b):
    return lax.dot_general(a, b, (((1,), (1,)), ((), ())), preferred_element_type=F32)


def _params(*sem):
    return pltpu.CompilerParams(dimension_semantics=sem, vmem_limit_bytes=VMEM_LIMIT_BYTES)


def _resident(shape, index_map):
    return pl.BlockSpec(shape, index_map, pipeline_mode=pl.Buffered(1))


def _stage_rows(shape):
    k, n = shape
    rows = min(k, max(2 * SUBLANES, WEIGHT_STAGE_BYTES // (4 * n)))
    assert k % rows == 0
    return rows


def _weight_scratch(shapes):
    scratch = [pltpu.VMEM(shape, BF16) for shape in shapes]
    for k, n in shapes:
        rows = _stage_rows((k, n))
        scratch.append(pltpu.VMEM((min(WEIGHT_STAGE_SLOTS, k // rows), rows, n), F32))
    return scratch + [pltpu.SemaphoreType.DMA((WEIGHT_STAGE_SLOTS * len(shapes),))]


def _load_weights_once(sources, weights, stages, sem):
    @pl.when((pl.program_id(0) == 0) & (pl.program_id(1) == 0))
    def _():
        for i, (src, dst, stage) in enumerate(zip(sources, weights, stages)):
            slots, rows = stage.shape[0], stage.shape[1]
            chunks = dst.shape[0] // rows
            ahead = max(slots - 1, 1)

            def copy(c):
                return pltpu.make_async_copy(src.at[pl.ds(c * rows, rows), :], stage.at[c % slots],
                                             sem.at[WEIGHT_STAGE_SLOTS * i + c % slots])

            for c in range(min(ahead, chunks)):
                copy(c).start()
            for c in range(chunks):
                copy(c).wait()
                dst[c * rows:(c + 1) * rows, :] = stage[c % slots].astype(BF16)
                if c + ahead < chunks:
                    copy(c + ahead).start()


def _mod_kernel(c_ref, w_ref, b_ref, o_ref):
    c = c_ref[...]
    s = c * (1.0 / (1.0 + jnp.exp(-c)))
    s_hi = s.astype(BF16)
    s_lo = (s - s_hi.astype(F32)).astype(BF16)
    w = w_ref[...].astype(BF16)
    o_ref[...] = _dot(s_hi, w) + _dot(s_lo, w) + b_ref[...]


def _modulation(c_rows, w_mod, b_mod):
    depth = w_mod.shape[0]
    rows = c_rows.shape[0]
    tn = 2 * D_MODEL
    return pl.pallas_call(
        _mod_kernel,
        grid=(depth, N_MOD * D_MODEL // tn),
        in_specs=[
            pl.BlockSpec((rows, D_MODEL), lambda l, j: (0, 0)),
            pl.BlockSpec((None, D_MODEL, tn), lambda l, j: (l, 0, j)),
            pl.BlockSpec((None, 1, tn), lambda l, j: (l, 0, j)),
        ],
        out_specs=pl.BlockSpec((None, rows, tn), lambda l, j: (l, 0, j)),
        out_shape=jax.ShapeDtypeStruct((depth, rows, N_MOD * D_MODEL), F32),
        compiler_params=_params("parallel", "parallel"),
        name="modulation",
    )(c_rows, w_mod, b_mod.reshape(depth, 1, N_MOD * D_MODEL))


def _norm_modulate(x, gain, shift, scale):
    ms = jnp.mean(x * x, axis=-1, keepdims=True)
    return (x * lax.rsqrt(ms + RMS_EPS)) * (gain * (1.0 + scale)) + shift


def _head_norm(a, gain, seg_ref):
    sq = (a * a).astype(BF16)
    half = seg_ref.shape[0]
    ms = jnp.concatenate([_dot(sq[:, :half], seg_ref[...]), _dot(sq[:, half:], seg_ref[...])], axis=1)
    return a * lax.rsqrt(ms * (1.0 / NA_HEAD_DIM) + RMS_EPS) * gain


def _inproj_kernel(x_ref, mod_ref, g_ref, w_ref, qg_ref, kg_ref, seg_ref, f_ref, q_ref, k_ref, v_ref):
    tm = x_ref.shape[0]
    chains = max(1, tm // INPROJ_CHAIN_ROWS)
    tc = tm // chains

    def normed(i):
        x = x_ref[i * tc:(i + 1) * tc, :]
        return _norm_modulate(x, g_ref[...], mod_ref[:, 0:D_MODEL], mod_ref[:, D_MODEL:2 * D_MODEL]).astype(BF16)

    def outputs(i, p):
        rows = slice(i * tc, (i + 1) * tc)
        f_ref[rows, :] = p[:, :FOUR_WIDTH]
        q = p[:, FOUR_WIDTH:FOUR_WIDTH + NA_WIDTH]
        k = p[:, FOUR_WIDTH + NA_WIDTH:FOUR_WIDTH + 2 * NA_WIDTH]
        q_ref[rows, :] = (_head_norm(q, qg_ref[...], seg_ref) * (NA_HEAD_DIM ** -0.5 * LOG2_E)).astype(BF16)
        k_ref[rows, :] = _head_norm(k, kg_ref[...], seg_ref)
        v_ref[rows, :] = p[:, FOUR_WIDTH + 2 * NA_WIDTH:]

    u, p = {}, {}
    for step in range(chains + 2):
        if step < chains:
            u[step] = normed(step)
        if 0 <= step - 1 < chains:
            p[step - 1] = _dot(u.pop(step - 1), w_ref[...])
        if 0 <= step - 2 < chains:
            outputs(step - 2, p.pop(step - 2))


def _in_projection(x, mod_rows, row_of_batch, norm_g, w_in, q_g, k_g, seg, tm):
    b, n, _ = x.shape
    width = w_in.shape[1]
    tok = lambda bi, j: (bi, j, 0)
    const2 = lambda bi, j: (0, 0)
    out_tok = lambda w: pl.BlockSpec((None, tm, w), tok)
    return pl.pallas_call(
        _inproj_kernel,
        grid=(b, n // tm),
        in_specs=[
            pl.BlockSpec((None, tm, D_MODEL), tok),
            pl.BlockSpec((None, 1, N_MOD * D_MODEL), lambda bi, j: (row_of_batch(bi), 0, 0)),
            pl.BlockSpec((1, D_MODEL), const2),
            _resident((D_MODEL, width), const2),
            pl.BlockSpec((1, NA_WIDTH), const2),
            pl.BlockSpec((1, NA_WIDTH), const2),
            pl.BlockSpec(seg.shape, const2),
        ],
        out_specs=[out_tok(FOUR_WIDTH), out_tok(NA_WIDTH), out_tok(NA_WIDTH), out_tok(NA_WIDTH)],
        out_shape=[
            jax.ShapeDtypeStruct((b, n, FOUR_WIDTH), F32),
            jax.ShapeDtypeStruct((b, n, NA_WIDTH), BF16),
            jax.ShapeDtypeStruct((b, n, NA_WIDTH), F32),
            jax.ShapeDtypeStruct((b, n, NA_WIDTH), F32),
        ],
        compiler_params=_params("parallel", "parallel"),
        name="in_projection",
    )(x, mod_rows, norm_g, w_in, q_g, k_g, seg)


def _fft_tables(n):
    n1, n2 = FFT_N1, FFT_N2
    assert n1 * n2 == n
    k1 = np.arange(n1)[None, :, None]
    pos = (n2 * np.arange(n1)[None, None, :] + np.arange(n2)[:, None, None])
    ang = 2.0 * np.pi * ((k1 * pos) % n) / n
    l1 = np.concatenate([np.cos(ang), -np.sin(ang)], axis=1)
    a2 = 2.0 * np.pi * ((np.arange(n2)[:, None] * np.arange(n2)[None, :]) % n2) / n2
    c2, s2 = np.cos(a2), np.sin(a2)
    l2 = np.block([[c2, s2], [-s2, c2]])
    ac = 2.0 * np.pi * ((np.arange(FOUR_GC)[:, None] * np.arange(FOUR_GC)[None, :]) % FOUR_GC) / FOUR_GC
    cs = np.concatenate([np.cos(ac), np.sin(ac)], axis=0)
    return l1.astype(np.float32), l2.astype(np.float32), cs.astype(np.float32)


def _fft_kernel(f_ref, l1_ref, l2_ref, cs_ref, w_ref, o_ref, x_ref, y1_ref, z_ref, *, ortho):
    n1, n2 = FFT_N1, FFT_N2

    def repitch(i, carry):
        x_ref[pl.ds(pl.multiple_of(i * X_PITCH, SUBLANES), n2), :] = f_ref[pl.ds(pl.multiple_of(i * n2, n2), n2), :]
        return carry

    lax.fori_loop(0, n1, repitch, 0, unroll=FFT_UNROLL)

    def stage1(j, carry):
        r = x_ref[pl.ds(j, n1, stride=X_PITCH), :].astype(BF16)
        y1_ref[pl.ds(pl.multiple_of(j * Y1_PITCH, SUBLANES), 2 * n1), :] = _dot(l1_ref[j], r)
        return carry

    lax.fori_loop(0, n2, stage1, 0, unroll=FFT_UNROLL)

    l2 = l2_ref[...]

    def stage2(k1, carry):
        yr = y1_ref[pl.ds(k1, n2, stride=Y1_PITCH), :]
        yi = y1_ref[pl.ds(n1 + k1, n2, stride=Y1_PITCH), :]
        z = _dot(l2, jnp.concatenate([yr, yi], axis=0).astype(BF16))
        z_ref[0, pl.ds(k1, n2, stride=Z_PITCH), :] = z[:n2]
        z_ref[1, pl.ds(k1, n2, stride=Z_PITCH), :] = z[n2:]
        return carry

    lax.fori_loop(0, n1, stage2, 0, unroll=FFT_UNROLL)

    wf = (_dot_f32(cs_ref[...], w_ref[...]) * ortho).astype(BF16)
    wr, wi = wf[:FOUR_GC], wf[FOUR_GC:]

    def stage3(k2, carry):
        base = pl.multiple_of(k2 * Z_PITCH, SUBLANES)
        zr = z_ref[0, pl.ds(base, n1), :].astype(BF16)
        zi = z_ref[1, pl.ds(base, n1), :].astype(BF16)
        o_ref[pl.ds(pl.multiple_of(k2 * n1, n1), n1), :] = (_dot(zr, wr) + _dot(zi, wi)).astype(o_ref.dtype)
        return carry

    lax.fori_loop(0, n2, stage3, 0, unroll=FFT_UNROLL)


def _fourier_mix(f, w_four):
    b, n, _ = f.shape
    l1, l2, cs = _fft_tables(n)
    l1 = jnp.asarray(l1).astype(BF16)
    l2 = jnp.asarray(l2).astype(BF16)
    cs = jnp.asarray(cs)
    ortho = float(1.0 / np.sqrt(n * FOUR_GC))
    slab = lambda bi, g: (bi, 0, g)
    return pl.pallas_call(
        functools.partial(_fft_kernel, ortho=ortho),
        grid=(b, FOUR_GROUPS),
        in_specs=[
            pl.BlockSpec((None, n, FOUR_GC), slab),
            _resident(l1.shape, lambda bi, g: (0, 0, 0)),
            _resident(l2.shape, lambda bi, g: (0, 0)),
            _resident(cs.shape, lambda bi, g: (0, 0)),
            pl.BlockSpec((None, FOUR_GC, FOUR_GC), lambda bi, g: (g, 0, 0)),
        ],
        out_specs=pl.BlockSpec((None, n, FOUR_GC), slab),
        out_shape=jax.ShapeDtypeStruct((b, n, FOUR_WIDTH), BF16),
        scratch_shapes=[
            pltpu.VMEM((FFT_N1 * X_PITCH, FOUR_GC), F32),
            pltpu.VMEM((FFT_N2 * Y1_PITCH, FOUR_GC), F32),
            pltpu.VMEM((2, FFT_N2 * Z_PITCH, FOUR_GC), F32),
        ],
        compiler_params=_params("parallel", "parallel"),
        name="fourier_mix",
    )(f, l1, l2, cs, w_four)


def _band_start(g, rows):
    return jnp.clip(NA_Q_ROWS * g - (NA_BAND_ROWS - NA_Q_ROWS) // 2, 0, rows - NA_BAND_ROWS)


def _patch_col_start(cb):
    return int(np.clip(NA_Q_COLS * cb - (NA_K_COLS - NA_Q_COLS) // 2, 0, GRID_W - NA_K_COLS))


def _natten_bias_tables(rpb, rows):
    heads = rpb.shape[0]
    groups = rows // NA_Q_ROWS
    n_dr, n_dc = 2 * NA_WIN_ROWS - 1, 2 * NA_WIN_COLS - 1
    n_cb = GRID_W // NA_Q_COLS
    nq, nk = NA_Q_ROWS * NA_Q_COLS, NA_BAND_ROWS * NA_K_COLS
    lane_tiles = 128 // NA_K_COLS
    lead = NA_WIN_ROWS
    strip_tiles = 2 * NA_BAND_ROWS
    col = np.arange(GRID_W)
    col_start = np.clip(col - NA_WIN_COLS // 2, 0, GRID_W - NA_WIN_COLS)
    col_ok = (col[None, :] >= col_start[:, None]) & (col[None, :] < col_start[:, None] + NA_WIN_COLS)
    dc = np.clip(col[None, :] - col[:, None] + (NA_WIN_COLS - 1), 0, n_dc - 1)
    onehot_dc = (dc[None] == np.arange(n_dc)[:, None, None]).astype(np.float32)
    tiles = jnp.einsum('hre,eqk->hrqk', rpb.astype(F32), jnp.asarray(onehot_dc), precision=lax.Precision.HIGHEST)
    tiles = jnp.where(jnp.asarray(col_ok), tiles * LOG2_E, MASK_VALUE)
    total_tiles = -(-(lead + n_dr + NA_BAND_ROWS) // lane_tiles) * lane_tiles
    tail = total_tiles - lead - n_dr
    tiles = jnp.pad(tiles, ((0, 0), (lead, tail), (0, 0), (0, 0)), constant_values=MASK_VALUE)
    strips = []
    for cb in range(n_cb):
        q0, k0 = NA_Q_COLS * cb, _patch_col_start(cb)
        assert col_start[q0] >= k0 and col_start[q0 + NA_Q_COLS - 1] + NA_WIN_COLS <= k0 + NA_K_COLS
        sub = tiles[:, :, q0:q0 + NA_Q_COLS, k0:k0 + NA_K_COLS].transpose(0, 2, 1, 3)
        strips.append(sub.reshape(heads, NA_Q_COLS, total_tiles * NA_K_COLS))
    strips = jnp.stack(strips, axis=1)

    plan = []
    for kind, g in enumerate((0, groups // 2, groups - 1)):
        band0 = int(np.clip(NA_Q_ROWS * g - (NA_BAND_ROWS - NA_Q_ROWS) // 2, 0, rows - NA_BAND_ROWS))
        for a in range(NA_Q_ROWS):
            qr = NA_Q_ROWS * g + a
            row_start = int(np.clip(qr - NA_WIN_ROWS // 2, 0, rows - NA_WIN_ROWS))
            assert band0 <= row_start and row_start + NA_WIN_ROWS <= band0 + NA_BAND_ROWS
            first_tile = band0 - qr + NA_WIN_ROWS - 1 + lead
            assert 0 <= first_tile and first_tile + NA_BAND_ROWS <= total_tiles
            for cb in range(n_cb):
                for hd in range(2):
                    plan.append((kind * n_cb + cb, hd, cb, a, first_tile, row_start - band0))

    def build(strip_ref, o_ref):
        patch_row = lax.broadcasted_iota(jnp.int32, (1, nk), 1) // NA_K_COLS
        for idx, hd, cb, a, first_tile, first_row in plan:
            lane0 = first_tile * NA_K_COLS
            val = strip_ref[hd, cb, :, lane0:lane0 + nk]
            in_window = (patch_row >= first_row) & (patch_row < first_row + NA_WIN_ROWS)
            r0 = hd * nq + a * NA_Q_COLS
            o_ref[idx, r0:r0 + NA_Q_COLS, :] = jnp.where(in_window, val, MASK_VALUE)

    return pl.pallas_call(
        build,
        grid=(heads // 2,),
        in_specs=[pl.BlockSpec((2,) + strips.shape[1:], lambda hp: (hp, 0, 0, 0))],
        out_specs=pl.BlockSpec((None, 3 * n_cb, 2 * nq, nk), lambda hp: (hp, 0, 0, 0)),
        out_shape=jax.ShapeDtypeStruct((heads // 2, 3 * n_cb, 2 * nq, nk), F32),
        compiler_params=_params("parallel"),
        name="bias_table",
    )(strips)


def _natten_kernel(q_ref, k_ref, v_ref, kc_ref, vc_ref, tab_ref, o_ref, *, rows):
    nq, nk = NA_Q_ROWS * NA_Q_COLS, NA_BAND_ROWS * NA_K_COLS
    groups = rows // NA_Q_ROWS
    n_cb = GRID_W // NA_Q_COLS
    first_head = lax.broadcasted_iota(jnp.int32, (1, 2 * NA_HEAD_DIM), 1) < NA_HEAD_DIM
    kc, vc = kc_ref[...].astype(BF16), vc_ref[...].astype(BF16)
    chains = [(g, cb) for g in range(NA_GROUPS_PER_STEP) for cb in range(n_cb)]

    def group_index(g):
        return pl.program_id(2) * NA_GROUPS_PER_STEP + g

    def q_rows(g, cb, a):
        start = (g * NA_Q_ROWS + a) * GRID_W + cb * NA_Q_COLS
        return slice(start, start + NA_Q_COLS)

    def patch(ref, g, cb):
        tok0 = _band_start(group_index(g), rows) * GRID_W + _patch_col_start(cb)
        parts = [ref[pl.ds(pl.multiple_of(tok0 + bb * GRID_W, SUBLANES), NA_K_COLS), :] for bb in range(NA_BAND_ROWS)]
        return jnp.concatenate(parts, axis=0).astype(BF16)

    def scores(g, cb):
        gi = group_index(g)
        kind = jnp.where(gi == 0, 0, jnp.where(gi == groups - 1, 2, 1))
        q2 = jnp.concatenate([q_ref[q_rows(g, cb, a), :] for a in range(NA_Q_ROWS)], axis=0)
        zero = jnp.zeros_like(q2)
        qs = jnp.concatenate([jnp.where(first_head, q2, zero), jnp.where(first_head, zero, q2)], axis=0)
        s_all = _dot_nt(qs, jnp.concatenate([patch(k_ref, g, cb), kc], axis=0))
        return s_all[:, :nk] + tab_ref[kind * n_cb + cb], s_all[:, nk:]

    def softmax(s_win, s_ctx):
        s_all = jnp.concatenate([s_win, s_ctx], axis=1)
        p = jnp.exp2(s_all - jnp.max(s_all, axis=-1, keepdims=True))
        return p.astype(BF16), jnp.sum(p, axis=-1, keepdims=True)

    def weighted_values(g, cb, p, l):
        values = jnp.concatenate([patch(v_ref, g, cb), vc], axis=0)
        o = _dot(p, values) / l
        o = jnp.where(first_head, o[:nq], o[nq:]).astype(o_ref.dtype)
        for a in range(NA_Q_ROWS):
            o_ref[q_rows(g, cb, a), :] = o[a * NA_Q_COLS:(a + 1) * NA_Q_COLS]

    s, p = {}, {}
    for step in range(len(chains) + 2):
        if step < len(chains):
            s[step] = scores(*chains[step])
        if 0 <= step - 1 < len(chains):
            p[step - 1] = softmax(*s.pop(step - 1))
        if 0 <= step - 2 < len(chains):
            weighted_values(*chains[step - 2], *p.pop(step - 2))


def _neighbourhood_attention(q, k, v, kc, vc, rpb):
    b, n, _ = q.shape
    rows = n // GRID_W
    steps = rows // (NA_Q_ROWS * NA_GROUPS_PER_STEP)
    ctx_len = kc.shape[1]
    tq = NA_GROUPS_PER_STEP * NA_Q_ROWS * GRID_W
    tables = _natten_bias_tables(rpb, rows)
    lanes = 2 * NA_HEAD_DIM
    seq = lambda bi, hp, i: (bi, 0, hp)
    return pl.pallas_call(
        functools.partial(_natten_kernel, rows=rows),
        grid=(b, NA_HEADS // 2, steps),
        in_specs=[
            pl.BlockSpec((None, tq, lanes), lambda bi, hp, i: (bi, i, hp)),
            pl.BlockSpec((None, n, lanes), seq),
            pl.BlockSpec((None, n, lanes), seq),
            pl.BlockSpec((None, ctx_len, lanes), seq),
            pl.BlockSpec((None, ctx_len, lanes), seq),
            pl.BlockSpec((None,) + tables.shape[1:], lambda bi, hp, i: (hp, 0, 0, 0)),
        ],
        out_specs=pl.BlockSpec((None, tq, lanes), lambda bi, hp, i: (bi, i, hp)),
        out_shape=jax.ShapeDtypeStruct((b, n, NA_WIDTH), BF16),
        compiler_params=_params("parallel", "parallel", "arbitrary"),
        name="neighbourhood_attention",
    )(q, k, v, kc, vc, tables)


def _mod_chunk(mod_ref, idx):
    return mod_ref[:, idx * D_MODEL:(idx + 1) * D_MODEL]


def _mlp_inputs(h, y, mod_ref, g2n_ref):
    h1 = h + _mod_chunk(mod_ref, 2) * y
    u = _norm_modulate(h1, g2n_ref[...], _mod_chunk(mod_ref, 3), _mod_chunk(mod_ref, 4)).astype(BF16)
    return h1, u


def _mlp_chunk(u, acc, c, w1_ref, w2_ref):
    lo, hi = c * MLP_CHUNK, (c + 1) * MLP_CHUNK
    a = jnp.maximum(_dot(u, w1_ref[:, lo:hi]), 0.0)
    part = _dot((a * a).astype(BF16), w2_ref[lo:hi, :])
    return part if acc is None else acc + part


def _pad_stages(stages, sizes=None):
    if sizes is None:
        sizes = [1] * (len(stages) - 1) + [0] * (MLP_CHUNKS - len(stages)) + [1]
    assert len(sizes) == MLP_CHUNKS and sum(sizes) == len(stages)

    def merged(group):
        def run():
            out = None
            for stage in group:
                out = stage()
            return out
        return run

    starts = [sum(sizes[:k]) for k in range(MLP_CHUNKS)]
    return [merged(stages[s0:s0 + n]) for s0, n in zip(starts, sizes)]


def _chunk_bounds(tm):
    assert sum(TAIL_CHUNK_ROWS) == tm
    starts = [sum(TAIL_CHUNK_ROWS[:i]) for i in range(len(TAIL_CHUNK_ROWS))]
    return list(zip(starts, TAIL_CHUNK_ROWS))


def _run_chains(prologues, bounds, mod_ref, w1_ref, w2_ref, o_ref):
    n = len(prologues)
    for stage in prologues[0][:-1]:
        stage()
    h1, u = prologues[0][-1]()
    for i, (r0, tc) in enumerate(bounds):
        acc, nxt = None, None
        for c in range(MLP_CHUNKS):
            acc = _mlp_chunk(u, acc, c, w1_ref, w2_ref)
            if i + 1 < n:
                nxt = prologues[i + 1][c]()
        o_ref[r0:r0 + tc, :] = h1 + _mod_chunk(mod_ref, 5) * acc
        if i + 1 < n:
            h1, u = nxt


def _even_tail_kernel(h_ref, of_ref, on_ref, mod_ref, g2n_ref, wo_hbm, w1_hbm, w2_hbm, o_ref,
                      wo_ref, w1_ref, w2_ref, wo_stage, w1_stage, w2_stage, sem, *, layer):
    _load_weights_once([wo_hbm.at[layer // 2], w1_hbm.at[layer], w2_hbm.at[layer]], [wo_ref, w1_ref, w2_ref],
                       [wo_stage, w1_stage, w2_stage], sem)
    bounds = _chunk_bounds(h_ref.shape[0])

    def prologue(r0, tc):
        rows = slice(r0, r0 + tc)
        st = {}

        def s0():
            st["y"] = _dot(of_ref[rows, :], wo_ref[:FOUR_WIDTH, :])

        def s1():
            st["y"] = st["y"] + _dot(on_ref[rows, :], wo_ref[FOUR_WIDTH:, :])

        def s3():
            return _mlp_inputs(h_ref[rows, :], st["y"], mod_ref, g2n_ref)

        return _pad_stages([s0, s1, s3])

    _run_chains([prologue(r0, tc) for r0, tc in bounds], bounds, mod_ref, w1_ref, w2_ref, o_ref)


def _even_tail(h, o_four, o_na, mod_rows, w_out, norm2_g, w1, w2, layer, tm):
    b, n, _ = h.shape
    tok = lambda bi, j: (bi, j, 0)
    const2 = lambda bi, j: (0, 0)
    return pl.pallas_call(
        functools.partial(_even_tail_kernel, layer=layer),
        grid=(b, n // tm),
        in_specs=[
            pl.BlockSpec((None, tm, D_MODEL), tok),
            pl.BlockSpec((None, tm, FOUR_WIDTH), tok),
            pl.BlockSpec((None, tm, NA_WIDTH), tok),
            pl.BlockSpec((None, 1, N_MOD * D_MODEL), lambda bi, j: (bi, 0, 0)),
            pl.BlockSpec((1, D_MODEL), const2),
            pl.BlockSpec(memory_space=pl.ANY),
            pl.BlockSpec(memory_space=pl.ANY),
            pl.BlockSpec(memory_space=pl.ANY),
        ],
        out_specs=pl.BlockSpec((None, tm, D_MODEL), tok),
        out_shape=jax.ShapeDtypeStruct(h.shape, F32),
        scratch_shapes=_weight_scratch([w_out.shape[1:], w1.shape[1:], w2.shape[1:]]),
        compiler_params=_params("arbitrary", "arbitrary"),
        name="even_tail",
    )(h, o_four, o_na, mod_rows, norm2_g, w_out, w1, w2)


HALO = 8


def _shift_rows(a, s):
    return pltpu.roll(a, s % a.shape[0], 0)


def _pool_kernel(hp_ref, h_ref, hn_ref, mod_ref, g1n_ref, ps_ref, g2n_ref, wp_hbm, w1_hbm, w2_hbm, o_ref,
                 wp_ref, w1_ref, w2_ref, wp_stage, w1_stage, w2_stage, sem, *, seq_len, layer):
    _load_weights_once([wp_hbm.at[layer // 2], w1_hbm.at[layer], w2_hbm.at[layer]], [wp_ref, w1_ref, w2_ref],
                       [wp_stage, w1_stage, w2_stage], sem)
    j = pl.program_id(1)
    tm = h_ref.shape[0]
    bounds = _chunk_bounds(tm)
    gc = POOL_GC
    gain, shift, scale = g1n_ref[...], _mod_chunk(mod_ref, 0), _mod_chunk(mod_ref, 1)
    unorm = lambda x: _norm_modulate(x, gain, shift, scale)

    def prologue(r0, tc):
        st = {}

        def normed():
            st["u"] = unorm(h_ref[r0:r0 + tc, :])

        def with_halo():
            if r0 == 0:
                u_prev = jnp.where(j > 0, unorm(hp_ref[...]), 0.0)
            else:
                u_prev = unorm(h_ref[r0 - HALO:r0, :])
            if r0 + tc == tm:
                u_next = jnp.where(j < pl.num_programs(1) - 1, unorm(hn_ref[...]), 0.0)
            else:
                u_next = unorm(h_ref[r0 + tc:r0 + tc + HALO, :])
            st["ue"] = jnp.concatenate([u_prev, st["u"], u_next], axis=0)

        def sum2():
            st["s2"] = st["ue"] + _shift_rows(st["ue"], 1)

        def sum4():
            s2 = st["s2"][:, gc:]
            st["s4"] = _shift_rows(s2, 1) + _shift_rows(s2, -1)

        def sum8_16():
            s4 = st["s4"][:, gc:]
            s8 = _shift_rows(s4, 2) + _shift_rows(s4, -2)
            s16 = _shift_rows(s8[:, gc:], 4) + _shift_rows(s8[:, gc:], -4)
            st["sums"] = (st["s2"][:, :gc], st["s4"][:, :gc], s8[:, :gc], s16)

        def group_linear(groups):
            def run():
                t = j * tm + r0 + lax.broadcasted_iota(jnp.int32, (tc, 1), 0)
                for gi in groups:
                    w = POOL_SIZES[gi]
                    lo = jnp.maximum(t - w // 2, 0)
                    hi = jnp.minimum(t + w - w // 2, seq_len)
                    inv_cnt = 1.0 / (hi - lo).astype(F32)
                    pooled = st["sums"][gi][HALO:HALO + tc] * inv_cnt - st["u"][:, gi * gc:(gi + 1) * gc]
                    st["y%d" % gi] = _dot(pooled.astype(BF16), wp_ref[gi * gc:(gi + 1) * gc, :]) * ps_ref[:, gi * gc:(gi + 1) * gc]
            return run

        def mlp_inputs():
            y = jnp.concatenate([st["y%d" % gi] for gi in range(len(POOL_SIZES))], axis=1)
            return _mlp_inputs(h_ref[r0:r0 + tc, :], y, mod_ref, g2n_ref)

        return _pad_stages([normed, with_halo, sum2, sum4, sum8_16, group_linear((0, 1)), group_linear((2, 3)),
                            mlp_inputs], sizes=POOL_STAGE_SIZES)

    _run_chains([prologue(r0, tc) for r0, tc in bounds], bounds, mod_ref, w1_ref, w2_ref, o_ref)


def _odd_layer(h, mod_rows, norm1_g, w_pool, pool_scale, norm2_g, w1, w2, layer, tm):
    b, n, _ = h.shape
    tok = lambda bi, j: (bi, j, 0)
    const2 = lambda bi, j: (0, 0)
    per_tile = tm // HALO
    last = n // HALO - 1
    return pl.pallas_call(
        functools.partial(_pool_kernel, seq_len=n, layer=layer),
        grid=(b, n // tm),
        in_specs=[
            pl.BlockSpec((None, HALO, D_MODEL), lambda bi, j: (bi, jnp.maximum(j * per_tile - 1, 0), 0)),
            pl.BlockSpec((None, tm, D_MODEL), tok),
            pl.BlockSpec((None, HALO, D_MODEL), lambda bi, j: (bi, jnp.minimum((j + 1) * per_tile, last), 0)),
            pl.BlockSpec((None, 1, N_MOD * D_MODEL), lambda bi, j: (bi, 0, 0)),
            pl.BlockSpec((1, D_MODEL), const2),
            pl.BlockSpec((1, D_MODEL), const2),
            pl.BlockSpec((1, D_MODEL), const2),
            pl.BlockSpec(memory_space=pl.ANY),
            pl.BlockSpec(memory_space=pl.ANY),
            pl.BlockSpec(memory_space=pl.ANY),
        ],
        out_specs=pl.BlockSpec((None, tm, D_MODEL), tok),
        out_shape=jax.ShapeDtypeStruct(h.shape, F32),
        scratch_shapes=_weight_scratch([w_pool.shape[1:], w1.shape[1:], w2.shape[1:]]),
        compiler_params=_params("arbitrary", "arbitrary"),
        name="odd_layer",
    )(h, h, h, mod_rows, norm1_g, pool_scale, norm2_g, w_pool, w1, w2)


def _head_segments():
    lanes = 256
    seg = (np.arange(lanes)[:, None] // NA_HEAD_DIM) == (np.arange(lanes)[None, :] // NA_HEAD_DIM)
    return jnp.asarray(seg.astype(np.float32)).astype(BF16)


def kernel(x, c, ctx, c_ctx, w_mod, b_mod, norm1_g, norm2_g, w_in_even, w_four, q_norm_g, k_norm_g, rpb, w_out_even, w_pool, pool_scale, w_mlp1, w_mlp2):
    b = x.shape[0]
    depth = w_mod.shape[0]
    assert depth == 2, "layer 0 is the even (Fourier + attention) layer, layer 1 the odd (pooling) layer"
    ctx_row = b
    pad = (-(b + 1)) % SUBLANES
    c_rows = jnp.concatenate([c, c_ctx[None, :], jnp.zeros((pad, D_MODEL), F32)], axis=0)
    mod = _modulation(c_rows, w_mod, b_mod)
    mod = mod.reshape(depth, mod.shape[1], 1, N_MOD * D_MODEL)

    row = lambda a: a.reshape(1, -1)
    heads = lambda g: jnp.tile(g, NA_HEADS).reshape(1, NA_WIDTH)
    seg = _head_segments()

    w_in = w_in_even[0].astype(BF16)
    qg, kg = heads(q_norm_g[0]), heads(k_norm_g[0])
    f, q, k, v = _in_projection(x, mod[0], lambda bi: bi, row(norm1_g[0]), w_in, qg, kg, seg, tm=1024)
    _, _, kc, vc = _in_projection(ctx, mod[0], lambda bi: ctx_row, row(norm1_g[0]), w_in, qg, kg, seg,
                                  tm=ctx.shape[1])
    o_four = _fourier_mix(f, w_four[0])
    o_na = _neighbourhood_attention(q, k, v, kc, vc, rpb[0])
    h = _even_tail(x, o_four, o_na, mod[0], w_out_even, row(norm2_g[0]), w_mlp1, w_mlp2, layer=0, tm=1024)

    w_pool_rows = w_pool.reshape(w_pool.shape[0], len(POOL_SIZES) * POOL_GC, POOL_GC)
    h = _odd_layer(h, mod[1], row(norm1_g[1]), w_pool_rows, row(pool_scale[0]), row(norm2_g[1]), w_mlp1, w_mlp2,
                   layer=1, tm=1024)
    return h
```

```python
import functools

import numpy as np
import jax
import jax.numpy as jnp
from jax import lax
from jax.experimental import pallas as pl
from jax.experimental.pallas import tpu as pltpu

D_MODEL = 1024
GRID_W = 64
FOUR_GROUPS = 4
FOUR_GC = 128
FOUR_WIDTH = 512
NA_HEADS = 8
NA_HEAD_DIM = 64
NA_WIDTH = 512
NA_WIN_ROWS = 8
NA_WIN_COLS = 16
POOL_SIZES = (2, 4, 8, 16)
POOL_GC = 256
MLP_HIDDEN = 4096
N_MOD = 6
RMS_EPS = 1e-6

VMEM_LIMIT_BYTES = 56 * 1024 * 1024
SUBLANES = 8
LANES = 128
WEIGHT_STAGE_BYTES = 512 * 1024
WEIGHT_STAGE_SLOTS = 8

F32 = jnp.float32
BF16 = jnp.bfloat16
MASK_VALUE = -1e30
LOG2_E = 1.4426950408889634

FFT_N1 = 128
FFT_N2 = 64
X_PITCH = FFT_N2 + 8
Y1_PITCH = 2 * FFT_N1 + 8
Z_PITCH = FFT_N1 + 8
FFT_UNROLL = 32

NA_Q_ROWS = 8
NA_BAND_ROWS = NA_Q_ROWS + NA_WIN_ROWS
NA_Q_COLS = 16
NA_K_COLS = NA_Q_COLS + NA_WIN_COLS
NA_GROUPS_PER_STEP = 8
TOKEN_TILE = 1024
TAIL_CHUNK_ROWS = (256, 256, 256, 256)
INPROJ_CHAIN_ROWS = 512
MLP_CHUNK = 1024
MLP_CHUNKS = MLP_HIDDEN // MLP_CHUNK
POOL_STAGE_SIZES = (2, 2, 2, 2)


def _dot(a, b):
    return jnp.dot(a, b, preferred_element_type=F32)


def _dot_f32(a, b):
    return jnp.dot(a, b, preferred_element_type=F32, precision=lax.Precision.HIGHEST)


def _dot_nt(a, b):
    return lax.dot_general(a, b, (((1,), (1,)), ((), ())), preferred_element_type=F32)


def _params(*sem):
    return pltpu.CompilerParams(dimension_semantics=sem, vmem_limit_bytes=VMEM_LIMIT_BYTES)


def _resident(shape, index_map):
    return pl.BlockSpec(shape, index_map, pipeline_mode=pl.Buffered(1))


def _stage_rows(shape):
    k, n = shape
    rows = min(k, max(2 * SUBLANES, WEIGHT_STAGE_BYTES // (4 * n)))
    assert k % rows == 0
    return rows


def _weight_scratch(shapes):
    scratch = [pltpu.VMEM(shape, BF16) for shape in shapes]
    for k, n in shapes:
        rows = _stage_rows((k, n))
        scratch.append(pltpu.VMEM((min(WEIGHT_STAGE_SLOTS, k // rows), rows, n), F32))
    return scratch + [pltpu.SemaphoreType.DMA((WEIGHT_STAGE_SLOTS * len(shapes),))]


def _load_weights_once(sources, weights, stages, sem):
    @pl.when((pl.program_id(0) == 0) & (pl.program_id(1) == 0))
    def _():
        for i, (src, dst, stage) in enumerate(zip(sources, weights, stages)):
            slots, rows = stage.shape[0], stage.shape[1]
            chunks = dst.shape[0] // rows
            ahead = max(slots - 1, 1)

            def copy(c):
                return pltpu.make_async_copy(src.at[pl.ds(c * rows, rows), :], stage.at[c % slots],
                                             sem.at[WEIGHT_STAGE_SLOTS * i + c % slots])

            for c in range(min(ahead, chunks)):
                copy(c).start()
            for c in range(chunks):
                copy(c).wait()
                dst[c * rows:(c + 1) * rows, :] = stage[c % slots].astype(BF16)
                if c + ahead < chunks:
                    copy(c + ahead).start()


def _mod_kernel(c_ref, w_ref, b_ref, o_ref):
    c = c_ref[...]
    s = c * (1.0 / (1.0 + jnp.exp(-c)))
    s_hi = s.astype(BF16)
    s_lo = (s - s_hi.astype(F32)).astype(BF16)
    w = w_ref[...].astype(BF16)
    o_ref[...] = _dot(s_hi, w) + _dot(s_lo, w) + b_ref[...]


def _modulation(c_rows, w_mod, b_mod):
    depth = w_mod.shape[0]
    rows = c_rows.shape[0]
    tn = 2 * D_MODEL
    return pl.pallas_call(
        _mod_kernel,
        grid=(depth, N_MOD * D_MODEL // tn),
        in_specs=[
            pl.BlockSpec((rows, D_MODEL), lambda l, j: (0, 0)),
            pl.BlockSpec((None, D_MODEL, tn), lambda l, j: (l, 0, j)),
            pl.BlockSpec((None, 1, tn), lambda l, j: (l, 0, j)),
        ],
        out_specs=pl.BlockSpec((None, rows, tn), lambda l, j: (l, 0, j)),
        out_shape=jax.ShapeDtypeStruct((depth, rows, N_MOD * D_MODEL), F32),
        compiler_params=_params("parallel", "parallel"),
        name="modulation",
    )(c_rows, w_mod, b_mod.reshape(depth, 1, N_MOD * D_MODEL))


def _norm_modulate(x, gain, shift, scale):
    ms = jnp.mean(x * x, axis=-1, keepdims=True)
    return (x * lax.rsqrt(ms + RMS_EPS)) * (gain * (1.0 + scale)) + shift


def _head_norm(a, gain, seg_ref):
    sq = (a * a).astype(BF16)
    half = seg_ref.shape[0]
    ms = jnp.concatenate([_dot(sq[:, :half], seg_ref[...]), _dot(sq[:, half:], seg_ref[...])], axis=1)
    return a * lax.rsqrt(ms * (1.0 / NA_HEAD_DIM) + RMS_EPS) * gain


def _inproj_kernel(x_ref, mod_ref, g_ref, w_ref, qg_ref, kg_ref, seg_ref, f_ref, q_ref, k_ref, v_ref):
    tm = x_ref.shape[0]
    chains = max(1, tm // INPROJ_CHAIN_ROWS)
    tc = tm // chains

    def normed(i):
        x = x_ref[i * tc:(i + 1) * tc, :]
        return _norm_modulate(x, g_ref[...], mod_ref[:, 0:D_MODEL], mod_ref[:, D_MODEL:2 * D_MODEL]).astype(BF16)

    def outputs(i, p):
        rows = slice(i * tc, (i + 1) * tc)
        f_ref[rows, :] = p[:, :FOUR_WIDTH]
        q = p[:, FOUR_WIDTH:FOUR_WIDTH + NA_WIDTH]
        k = p[:, FOUR_WIDTH + NA_WIDTH:FOUR_WIDTH + 2 * NA_WIDTH]
        q_ref[rows, :] = (_head_norm(q, qg_ref[...], seg_ref) * (NA_HEAD_DIM ** -0.5 * LOG2_E)).astype(BF16)
        k_ref[rows, :] = _head_norm(k, kg_ref[...], seg_ref)
        v_ref[rows, :] = p[:, FOUR_WIDTH + 2 * NA_WIDTH:]

    u, p = {}, {}
    for step in range(chains + 2):
        if step < chains:
            u[step] = normed(step)
        if 0 <= step - 1 < chains:
            p[step - 1] = _dot(u.pop(step - 1), w_ref[...])
        if 0 <= step - 2 < chains:
            outputs(step - 2, p.pop(step - 2))


def _in_projection(x, mod_rows, row_of_batch, norm_g, w_in, q_g, k_g, seg, tm):
    b, n, _ = x.shape
    width = w_in.shape[1]
    tok = lambda bi, j: (bi, j, 0)
    const2 = lambda bi, j: (0, 0)
    out_tok = lambda w: pl.BlockSpec((None, tm, w), tok)
    return pl.pallas_call(
        _inproj_kernel,
        grid=(b, n // tm),
        in_specs=[
            pl.BlockSpec((None, tm, D_MODEL), tok),
            pl.BlockSpec((None, 1, N_MOD * D_MODEL), lambda bi, j: (row_of_batch(bi), 0, 0)),
            pl.BlockSpec((1, D_MODEL), const2),
            _resident((D_MODEL, width), const2),
            pl.BlockSpec((1, NA_WIDTH), const2),
            pl.BlockSpec((1, NA_WIDTH), const2),
            pl.BlockSpec(seg.shape, const2),
        ],
        out_specs=[out_tok(FOUR_WIDTH), out_tok(NA_WIDTH), out_tok(NA_WIDTH), out_tok(NA_WIDTH)],
        out_shape=[
            jax.ShapeDtypeStruct((b, n, FOUR_WIDTH), F32),
            jax.ShapeDtypeStruct((b, n, NA_WIDTH), BF16),
            jax.ShapeDtypeStruct((b, n, NA_WIDTH), F32),
            jax.ShapeDtypeStruct((b, n, NA_WIDTH), F32),
        ],
        compiler_params=_params("parallel", "parallel"),
        name="in_projection",
    )(x, mod_rows, norm_g, w_in, q_g, k_g, seg)


def _fft_tables(n):
    n1, n2 = FFT_N1, FFT_N2
    assert n1 * n2 == n
    k1 = np.arange(n1)[None, :, None]
    pos = (n2 * np.arange(n1)[None, None, :] + np.arange(n2)[:, None, None])
    ang = 2.0 * np.pi * ((k1 * pos) % n) / n
    l1 = np.concatenate([np.cos(ang), -np.sin(ang)], axis=1)
    a2 = 2.0 * np.pi * ((np.arange(n2)[:, None] * np.arange(n2)[None, :]) % n2) / n2
    c2, s2 = np.cos(a2), np.sin(a2)
    l2 = np.block([[c2, s2], [-s2, c2]])
    ac = 2.0 * np.pi * ((np.arange(FOUR_GC)[:, None] * np.arange(FOUR_GC)[None, :]) % FOUR_GC) / FOUR_GC
    cs = np.concatenate([np.cos(ac), np.sin(ac)], axis=0)
    return l1.astype(np.float32), l2.astype(np.float32), cs.astype(np.float32)


def _fft_kernel(f_ref, l1_ref, l2_ref, cs_ref, w_ref, o_ref, x_ref, y1_ref, z_ref, *, ortho):
    n1, n2 = FFT_N1, FFT_N2

    def repitch(i, carry):
        x_ref[pl.ds(pl.multiple_of(i * X_PITCH, SUBLANES), n2), :] = f_ref[pl.ds(pl.multiple_of(i * n2, n2), n2), :]
        return carry

    lax.fori_loop(0, n1, repitch, 0, unroll=FFT_UNROLL)

    def stage1(j, carry):
        r = x_ref[pl.ds(j, n1, stride=X_PITCH), :].astype(BF16)
        y1_ref[pl.ds(pl.multiple_of(j * Y1_PITCH, SUBLANES), 2 * n1), :] = _dot(l1_ref[j], r)
        return carry

    lax.fori_loop(0, n2, stage1, 0, unroll=FFT_UNROLL)

    l2 = l2_ref[...]

    def stage2(k1, carry):
        yr = y1_ref[pl.ds(k1, n2, stride=Y1_PITCH), :]
        yi = y1_ref[pl.ds(n1 + k1, n2, stride=Y1_PITCH), :]
        z = _dot(l2, jnp.concatenate([yr, yi], axis=0).astype(BF16))
        z_ref[0, pl.ds(k1, n2, stride=Z_PITCH), :] = z[:n2]
        z_ref[1, pl.ds(k1, n2, stride=Z_PITCH), :] = z[n2:]
        return carry

    lax.fori_loop(0, n1, stage2, 0, unroll=FFT_UNROLL)

    wf = (_dot_f32(cs_ref[...], w_ref[...]) * ortho).astype(BF16)
    wr, wi = wf[:FOUR_GC], wf[FOUR_GC:]

    def stage3(k2, carry):
        base = pl.multiple_of(k2 * Z_PITCH, SUBLANES)
        zr = z_ref[0, pl.ds(base, n1), :].astype(BF16)
        zi = z_ref[1, pl.ds(base, n1), :].astype(BF16)
        o_ref[pl.ds(pl.multiple_of(k2 * n1, n1), n1), :] = (_dot(zr, wr) + _dot(zi, wi)).astype(o_ref.dtype)
        return carry

    lax.fori_loop(0, n2, stage3, 0, unroll=FFT_UNROLL)


def _fourier_mix(f, w_four):
    b, n, _ = f.shape
    l1, l2, cs = _fft_tables(n)
    l1 = jnp.asarray(l1).astype(BF16)
    l2 = jnp.asarray(l2).astype(BF16)
    cs = jnp.asarray(cs)
    ortho = float(1.0 / np.sqrt(n * FOUR_GC))
    slab = lambda bi, g: (bi, 0, g)
    return pl.pallas_call(
        functools.partial(_fft_kernel, ortho=ortho),
        grid=(b, FOUR_GROUPS),
        in_specs=[
            pl.BlockSpec((None, n, FOUR_GC), slab),
            _resident(l1.shape, lambda bi, g: (0, 0, 0)),
            _resident(l2.shape, lambda bi, g: (0, 0)),
            _resident(cs.shape, lambda bi, g: (0, 0)),
            pl.BlockSpec((None, FOUR_GC, FOUR_GC), lambda bi, g: (g, 0, 0)),
        ],
        out_specs=pl.BlockSpec((None, n, FOUR_GC), slab),
        out_shape=jax.ShapeDtypeStruct((b, n, FOUR_WIDTH), BF16),
        scratch_shapes=[
            pltpu.VMEM((FFT_N1 * X_PITCH, FOUR_GC), F32),
            pltpu.VMEM((FFT_N2 * Y1_PITCH, FOUR_GC), F32),
            pltpu.VMEM((2, FFT_N2 * Z_PITCH, FOUR_GC), F32),
        ],
        compiler_params=_params("parallel", "parallel"),
        name="fourier_mix",
    )(f, l1, l2, cs, w_four)


def _band_start(g, rows):
    return jnp.clip(NA_Q_ROWS * g - (NA_BAND_ROWS - NA_Q_ROWS) // 2, 0, rows - NA_BAND_ROWS)


def _patch_col_start(cb):
    return int(np.clip(NA_Q_COLS * cb - (NA_K_COLS - NA_Q_COLS) // 2, 0, GRID_W - NA_K_COLS))


def _natten_bias_tables(rpb, rows):
    heads = rpb.shape[0]
    groups = rows // NA_Q_ROWS
    n_dr, n_dc = 2 * NA_WIN_ROWS - 1, 2 * NA_WIN_COLS - 1
    n_cb = GRID_W // NA_Q_COLS
    nq, nk = NA_Q_ROWS * NA_Q_COLS, NA_BAND_ROWS * NA_K_COLS
    lane_tiles = LANES // NA_K_COLS
    lead = NA_WIN_ROWS
    strip_tiles = 2 * NA_BAND_ROWS
    col = np.arange(GRID_W)
    col_start = np.clip(col - NA_WIN_COLS // 2, 0, GRID_W - NA_WIN_COLS)
    col_ok = (col[None, :] >= col_start[:, None]) & (col[None, :] < col_start[:, None] + NA_WIN_COLS)
    dc = np.clip(col[None, :] - col[:, None] + (NA_WIN_COLS - 1), 0, n_dc - 1)
    onehot_dc = (dc[None] == np.arange(n_dc)[:, None, None]).astype(np.float32)
    tiles = jnp.einsum('hre,eqk->hrqk', rpb.astype(F32), jnp.asarray(onehot_dc), precision=lax.Precision.HIGHEST)
    tiles = jnp.where(jnp.asarray(col_ok), tiles * LOG2_E, MASK_VALUE)
    total_tiles = -(-(lead + n_dr + NA_BAND_ROWS) // lane_tiles) * lane_tiles
    tail = total_tiles - lead - n_dr
    tiles = jnp.pad(tiles, ((0, 0), (lead, tail), (0, 0), (0, 0)), constant_values=MASK_VALUE)
    strips = []
    for cb in range(n_cb):
        q0, k0 = NA_Q_COLS * cb, _patch_col_start(cb)
        assert col_start[q0] >= k0 and col_start[q0 + NA_Q_COLS - 1] + NA_WIN_COLS <= k0 + NA_K_COLS
        sub = tiles[:, :, q0:q0 + NA_Q_COLS, k0:k0 + NA_K_COLS].transpose(0, 2, 1, 3)
        strips.append(sub.reshape(heads, NA_Q_COLS, total_tiles * NA_K_COLS))
    strips = jnp.stack(strips, axis=1)

    plan = []
    for kind, g in enumerate((0, groups // 2, groups - 1)):
        band0 = int(np.clip(NA_Q_ROWS * g - (NA_BAND_ROWS - NA_Q_ROWS) // 2, 0, rows - NA_BAND_ROWS))
        for a in range(NA_Q_ROWS):
            qr = NA_Q_ROWS * g + a
            row_start = int(np.clip(qr - NA_WIN_ROWS // 2, 0, rows - NA_WIN_ROWS))
            assert band0 <= row_start and row_start + NA_WIN_ROWS <= band0 + NA_BAND_ROWS
            first_tile = band0 - qr + NA_WIN_ROWS - 1 + lead
            assert 0 <= first_tile and first_tile + NA_BAND_ROWS <= total_tiles
            for cb in range(n_cb):
                for hd in range(2):
                    plan.append((kind * n_cb + cb, hd, cb, a, first_tile, row_start - band0))

    def build(strip_ref, o_ref):
        patch_row = lax.broadcasted_iota(jnp.int32, (1, nk), 1) // NA_K_COLS
        for idx, hd, cb, a, first_tile, first_row in plan:
            lane0 = first_tile * NA_K_COLS
            val = strip_ref[hd, cb, :, lane0:lane0 + nk]
            in_window = (patch_row >= first_row) & (patch_row < first_row + NA_WIN_ROWS)
            r0 = hd * nq + a * NA_Q_COLS
            o_ref[idx, r0:r0 + NA_Q_COLS, :] = jnp.where(in_window, val, MASK_VALUE)

    return pl.pallas_call(
        build,
        grid=(heads // 2,),
        in_specs=[pl.BlockSpec((2,) + strips.shape[1:], lambda hp: (hp, 0, 0, 0))],
        out_specs=pl.BlockSpec((None, 3 * n_cb, 2 * nq, nk), lambda hp: (hp, 0, 0, 0)),
        out_shape=jax.ShapeDtypeStruct((heads // 2, 3 * n_cb, 2 * nq, nk), F32),
        compiler_params=_params("parallel"),
        name="bias_table",
    )(strips)


def _natten_kernel(q_ref, k_ref, v_ref, kc_ref, vc_ref, tab_ref, o_ref, *, rows):
    nq, nk = NA_Q_ROWS * NA_Q_COLS, NA_BAND_ROWS * NA_K_COLS
    groups = rows // NA_Q_ROWS
    n_cb = GRID_W // NA_Q_COLS
    first_head = lax.broadcasted_iota(jnp.int32, (1, 2 * NA_HEAD_DIM), 1) < NA_HEAD_DIM
    kc, vc = kc_ref[...].astype(BF16), vc_ref[...].astype(BF16)
    chains = [(g, cb) for g in range(NA_GROUPS_PER_STEP) for cb in range(n_cb)]

    def group_index(g):
        return pl.program_id(2) * NA_GROUPS_PER_STEP + g

    def q_rows(g, cb, a):
        start = (g * NA_Q_ROWS + a) * GRID_W + cb * NA_Q_COLS
        return slice(start, start + NA_Q_COLS)

    def patch(ref, g, cb):
        tok0 = _band_start(group_index(g), rows) * GRID_W + _patch_col_start(cb)
        parts = [ref[pl.ds(pl.multiple_of(tok0 + bb * GRID_W, SUBLANES), NA_K_COLS), :] for bb in range(NA_BAND_ROWS)]
        return jnp.concatenate(parts, axis=0).astype(BF16)

    def scores(g, cb):
        gi = group_index(g)
        kind = jnp.where(gi == 0, 0, jnp.where(gi == groups - 1, 2, 1))
        q2 = jnp.concatenate([q_ref[q_rows(g, cb, a), :] for a in range(NA_Q_ROWS)], axis=0)
        zero = jnp.zeros_like(q2)
        qs = jnp.concatenate([jnp.where(first_head, q2, zero), jnp.where(first_head, zero, q2)], axis=0)
        s_all = _dot_nt(qs, jnp.concatenate([patch(k_ref, g, cb), kc], axis=0))
        return s_all[:, :nk] + tab_ref[kind * n_cb + cb], s_all[:, nk:]

    def softmax(s_win, s_ctx):
        s_all = jnp.concatenate([s_win, s_ctx], axis=1)
        p = jnp.exp2(s_all - jnp.max(s_all, axis=-1, keepdims=True))
        return p.astype(BF16), jnp.sum(p, axis=-1, keepdims=True)

    def weighted_values(g, cb, p, l):
        values = jnp.concatenate([patch(v_ref, g, cb), vc], axis=0)
        o = _dot(p, values) / l
        o = jnp.where(first_head, o[:nq], o[nq:]).astype(o_ref.dtype)
        for a in range(NA_Q_ROWS):
            o_ref[q_rows(g, cb, a), :] = o[a * NA_Q_COLS:(a + 1) * NA_Q_COLS]

    s, p = {}, {}
    for step in range(len(chains) + 2):
        if step < len(chains):
            s[step] = scores(*chains[step])
        if 0 <= step - 1 < len(chains):
            p[step - 1] = softmax(*s.pop(step - 1))
        if 0 <= step - 2 < len(chains):
            weighted_values(*chains[step - 2], *p.pop(step - 2))


def _neighbourhood_attention(q, k, v, kc, vc, rpb):
    b, n, _ = q.shape
    rows = n // GRID_W
    steps = rows // (NA_Q_ROWS * NA_GROUPS_PER_STEP)
    ctx_len = kc.shape[1]
    tq = NA_GROUPS_PER_STEP * NA_Q_ROWS * GRID_W
    tables = _natten_bias_tables(rpb, rows)
    lanes = 2 * NA_HEAD_DIM
    seq = lambda bi, hp, i: (bi, 0, hp)
    return pl.pallas_call(
        functools.partial(_natten_kernel, rows=rows),
        grid=(b, NA_HEADS // 2, steps),
        in_specs=[
            pl.BlockSpec((None, tq, lanes), lambda bi, hp, i: (bi, i, hp)),
            pl.BlockSpec((None, n, lanes), seq),
            pl.BlockSpec((None, n, lanes), seq),
            pl.BlockSpec((None, ctx_len, lanes), seq),
            pl.BlockSpec((None, ctx_len, lanes), seq),
            pl.BlockSpec((None,) + tables.shape[1:], lambda bi, hp, i: (hp, 0, 0, 0)),
        ],
        out_specs=pl.BlockSpec((None, tq, lanes), lambda bi, hp, i: (bi, i, hp)),
        out_shape=jax.ShapeDtypeStruct((b, n, NA_WIDTH), BF16),
        compiler_params=_params("parallel", "parallel", "arbitrary"),
        name="neighbourhood_attention",
    )(q, k, v, kc, vc, tables)


def _mod_chunk(mod_ref, idx):
    return mod_ref[:, idx * D_MODEL:(idx + 1) * D_MODEL]


def _mlp_inputs(h, y, mod_ref, g2n_ref):
    h1 = h + _mod_chunk(mod_ref, 2) * y
    u = _norm_modulate(h1, g2n_ref[...], _mod_chunk(mod_ref, 3), _mod_chunk(mod_ref, 4)).astype(BF16)
    return h1, u


def _mlp_chunk(u, acc, c, w1_ref, w2_ref):
    lo, hi = c * MLP_CHUNK, (c + 1) * MLP_CHUNK
    a = jnp.maximum(_dot(u, w1_ref[:, lo:hi]), 0.0)
    part = _dot((a * a).astype(BF16), w2_ref[lo:hi, :])
    return part if acc is None else acc + part


def _pad_stages(stages, sizes=None):
    if sizes is None:
        sizes = [1] * (len(stages) - 1) + [0] * (MLP_CHUNKS - len(stages)) + [1]
    assert len(sizes) == MLP_CHUNKS and sum(sizes) == len(stages)

    def merged(group):
        def run():
            out = None
            for stage in group:
                out = stage()
            return out
        return run

    starts = [sum(sizes[:k]) for k in range(MLP_CHUNKS)]
    return [merged(stages[s0:s0 + n]) for s0, n in zip(starts, sizes)]


def _chunk_bounds(tm):
    assert sum(TAIL_CHUNK_ROWS) == tm
    starts = [sum(TAIL_CHUNK_ROWS[:i]) for i in range(len(TAIL_CHUNK_ROWS))]
    return list(zip(starts, TAIL_CHUNK_ROWS))


def _run_chains(prologues, bounds, mod_ref, w1_ref, w2_ref, o_ref):
    n = len(prologues)
    for stage in prologues[0][:-1]:
        stage()
    h1, u = prologues[0][-1]()
    for i, (r0, tc) in enumerate(bounds):
        acc, nxt = None, None
        for c in range(MLP_CHUNKS):
            acc = _mlp_chunk(u, acc, c, w1_ref, w2_ref)
            if i + 1 < n:
                nxt = prologues[i + 1][c]()
        o_ref[r0:r0 + tc, :] = h1 + _mod_chunk(mod_ref, 5) * acc
        if i + 1 < n:
            h1, u = nxt


def _even_tail_kernel(h_ref, of_ref, on_ref, mod_ref, g2n_ref, wo_hbm, w1_hbm, w2_hbm, o_ref,
                      wo_ref, w1_ref, w2_ref, wo_stage, w1_stage, w2_stage, sem, *, layer):
    _load_weights_once([wo_hbm.at[layer // 2], w1_hbm.at[layer], w2_hbm.at[layer]], [wo_ref, w1_ref, w2_ref],
                       [wo_stage, w1_stage, w2_stage], sem)
    bounds = _chunk_bounds(h_ref.shape[0])

    def prologue(r0, tc):
        rows = slice(r0, r0 + tc)
        st = {}

        def s0():
            st["y"] = _dot(of_ref[rows, :], wo_ref[:FOUR_WIDTH, :])

        def s1():
            st["y"] = st["y"] + _dot(on_ref[rows, :], wo_ref[FOUR_WIDTH:, :])

        def s3():
            return _mlp_inputs(h_ref[rows, :], st["y"], mod_ref, g2n_ref)

        return _pad_stages([s0, s1, s3])

    _run_chains([prologue(r0, tc) for r0, tc in bounds], bounds, mod_ref, w1_ref, w2_ref, o_ref)


def _even_tail(h, o_four, o_na, mod_rows, w_out, norm2_g, w1, w2, layer, tm):
    b, n, _ = h.shape
    tok = lambda bi, j: (bi, j, 0)
    const2 = lambda bi, j: (0, 0)
    return pl.pallas_call(
        functools.partial(_even_tail_kernel, layer=layer),
        grid=(b, n // tm),
        in_specs=[
            pl.BlockSpec((None, tm, D_MODEL), tok),
            pl.BlockSpec((None, tm, FOUR_WIDTH), tok),
            pl.BlockSpec((None, tm, NA_WIDTH), tok),
            pl.BlockSpec((None, 1, N_MOD * D_MODEL), lambda bi, j: (bi, 0, 0)),
            pl.BlockSpec((1, D_MODEL), const2),
            pl.BlockSpec(memory_space=pl.ANY),
            pl.BlockSpec(memory_space=pl.ANY),
            pl.BlockSpec(memory_space=pl.ANY),
        ],
        out_specs=pl.BlockSpec((None, tm, D_MODEL), tok),
        out_shape=jax.ShapeDtypeStruct(h.shape, F32),
        scratch_shapes=_weight_scratch([w_out.shape[1:], w1.shape[1:], w2.shape[1:]]),
        compiler_params=_params("arbitrary", "arbitrary"),
        name="even_tail",
    )(h, o_four, o_na, mod_rows, norm2_g, w_out, w1, w2)


HALO = 8


def _shift_rows(a, s):
    return pltpu.roll(a, s % a.shape[0], 0)


def _pool_kernel(hp_ref, h_ref, hn_ref, mod_ref, g1n_ref, ps_ref, g2n_ref, wp_hbm, w1_hbm, w2_hbm, o_ref,
                 wp_ref, w1_ref, w2_ref, wp_stage, w1_stage, w2_stage, sem, *, seq_len, layer):
    _load_weights_once([wp_hbm.at[layer // 2], w1_hbm.at[layer], w2_hbm.at[layer]], [wp_ref, w1_ref, w2_ref],
                       [wp_stage, w1_stage, w2_stage], sem)
    j = pl.program_id(1)
    tm = h_ref.shape[0]
    bounds = _chunk_bounds(tm)
    gc = POOL_GC
    gain, shift, scale = g1n_ref[...], _mod_chunk(mod_ref, 0), _mod_chunk(mod_ref, 1)
    unorm = lambda x: _norm_modulate(x, gain, shift, scale)

    def prologue(r0, tc):
        st = {}

        def normed():
            st["u"] = unorm(h_ref[r0:r0 + tc, :])

        def with_halo():
            if r0 == 0:
                u_prev = jnp.where(j > 0, unorm(hp_ref[...]), 0.0)
            else:
                u_prev = unorm(h_ref[r0 - HALO:r0, :])
            if r0 + tc == tm:
                u_next = jnp.where(j < pl.num_programs(1) - 1, unorm(hn_ref[...]), 0.0)
            else:
                u_next = unorm(h_ref[r0 + tc:r0 + tc + HALO, :])
            st["ue"] = jnp.concatenate([u_prev, st["u"], u_next], axis=0)

        def sum2():
            st["s2"] = st["ue"] + _shift_rows(st["ue"], 1)

        def sum4():
            s2 = st["s2"][:, gc:]
            st["s4"] = _shift_rows(s2, 1) + _shift_rows(s2, -1)

        def sum8_16():
            s4 = st["s4"][:, gc:]
            s8 = _shift_rows(s4, 2) + _shift_rows(s4, -2)
            s16 = _shift_rows(s8[:, gc:], 4) + _shift_rows(s8[:, gc:], -4)
            st["sums"] = (st["s2"][:, :gc], st["s4"][:, :gc], s8[:, :gc], s16)

        def group_linear(groups):
            def run():
                t = j * tm + r0 + lax.broadcasted_iota(jnp.int32, (tc, 1), 0)
                for gi in groups:
                    w = POOL_SIZES[gi]
                    lo = jnp.maximum(t - w // 2, 0)
                    hi = jnp.minimum(t + w - w // 2, seq_len)
                    inv_cnt = 1.0 / (hi - lo).astype(F32)
                    pooled = st["sums"][gi][HALO:HALO + tc] * inv_cnt - st["u"][:, gi * gc:(gi + 1) * gc]
                    st["y%d" % gi] = _dot(pooled.astype(BF16), wp_ref[gi * gc:(gi + 1) * gc, :]) * ps_ref[:, gi * gc:(gi + 1) * gc]
            return run

        def mlp_inputs():
            y = jnp.concatenate([st["y%d" % gi] for gi in range(len(POOL_SIZES))], axis=1)
            return _mlp_inputs(h_ref[r0:r0 + tc, :], y, mod_ref, g2n_ref)

        return _pad_stages([normed, with_halo, sum2, sum4, sum8_16, group_linear((0, 1)), group_linear((2, 3)),
                            mlp_inputs], sizes=POOL_STAGE_SIZES)

    _run_chains([prologue(r0, tc) for r0, tc in bounds], bounds, mod_ref, w1_ref, w2_ref, o_ref)


def _odd_layer(h, mod_rows, norm1_g, w_pool, pool_scale, norm2_g, w1, w2, layer, tm):
    b, n, _ = h.shape
    tok = lambda bi, j: (bi, j, 0)
    const2 = lambda bi, j: (0, 0)
    per_tile = tm // HALO
    last = n // HALO - 1
    return pl.pallas_call(
        functools.partial(_pool_kernel, seq_len=n, layer=layer),
        grid=(b, n // tm),
        in_specs=[
            pl.BlockSpec((None, HALO, D_MODEL), lambda bi, j: (bi, jnp.maximum(j * per_tile - 1, 0), 0)),
            pl.BlockSpec((None, tm, D_MODEL), tok),
            pl.BlockSpec((None, HALO, D_MODEL), lambda bi, j: (bi, jnp.minimum((j + 1) * per_tile, last), 0)),
            pl.BlockSpec((None, 1, N_MOD * D_MODEL), lambda bi, j: (bi, 0, 0)),
            pl.BlockSpec((1, D_MODEL), const2),
            pl.BlockSpec((1, D_MODEL), const2),
            pl.BlockSpec((1, D_MODEL), const2),
            pl.BlockSpec(memory_space=pl.ANY),
            pl.BlockSpec(memory_space=pl.ANY),
            pl.BlockSpec(memory_space=pl.ANY),
        ],
        out_specs=pl.BlockSpec((None, tm, D_MODEL), tok),
        out_shape=jax.ShapeDtypeStruct(h.shape, F32),
        scratch_shapes=_weight_scratch([w_pool.shape[1:], w1.shape[1:], w2.shape[1:]]),
        compiler_params=_params("arbitrary", "arbitrary"),
        name="odd_layer",
    )(h, h, h, mod_rows, norm1_g, pool_scale, norm2_g, w_pool, w1, w2)


def _head_segments():
    lanes = 256
    seg = (np.arange(lanes)[:, None] // NA_HEAD_DIM) == (np.arange(lanes)[None, :] // NA_HEAD_DIM)
    return jnp.asarray(seg.astype(np.float32)).astype(BF16)


def kernel(x, c, ctx, c_ctx, w_mod, b_mod, norm1_g, norm2_g, w_in_even, w_four, q_norm_g, k_norm_g, rpb, w_out_even, w_pool, pool_scale, w_mlp1, w_mlp2):
    b = x.shape[0]
    depth = w_mod.shape[0]
    assert depth == 2, "layer 0 is the even (Fourier + attention) layer, layer 1 the odd (pooling) layer"
    ctx_row = b
    pad = (-(b + 1)) % SUBLANES
    c_rows = jnp.concatenate([c, c_ctx[None, :], jnp.zeros((pad, D_MODEL), F32)], axis=0)
    mod = _modulation(c_rows, w_mod, b_mod)
    mod = mod.reshape(depth, mod.shape[1], 1, N_MOD * D_MODEL)

    row = lambda a: a.reshape(1, -1)
    heads = lambda g: jnp.tile(g, NA_HEADS).reshape(1, NA_WIDTH)
    seg = _head_segments()

    w_in = w_in_even[0].astype(BF16)
    qg, kg = heads(q_norm_g[0]), heads(k_norm_g[0])
    f, q, k, v = _in_projection(x, mod[0], lambda bi: bi, row(norm1_g[0]), w_in, qg, kg, seg, tm=TOKEN_TILE)
    _, _, kc, vc = _in_projection(ctx, mod[0], lambda bi: ctx_row, row(norm1_g[0]), w_in, qg, kg, seg,
                                  tm=ctx.shape[1])
    o_four = _fourier_mix(f, w_four[0])
    o_na = _neighbourhood_attention(q, k, v, kc, vc, rpb[0])
    h = _even_tail(x, o_four, o_na, mod[0], w_out_even, row(norm2_g[0]), w_mlp1, w_mlp2, layer=0, tm=TOKEN_TILE)

    w_pool_rows = w_pool.reshape(w_pool.shape[0], len(POOL_SIZES) * POOL_GC, POOL_GC)
    h = _odd_layer(h, mod[1], row(norm1_g[1]), w_pool_rows, row(pool_scale[0]), row(norm2_g[1]), w_mlp1, w_mlp2,
                   layer=1, tm=TOKEN_TILE)
    return h
```

```python
import functools

import numpy as np
import jax
import jax.numpy as jnp
from jax import lax
from jax.experimental import pallas as pl
from jax.experimental.pallas import tpu as pltpu

D_MODEL = 1024
GRID_W = 64
FOUR_GROUPS = 4
FOUR_GC = 128
FOUR_WIDTH = 512
NA_HEADS = 8
NA_HEAD_DIM = 64
NA_WIDTH = 512
NA_WIN_ROWS = 8
NA_WIN_COLS = 16
POOL_SIZES = (2, 4, 8, 16)
POOL_GC = 256
MLP_HIDDEN = 4096
N_MOD = 6
RMS_EPS = 1e-6

VMEM_LIMIT_BYTES = 56 * 1024 * 1024
SUBLANES = 8
LANES = 128
WEIGHT_STAGE_BYTES = 512 * 1024
WEIGHT_STAGE_SLOTS = 8

F32 = jnp.float32
BF16 = jnp.bfloat16
MASK_VALUE = -1e30
LOG2_E = 1.4426950408889634

FFT_N1 = 128
FFT_N2 = 64
X_PITCH = FFT_N2 + 8
Y1_PITCH = 2 * FFT_N1 + 8
Z_PITCH = FFT_N1 + 8
FFT_UNROLL = 32

NA_Q_ROWS = 8
NA_BAND_ROWS = NA_Q_ROWS + NA_WIN_ROWS
NA_Q_COLS = 16
NA_K_COLS = NA_Q_COLS + NA_WIN_COLS
NA_GROUPS_PER_STEP = 8
TOKEN_TILE = 1024
TAIL_CHUNK_ROWS = (256, 256, 256, 256)
INPROJ_CHAIN_ROWS = 512
MLP_CHUNK = 1024
MLP_CHUNKS = MLP_HIDDEN // MLP_CHUNK
POOL_STAGE_SIZES = (2, 2, 2, 2)


def _dot(a, b):
    return jnp.dot(a, b, preferred_element_type=F32)


def _dot_f32(a, b):
    return jnp.dot(a, b, preferred_element_type=F32, precision=lax.Precision.HIGHEST)


def _dot_nt(a, b):
    return lax.dot_general(a, b, (((1,), (1,)), ((), ())), preferred_element_type=F32)


def _params(*sem):
    return pltpu.CompilerParams(dimension_semantics=sem, vmem_limit_bytes=VMEM_LIMIT_BYTES)


def _resident(shape, index_map):
    return pl.BlockSpec(shape, index_map, pipeline_mode=pl.Buffered(1))


def _stage_rows(shape):
    k, n = shape
    rows = min(k, max(2 * SUBLANES, WEIGHT_STAGE_BYTES // (4 * n)))
    assert k % rows == 0
    return rows


def _weight_scratch(shapes):
    scratch = [pltpu.VMEM(shape, BF16) for shape in shapes]
    for k, n in shapes:
        rows = _stage_rows((k, n))
        scratch.append(pltpu.VMEM((min(WEIGHT_STAGE_SLOTS, k // rows), rows, n), F32))
    return scratch + [pltpu.SemaphoreType.DMA((WEIGHT_STAGE_SLOTS * len(shapes),))]


def _load_weights_once(sources, weights, stages, sem):
    @pl.when((pl.program_id(0) == 0) & (pl.program_id(1) == 0))
    def _():
        for i, (src, dst, stage) in enumerate(zip(sources, weights, stages)):
            slots, rows = stage.shape[0], stage.shape[1]
            chunks = dst.shape[0] // rows
            ahead = max(slots - 1, 1)

            def copy(c):
                return pltpu.make_async_copy(src.at[pl.ds(c * rows, rows), :], stage.at[c % slots],
                                             sem.at[WEIGHT_STAGE_SLOTS * i + c % slots])

            for c in range(min(ahead, chunks)):
                copy(c).start()
            for c in range(chunks):
                copy(c).wait()
                dst[c * rows:(c + 1) * rows, :] = stage[c % slots].astype(BF16)
                if c + ahead < chunks:
                    copy(c + ahead).start()


def _mod_kernel(c_ref, w_ref, b_ref, o_ref):
    c = c_ref[...]
    s = c * (1.0 / (1.0 + jnp.exp(-c)))
    s_hi = s.astype(BF16)
    s_lo = (s - s_hi.astype(F32)).astype(BF16)
    rows = s.shape[0]
    r = _dot(jnp.concatenate([s_hi, s_lo], axis=0), w_ref[...].astype(BF16))
    o_ref[...] = r[:rows] + r[rows:] + b_ref[...]


def _modulation(c_rows, w_mod, b_mod):
    depth = w_mod.shape[0]
    rows = c_rows.shape[0]
    tn = 2 * D_MODEL
    return pl.pallas_call(
        _mod_kernel,
        grid=(depth, N_MOD * D_MODEL // tn),
        in_specs=[
            pl.BlockSpec((rows, D_MODEL), lambda l, j: (0, 0)),
            pl.BlockSpec((None, D_MODEL, tn), lambda l, j: (l, 0, j)),
            pl.BlockSpec((None, 1, tn), lambda l, j: (l, 0, j)),
        ],
        out_specs=pl.BlockSpec((None, rows, tn), lambda l, j: (l, 0, j)),
        out_shape=jax.ShapeDtypeStruct((depth, rows, N_MOD * D_MODEL), F32),
        compiler_params=_params("parallel", "parallel"),
        name="modulation",
    )(c_rows, w_mod, b_mod.reshape(depth, 1, N_MOD * D_MODEL))


def _norm_modulate(x, gain, shift, scale):
    ms = jnp.mean(x * x, axis=-1, keepdims=True)
    return (x * lax.rsqrt(ms + RMS_EPS)) * (gain * (1.0 + scale)) + shift


def _head_norm(a, gain, seg_ref):
    sq = (a * a).astype(BF16)
    half = seg_ref.shape[0]
    ms = jnp.concatenate([_dot(sq[:, :half], seg_ref[...]), _dot(sq[:, half:], seg_ref[...])], axis=1)
    return a * lax.rsqrt(ms * (1.0 / NA_HEAD_DIM) + RMS_EPS) * gain


def _inproj_kernel(x_ref, mod_ref, g_ref, w_ref, qg_ref, kg_ref, seg_ref, f_ref, q_ref, k_ref, v_ref):
    tm = x_ref.shape[0]
    chains = max(1, tm // INPROJ_CHAIN_ROWS)
    tc = tm // chains

    def normed(i):
        x = x_ref[i * tc:(i + 1) * tc, :]
        return _norm_modulate(x, g_ref[...], mod_ref[:, 0:D_MODEL], mod_ref[:, D_MODEL:2 * D_MODEL]).astype(BF16)

    def outputs(i, p):
        rows = slice(i * tc, (i + 1) * tc)
        f_ref[rows, :] = p[:, :FOUR_WIDTH]
        q = p[:, FOUR_WIDTH:FOUR_WIDTH + NA_WIDTH]
        k = p[:, FOUR_WIDTH + NA_WIDTH:FOUR_WIDTH + 2 * NA_WIDTH]
        q_ref[rows, :] = (_head_norm(q, qg_ref[...], seg_ref) * (NA_HEAD_DIM ** -0.5 * LOG2_E)).astype(BF16)
        k_ref[rows, :] = _head_norm(k, kg_ref[...], seg_ref)
        v_ref[rows, :] = p[:, FOUR_WIDTH + 2 * NA_WIDTH:]

    u, p = {}, {}
    for step in range(chains + 2):
        if step < chains:
            u[step] = normed(step)
        if 0 <= step - 1 < chains:
            p[step - 1] = _dot(u.pop(step - 1), w_ref[...])
        if 0 <= step - 2 < chains:
            outputs(step - 2, p.pop(step - 2))


def _in_projection(x, mod_rows, row_of_batch, norm_g, w_in, q_g, k_g, seg, tm):
    b, n, _ = x.shape
    width = w_in.shape[1]
    tok = lambda bi, j: (bi, j, 0)
    const2 = lambda bi, j: (0, 0)
    out_tok = lambda w: pl.BlockSpec((None, tm, w), tok)
    return pl.pallas_call(
        _inproj_kernel,
        grid=(b, n // tm),
        in_specs=[
            pl.BlockSpec((None, tm, D_MODEL), tok),
            pl.BlockSpec((None, 1, N_MOD * D_MODEL), lambda bi, j: (row_of_batch(bi), 0, 0)),
            pl.BlockSpec((1, D_MODEL), const2),
            _resident((D_MODEL, width), const2),
            pl.BlockSpec((1, NA_WIDTH), const2),
            pl.BlockSpec((1, NA_WIDTH), const2),
            pl.BlockSpec(seg.shape, const2),
        ],
        out_specs=[out_tok(FOUR_WIDTH), out_tok(NA_WIDTH), out_tok(NA_WIDTH), out_tok(NA_WIDTH)],
        out_shape=[
            jax.ShapeDtypeStruct((b, n, FOUR_WIDTH), F32),
            jax.ShapeDtypeStruct((b, n, NA_WIDTH), BF16),
            jax.ShapeDtypeStruct((b, n, NA_WIDTH), F32),
            jax.ShapeDtypeStruct((b, n, NA_WIDTH), F32),
        ],
        compiler_params=_params("parallel", "parallel"),
        name="in_projection",
    )(x, mod_rows, norm_g, w_in, q_g, k_g, seg)


def _fft_tables(n):
    n1, n2 = FFT_N1, FFT_N2
    assert n1 * n2 == n
    k1 = np.arange(n1)[None, :, None]
    pos = (n2 * np.arange(n1)[None, None, :] + np.arange(n2)[:, None, None])
    ang = 2.0 * np.pi * ((k1 * pos) % n) / n
    l1 = np.concatenate([np.cos(ang), -np.sin(ang)], axis=1)
    a2 = 2.0 * np.pi * ((np.arange(n2)[:, None] * np.arange(n2)[None, :]) % n2) / n2
    c2, s2 = np.cos(a2), np.sin(a2)
    l2 = np.block([[c2, s2], [-s2, c2]])
    ac = 2.0 * np.pi * ((np.arange(FOUR_GC)[:, None] * np.arange(FOUR_GC)[None, :]) % FOUR_GC) / FOUR_GC
    cs = np.concatenate([np.cos(ac), np.sin(ac)], axis=0)
    return l1.astype(np.float32), l2.astype(np.float32), cs.astype(np.float32)


def _fft_kernel(f_ref, l1_ref, l2_ref, cs_ref, w_ref, o_ref, x_ref, y1_ref, z_ref, *, ortho):
    n1, n2 = FFT_N1, FFT_N2

    def repitch(i, carry):
        x_ref[pl.ds(pl.multiple_of(i * X_PITCH, SUBLANES), n2), :] = f_ref[pl.ds(pl.multiple_of(i * n2, n2), n2), :]
        return carry

    lax.fori_loop(0, n1, repitch, 0, unroll=FFT_UNROLL)

    def stage1(j, carry):
        r = x_ref[pl.ds(j, n1, stride=X_PITCH), :].astype(BF16)
        y1_ref[pl.ds(pl.multiple_of(j * Y1_PITCH, SUBLANES), 2 * n1), :] = _dot(l1_ref[j], r)
        return carry

    lax.fori_loop(0, n2, stage1, 0, unroll=FFT_UNROLL)

    l2 = l2_ref[...]

    def stage2(k1, carry):
        yr = y1_ref[pl.ds(k1, n2, stride=Y1_PITCH), :]
        yi = y1_ref[pl.ds(n1 + k1, n2, stride=Y1_PITCH), :]
        z = _dot(l2, jnp.concatenate([yr, yi], axis=0).astype(BF16))
        z_ref[0, pl.ds(k1, n2, stride=Z_PITCH), :] = z[:n2]
        z_ref[1, pl.ds(k1, n2, stride=Z_PITCH), :] = z[n2:]
        return carry

    lax.fori_loop(0, n1, stage2, 0, unroll=FFT_UNROLL)

    wf = (_dot_f32(cs_ref[...], w_ref[...]) * ortho).astype(BF16)
    wr, wi = wf[:FOUR_GC], wf[FOUR_GC:]

    def stage3(k2, carry):
        base = pl.multiple_of(k2 * Z_PITCH, SUBLANES)
        zr = z_ref[0, pl.ds(base, n1), :].astype(BF16)
        zi = z_ref[1, pl.ds(base, n1), :].astype(BF16)
        o_ref[pl.ds(pl.multiple_of(k2 * n1, n1), n1), :] = (_dot(zr, wr) + _dot(zi, wi)).astype(o_ref.dtype)
        return carry

    lax.fori_loop(0, n2, stage3, 0, unroll=FFT_UNROLL)


def _fourier_mix(f, w_four):
    b, n, _ = f.shape
    l1, l2, cs = _fft_tables(n)
    l1 = jnp.asarray(l1).astype(BF16)
    l2 = jnp.asarray(l2).astype(BF16)
    cs = jnp.asarray(cs)
    ortho = float(1.0 / np.sqrt(n * FOUR_GC))
    slab = lambda bi, g: (bi, 0, g)
    return pl.pallas_call(
        functools.partial(_fft_kernel, ortho=ortho),
        grid=(b, FOUR_GROUPS),
        in_specs=[
            pl.BlockSpec((None, n, FOUR_GC), slab),
            _resident(l1.shape, lambda bi, g: (0, 0, 0)),
            _resident(l2.shape, lambda bi, g: (0, 0)),
            _resident(cs.shape, lambda bi, g: (0, 0)),
            pl.BlockSpec((None, FOUR_GC, FOUR_GC), lambda bi, g: (g, 0, 0)),
        ],
        out_specs=pl.BlockSpec((None, n, FOUR_GC), slab),
        out_shape=jax.ShapeDtypeStruct((b, n, FOUR_WIDTH), BF16),
        scratch_shapes=[
            pltpu.VMEM((FFT_N1 * X_PITCH, FOUR_GC), F32),
            pltpu.VMEM((FFT_N2 * Y1_PITCH, FOUR_GC), F32),
            pltpu.VMEM((2, FFT_N2 * Z_PITCH, FOUR_GC), F32),
        ],
        compiler_params=_params("parallel", "parallel"),
        name="fourier_mix",
    )(f, l1, l2, cs, w_four)


def _band_start(g, rows):
    return jnp.clip(NA_Q_ROWS * g - (NA_BAND_ROWS - NA_Q_ROWS) // 2, 0, rows - NA_BAND_ROWS)


def _patch_col_start(cb):
    return int(np.clip(NA_Q_COLS * cb - (NA_K_COLS - NA_Q_COLS) // 2, 0, GRID_W - NA_K_COLS))


def _natten_bias_tables(rpb, rows):
    heads = rpb.shape[0]
    groups = rows // NA_Q_ROWS
    n_dr, n_dc = 2 * NA_WIN_ROWS - 1, 2 * NA_WIN_COLS - 1
    n_cb = GRID_W // NA_Q_COLS
    nq, nk = NA_Q_ROWS * NA_Q_COLS, NA_BAND_ROWS * NA_K_COLS
    lane_tiles = LANES // NA_K_COLS
    lead = NA_WIN_ROWS
    col = np.arange(GRID_W)
    col_start = np.clip(col - NA_WIN_COLS // 2, 0, GRID_W - NA_WIN_COLS)
    col_ok = (col[None, :] >= col_start[:, None]) & (col[None, :] < col_start[:, None] + NA_WIN_COLS)
    dc = np.clip(col[None, :] - col[:, None] + (NA_WIN_COLS - 1), 0, n_dc - 1)
    onehot_dc = (dc[None] == np.arange(n_dc)[:, None, None]).astype(np.float32)
    tiles = jnp.einsum('hre,eqk->hrqk', rpb.astype(F32), jnp.asarray(onehot_dc), precision=lax.Precision.HIGHEST)
    tiles = jnp.where(jnp.asarray(col_ok), tiles * LOG2_E, MASK_VALUE)
    total_tiles = -(-(lead + n_dr + NA_BAND_ROWS) // lane_tiles) * lane_tiles
    tail = total_tiles - lead - n_dr
    tiles = jnp.pad(tiles, ((0, 0), (lead, tail), (0, 0), (0, 0)), constant_values=MASK_VALUE)
    strips = []
    for cb in range(n_cb):
        q0, k0 = NA_Q_COLS * cb, _patch_col_start(cb)
        assert col_start[q0] >= k0 and col_start[q0 + NA_Q_COLS - 1] + NA_WIN_COLS <= k0 + NA_K_COLS
        sub = tiles[:, :, q0:q0 + NA_Q_COLS, k0:k0 + NA_K_COLS].transpose(0, 2, 1, 3)
        strips.append(sub.reshape(heads, NA_Q_COLS, total_tiles * NA_K_COLS))
    strips = jnp.stack(strips, axis=1)

    plan = []
    for kind, g in enumerate((0, groups // 2, groups - 1)):
        band0 = int(np.clip(NA_Q_ROWS * g - (NA_BAND_ROWS - NA_Q_ROWS) // 2, 0, rows - NA_BAND_ROWS))
        for a in range(NA_Q_ROWS):
            qr = NA_Q_ROWS * g + a
            row_start = int(np.clip(qr - NA_WIN_ROWS // 2, 0, rows - NA_WIN_ROWS))
            assert band0 <= row_start and row_start + NA_WIN_ROWS <= band0 + NA_BAND_ROWS
            first_tile = band0 - qr + NA_WIN_ROWS - 1 + lead
            assert 0 <= first_tile and first_tile + NA_BAND_ROWS <= total_tiles
            for cb in range(n_cb):
                for hd in range(2):
                    plan.append((kind * n_cb + cb, hd, cb, a, first_tile, row_start - band0))

    def build(strip_ref, o_ref):
        patch_row = lax.broadcasted_iota(jnp.int32, (1, nk), 1) // NA_K_COLS
        for idx, hd, cb, a, first_tile, first_row in plan:
            lane0 = first_tile * NA_K_COLS
            val = strip_ref[hd, cb, :, lane0:lane0 + nk]
            in_window = (patch_row >= first_row) & (patch_row < first_row + NA_WIN_ROWS)
            r0 = hd * nq + a * NA_Q_COLS
            o_ref[idx, r0:r0 + NA_Q_COLS, :] = jnp.where(in_window, val, MASK_VALUE)

    return pl.pallas_call(
        build,
        grid=(heads // 2,),
        in_specs=[pl.BlockSpec((2,) + strips.shape[1:], lambda hp: (hp, 0, 0, 0))],
        out_specs=pl.BlockSpec((None, 3 * n_cb, 2 * nq, nk), lambda hp: (hp, 0, 0, 0)),
        out_shape=jax.ShapeDtypeStruct((heads // 2, 3 * n_cb, 2 * nq, nk), F32),
        compiler_params=_params("parallel"),
        name="bias_table",
    )(strips)


def _natten_kernel(q_ref, k_ref, v_ref, kc_ref, vc_ref, tab_ref, o_ref, *, rows):
    nq, nk = NA_Q_ROWS * NA_Q_COLS, NA_BAND_ROWS * NA_K_COLS
    groups = rows // NA_Q_ROWS
    n_cb = GRID_W // NA_Q_COLS
    first_head = lax.broadcasted_iota(jnp.int32, (1, 2 * NA_HEAD_DIM), 1) < NA_HEAD_DIM
    kc, vc = kc_ref[...].astype(BF16), vc_ref[...].astype(BF16)
    chains = [(g, cb) for g in range(NA_GROUPS_PER_STEP) for cb in range(n_cb)]

    def group_index(g):
        return pl.program_id(2) * NA_GROUPS_PER_STEP + g

    def q_rows(g, cb, a):
        start = (g * NA_Q_ROWS + a) * GRID_W + cb * NA_Q_COLS
        return slice(start, start + NA_Q_COLS)

    def patch(ref, g, cb):
        tok0 = _band_start(group_index(g), rows) * GRID_W + _patch_col_start(cb)
        parts = [ref[pl.ds(pl.multiple_of(tok0 + bb * GRID_W, SUBLANES), NA_K_COLS), :] for bb in range(NA_BAND_ROWS)]
        return jnp.concatenate(parts, axis=0).astype(BF16)

    def scores(g, cb):
        gi = group_index(g)
        kind = jnp.where(gi == 0, 0, jnp.where(gi == groups - 1, 2, 1))
        q2 = jnp.concatenate([q_ref[q_rows(g, cb, a), :] for a in range(NA_Q_ROWS)], axis=0)
        zero = jnp.zeros_like(q2)
        qs = jnp.concatenate([jnp.where(first_head, q2, zero), jnp.where(first_head, zero, q2)], axis=0)
        s_all = _dot_nt(qs, jnp.concatenate([patch(k_ref, g, cb), kc], axis=0))
        return s_all[:, :nk] + tab_ref[kind * n_cb + cb], s_all[:, nk:]

    def softmax(s_win, s_ctx):
        s_all = jnp.concatenate([s_win, s_ctx], axis=1)
        p = jnp.exp2(s_all - jnp.max(s_all, axis=-1, keepdims=True))
        return p.astype(BF16), jnp.sum(p, axis=-1, keepdims=True)

    def weighted_values(g, cb, p, l):
        values = jnp.concatenate([patch(v_ref, g, cb), vc], axis=0)
        o = _dot(p, values) / l
        o = jnp.where(first_head, o[:nq], o[nq:]).astype(o_ref.dtype)
        for a in range(NA_Q_ROWS):
            o_ref[q_rows(g, cb, a), :] = o[a * NA_Q_COLS:(a + 1) * NA_Q_COLS]

    s, p = {}, {}
    for step in range(len(chains) + 2):
        if step < len(chains):
            s[step] = scores(*chains[step])
        if 0 <= step - 1 < len(chains):
            p[step - 1] = softmax(*s.pop(step - 1))
        if 0 <= step - 2 < len(chains):
            weighted_values(*chains[step - 2], *p.pop(step - 2))


def _neighbourhood_attention(q, k, v, kc, vc, rpb):
    b, n, _ = q.shape
    rows = n // GRID_W
    steps = rows // (NA_Q_ROWS * NA_GROUPS_PER_STEP)
    ctx_len = kc.shape[1]
    tq = NA_GROUPS_PER_STEP * NA_Q_ROWS * GRID_W
    tables = _natten_bias_tables(rpb, rows)
    lanes = 2 * NA_HEAD_DIM
    seq = lambda bi, hp, i: (bi, 0, hp)
    return pl.pallas_call(
        functools.partial(_natten_kernel, rows=rows),
        grid=(b, NA_HEADS // 2, steps),
        in_specs=[
            pl.BlockSpec((None, tq, lanes), lambda bi, hp, i: (bi, i, hp)),
            pl.BlockSpec((None, n, lanes), seq),
            pl.BlockSpec((None, n, lanes), seq),
            pl.BlockSpec((None, ctx_len, lanes), seq),
            pl.BlockSpec((None, ctx_len, lanes), seq),
            pl.BlockSpec((None,) + tables.shape[1:], lambda bi, hp, i: (hp, 0, 0, 0)),
        ],
        out_specs=pl.BlockSpec((None, tq, lanes), lambda bi, hp, i: (bi, i, hp)),
        out_shape=jax.ShapeDtypeStruct((b, n, NA_WIDTH), BF16),
        compiler_params=_params("parallel", "parallel", "arbitrary"),
        name="neighbourhood_attention",
    )(q, k, v, kc, vc, tables)


def _mod_chunk(mod_ref, idx):
    return mod_ref[:, idx * D_MODEL:(idx + 1) * D_MODEL]


def _mlp_inputs(h, y, mod_ref, g2n_ref):
    h1 = h + _mod_chunk(mod_ref, 2) * y
    u = _norm_modulate(h1, g2n_ref[...], _mod_chunk(mod_ref, 3), _mod_chunk(mod_ref, 4)).astype(BF16)
    return h1, u


def _mlp_chunk(u, acc, c, w1_ref, w2_ref):
    lo, hi = c * MLP_CHUNK, (c + 1) * MLP_CHUNK
    a = jnp.maximum(_dot(u, w1_ref[:, lo:hi]), 0.0)
    part = _dot((a * a).astype(BF16), w2_ref[lo:hi, :])
    return part if acc is None else acc + part


def _pad_stages(stages, sizes=None):
    if sizes is None:
        sizes = [1] * (len(stages) - 1) + [0] * (MLP_CHUNKS - len(stages)) + [1]
    assert len(sizes) == MLP_CHUNKS and sum(sizes) == len(stages)

    def merged(group):
        def run():
            out = None
            for stage in group:
                out = stage()
            return out
        return run

    starts = [sum(sizes[:k]) for k in range(MLP_CHUNKS)]
    return [merged(stages[s0:s0 + n]) for s0, n in zip(starts, sizes)]


def _chunk_bounds(tm):
    assert sum(TAIL_CHUNK_ROWS) == tm
    starts = [sum(TAIL_CHUNK_ROWS[:i]) for i in range(len(TAIL_CHUNK_ROWS))]
    return list(zip(starts, TAIL_CHUNK_ROWS))


def _run_chains(prologues, bounds, mod_ref, w1_ref, w2_ref, o_ref):
    n = len(prologues)
    for stage in prologues[0][:-1]:
        stage()
    h1, u = prologues[0][-1]()
    for i, (r0, tc) in enumerate(bounds):
        acc, nxt = None, None
        for c in range(MLP_CHUNKS):
            acc = _mlp_chunk(u, acc, c, w1_ref, w2_ref)
            if i + 1 < n:
                nxt = prologues[i + 1][c]()
        o_ref[r0:r0 + tc, :] = h1 + _mod_chunk(mod_ref, 5) * acc
        if i + 1 < n:
            h1, u = nxt


def _even_tail_kernel(h_ref, of_ref, on_ref, mod_ref, g2n_ref, wo_hbm, w1_hbm, w2_hbm, o_ref,
                      wo_ref, w1_ref, w2_ref, wo_stage, w1_stage, w2_stage, sem, *, layer):
    _load_weights_once([wo_hbm.at[layer // 2], w1_hbm.at[layer], w2_hbm.at[layer]], [wo_ref, w1_ref, w2_ref],
                       [wo_stage, w1_stage, w2_stage], sem)
    bounds = _chunk_bounds(h_ref.shape[0])

    def prologue(r0, tc):
        rows = slice(r0, r0 + tc)
        st = {}

        def s0():
            st["y"] = _dot(of_ref[rows, :], wo_ref[:FOUR_WIDTH, :])

        def s1():
            st["y"] = st["y"] + _dot(on_ref[rows, :], wo_ref[FOUR_WIDTH:, :])

        def s3():
            return _mlp_inputs(h_ref[rows, :], st["y"], mod_ref, g2n_ref)

        return _pad_stages([s0, s1, s3])

    _run_chains([prologue(r0, tc) for r0, tc in bounds], bounds, mod_ref, w1_ref, w2_ref, o_ref)


def _even_tail(h, o_four, o_na, mod_rows, w_out, norm2_g, w1, w2, layer, tm):
    b, n, _ = h.shape
    tok = lambda bi, j: (bi, j, 0)
    const2 = lambda bi, j: (0, 0)
    return pl.pallas_call(
        functools.partial(_even_tail_kernel, layer=layer),
        grid=(b, n // tm),
        in_specs=[
            pl.BlockSpec((None, tm, D_MODEL), tok),
            pl.BlockSpec((None, tm, FOUR_WIDTH), tok),
            pl.BlockSpec((None, tm, NA_WIDTH), tok),
            pl.BlockSpec((None, 1, N_MOD * D_MODEL), lambda bi, j: (bi, 0, 0)),
            pl.BlockSpec((1, D_MODEL), const2),
            pl.BlockSpec(memory_space=pl.ANY),
            pl.BlockSpec(memory_space=pl.ANY),
            pl.BlockSpec(memory_space=pl.ANY),
        ],
        out_specs=pl.BlockSpec((None, tm, D_MODEL), tok),
        out_shape=jax.ShapeDtypeStruct(h.shape, F32),
        scratch_shapes=_weight_scratch([w_out.shape[1:], w1.shape[1:], w2.shape[1:]]),
        compiler_params=_params("arbitrary", "arbitrary"),
        name="even_tail",
    )(h, o_four, o_na, mod_rows, norm2_g, w_out, w1, w2)


HALO = 8


def _shift_rows(a, s):
    return pltpu.roll(a, s % a.shape[0], 0)


def _pool_kernel(hp_ref, h_ref, hn_ref, mod_ref, g1n_ref, ps_ref, g2n_ref, wp_hbm, w1_hbm, w2_hbm, o_ref,
                 wp_ref, w1_ref, w2_ref, wp_stage, w1_stage, w2_stage, sem, *, seq_len, layer):
    _load_weights_once([wp_hbm.at[layer // 2], w1_hbm.at[layer], w2_hbm.at[layer]], [wp_ref, w1_ref, w2_ref],
                       [wp_stage, w1_stage, w2_stage], sem)
    j = pl.program_id(1)
    tm = h_ref.shape[0]
    bounds = _chunk_bounds(tm)
    gc = POOL_GC
    gain, shift, scale = g1n_ref[...], _mod_chunk(mod_ref, 0), _mod_chunk(mod_ref, 1)
    unorm = lambda x: _norm_modulate(x, gain, shift, scale)

    def prologue(r0, tc):
        st = {}

        def normed():
            st["u"] = unorm(h_ref[r0:r0 + tc, :])

        def with_halo():
            if r0 == 0:
                u_prev = jnp.where(j > 0, unorm(hp_ref[...]), 0.0)
            else:
                u_prev = unorm(h_ref[r0 - HALO:r0, :])
            if r0 + tc == tm:
                u_next = jnp.where(j < pl.num_programs(1) - 1, unorm(hn_ref[...]), 0.0)
            else:
                u_next = unorm(h_ref[r0 + tc:r0 + tc + HALO, :])
            st["ue"] = jnp.concatenate([u_prev, st["u"], u_next], axis=0)

        def sum2():
            st["s2"] = st["ue"] + _shift_rows(st["ue"], 1)

        def sum4():
            s2 = st["s2"][:, gc:]
            st["s4"] = _shift_rows(s2, 1) + _shift_rows(s2, -1)

        def sum8_16():
            s4 = st["s4"][:, gc:]
            s8 = _shift_rows(s4, 2) + _shift_rows(s4, -2)
            s16 = _shift_rows(s8[:, gc:], 4) + _shift_rows(s8[:, gc:], -4)
            st["sums"] = (st["s2"][:, :gc], st["s4"][:, :gc], s8[:, :gc], s16)

        def group_linear(groups):
            def run():
                t = j * tm + r0 + lax.broadcasted_iota(jnp.int32, (tc, 1), 0)
                for gi in groups:
                    w = POOL_SIZES[gi]
                    lo = jnp.maximum(t - w // 2, 0)
                    hi = jnp.minimum(t + w - w // 2, seq_len)
                    inv_cnt = 1.0 / (hi - lo).astype(F32)
                    pooled = st["sums"][gi][HALO:HALO + tc] * inv_cnt - st["u"][:, gi * gc:(gi + 1) * gc]
                    st["y%d" % gi] = _dot(pooled.astype(BF16), wp_ref[gi * gc:(gi + 1) * gc, :]) * ps_ref[:, gi * gc:(gi + 1) * gc]
            return run

        def mlp_inputs():
            y = jnp.concatenate([st["y%d" % gi] for gi in range(len(POOL_SIZES))], axis=1)
            return _mlp_inputs(h_ref[r0:r0 + tc, :], y, mod_ref, g2n_ref)

        return _pad_stages([normed, with_halo, sum2, sum4, sum8_16, group_linear((0, 1)), group_linear((2, 3)),
                            mlp_inputs], sizes=POOL_STAGE_SIZES)

    _run_chains([prologue(r0, tc) for r0, tc in bounds], bounds, mod_ref, w1_ref, w2_ref, o_ref)


def _odd_layer(h, mod_rows, norm1_g, w_pool, pool_scale, norm2_g, w1, w2, layer, tm):
    b, n, _ = h.shape
    tok = lambda bi, j: (bi, j, 0)
    const2 = lambda bi, j: (0, 0)
    per_tile = tm // HALO
    last = n // HALO - 1
    return pl.pallas_call(
        functools.partial(_pool_kernel, seq_len=n, layer=layer),
        grid=(b, n // tm),
        in_specs=[
            pl.BlockSpec((None, HALO, D_MODEL), lambda bi, j: (bi, jnp.maximum(j * per_tile - 1, 0), 0)),
            pl.BlockSpec((None, tm, D_MODEL), tok),
            pl.BlockSpec((None, HALO, D_MODEL), lambda bi, j: (bi, jnp.minimum((j + 1) * per_tile, last), 0)),
            pl.BlockSpec((None, 1, N_MOD * D_MODEL), lambda bi, j: (bi, 0, 0)),
            pl.BlockSpec((1, D_MODEL), const2),
            pl.BlockSpec((1, D_MODEL), const2),
            pl.BlockSpec((1, D_MODEL), const2),
            pl.BlockSpec(memory_space=pl.ANY),
            pl.BlockSpec(memory_space=pl.ANY),
            pl.BlockSpec(memory_space=pl.ANY),
        ],
        out_specs=pl.BlockSpec((None, tm, D_MODEL), tok),
        out_shape=jax.ShapeDtypeStruct(h.shape, F32),
        scratch_shapes=_weight_scratch([w_pool.shape[1:], w1.shape[1:], w2.shape[1:]]),
        compiler_params=_params("arbitrary", "arbitrary"),
        name="odd_layer",
    )(h, h, h, mod_rows, norm1_g, pool_scale, norm2_g, w_pool, w1, w2)


def _head_segments():
    lanes = 256
    seg = (np.arange(lanes)[:, None] // NA_HEAD_DIM) == (np.arange(lanes)[None, :] // NA_HEAD_DIM)
    return jnp.asarray(seg.astype(np.float32)).astype(BF16)


def kernel(x, c, ctx, c_ctx, w_mod, b_mod, norm1_g, norm2_g, w_in_even, w_four, q_norm_g, k_norm_g, rpb, w_out_even, w_pool, pool_scale, w_mlp1, w_mlp2):
    b = x.shape[0]
    depth = w_mod.shape[0]
    assert depth == 2, "layer 0 is the even (Fourier + attention) layer, layer 1 the odd (pooling) layer"
    ctx_row = b
    pad = (-(b + 1)) % SUBLANES
    c_rows = jnp.concatenate([c, c_ctx[None, :], jnp.zeros((pad, D_MODEL), F32)], axis=0)
    mod = _modulation(c_rows, w_mod, b_mod)
    mod = mod.reshape(depth, mod.shape[1], 1, N_MOD * D_MODEL)

    row = lambda a: a.reshape(1, -1)
    heads = lambda g: jnp.tile(g, NA_HEADS).reshape(1, NA_WIDTH)
    seg = _head_segments()

    w_in = w_in_even[0].astype(BF16)
    qg, kg = heads(q_norm_g[0]), heads(k_norm_g[0])
    f, q, k, v = _in_projection(x, mod[0], lambda bi: bi, row(norm1_g[0]), w_in, qg, kg, seg, tm=TOKEN_TILE)
    _, _, kc, vc = _in_projection(ctx, mod[0], lambda bi: ctx_row, row(norm1_g[0]), w_in, qg, kg, seg,
                                  tm=ctx.shape[1])
    o_four = _fourier_mix(f, w_four[0])
    o_na = _neighbourhood_attention(q, k, v, kc, vc, rpb[0])
    h = _even_tail(x, o_four, o_na, mod[0], w_out_even, row(norm2_g[0]), w_mlp1, w_mlp2, layer=0, tm=TOKEN_TILE)

    w_pool_rows = w_pool.reshape(w_pool.shape[0], len(POOL_SIZES) * POOL_GC, POOL_GC)
    h = _odd_layer(h, mod[1], row(norm1_g[1]), w_pool_rows, row(pool_scale[0]), row(norm2_g[1]), w_mlp1, w_mlp2,
                   layer=1, tm=TOKEN_TILE)
    return h
```
